```python
import math
import jax, jax.numpy as jnp
from jax import lax
import numpy as np

D_MODEL = 2048
BATCH = 4
SEQ = 4096
DEPTH = 4

HEAD_DIM = 64
N_MIXERS = 4
HEADS_PER_MIXER = 8
GROUP_WIDTH = HEADS_PER_MIXER * HEAD_DIM
MIX_WIDTH = N_MIXERS * GROUP_WIDTH
N_HEADS_TOTAL = N_MIXERS * HEADS_PER_MIXER
SCALE = HEAD_DIM ** -0.5

REL_BUCKETS = 32
REL_MAX_DIST = 2048

DILATED_CONFIGS = ((128, 1), (512, 4), (2048, 16))

SWA_WINDOW = 128
SWA_KV_HEADS = 2

MOBA_BLOCK = 256
MOBA_TOPK = 3

NSA_KV_HEADS = 2
NSA_CMP_BLOCK = 32
NSA_CMP_STRIDE = 16
NSA_CMP_HIDDEN = 128
NSA_SLC_BLOCK = 64
NSA_SLC_TOPN = 16
NSA_WINDOW = 512

BAND_BLOCK = 128
QUERY_CHUNK = 64

N_EXPERTS = 16
N_EXPERT_GROUPS = 4
TOPK_GROUPS = 1
TOPK_EXPERTS = 2
D_EXPERT = 512

NORM_EPS = 1e-6
NEG_INF = -1e30

KV_B = SWA_KV_HEADS * HEAD_DIM
KV_D = NSA_KV_HEADS * HEAD_DIM
IN_SIZES = (GROUP_WIDTH, GROUP_WIDTH, GROUP_WIDTH,
            GROUP_WIDTH, KV_B, KV_B,
            GROUP_WIDTH, GROUP_WIDTH, GROUP_WIDTH,
            GROUP_WIDTH, KV_D, KV_D, KV_D, KV_D, KV_D, KV_D, HEADS_PER_MIXER * 3)
IN_COLS = sum(IN_SIZES)

kernel_name = "hybrid_parallel_heads_moe"


def rmsnorm(x, g):
    xf = x.astype(jnp.float32)
    ms = jnp.mean(xf * xf, axis=-1, keepdims=True)
    return (xf * lax.rsqrt(ms + NORM_EPS)).astype(x.dtype) * g


def t5_bucket(dist):
    dist = jnp.maximum(dist, 0)
    max_exact = REL_BUCKETS // 2
    large = max_exact + (jnp.log(jnp.maximum(dist, 1).astype(jnp.float32) / max_exact)
                         / math.log(REL_MAX_DIST / max_exact) * (REL_BUCKETS - max_exact)).astype(jnp.int32)
    large = jnp.minimum(large, REL_BUCKETS - 1)
    return jnp.where(dist < max_exact, dist, large)


def masked_softmax(s, mask, sink=None):
    s = jnp.where(mask, s, NEG_INF)
    m = jnp.max(s, axis=-1, keepdims=True)
    if sink is not None:
        m = jnp.maximum(m, sink)
    e = jnp.where(mask, jnp.exp(s - m), 0.0)
    den = jnp.sum(e, axis=-1, keepdims=True)
    if sink is not None:
        den = den + jnp.exp(sink - m)
    den = jnp.maximum(den, jnp.finfo(jnp.float32).tiny)
    return e / den, (m + jnp.log(den))[..., 0]


def banded_attention(q, k, v, max_dist, tab, dist_scale, sink=None):
    b, hk, g, L, hd = q.shape
    nblk = -(-L // BAND_BLOCK)
    Lp = nblk * BAND_BLOCK
    nprev = -(-max_dist // BAND_BLOCK)
    nk = (nprev + 1) * BAND_BLOCK
    q = jnp.pad(q, ((0, 0), (0, 0), (0, 0), (0, Lp - L), (0, 0)))
    pad_kv = ((0, 0), (0, 0), (nprev * BAND_BLOCK, Lp - L), (0, 0))
    k = jnp.pad(k, pad_kv)
    v = jnp.pad(v, pad_kv)
    kidx = np.arange(nblk)[:, None] * BAND_BLOCK + np.arange(nk)[None, :]
    kb = k[:, :, kidx]
    vb = v[:, :, kidx]
    qb = q.reshape(b, hk, g, nblk, BAND_BLOCK, hd)
    rel = np.arange(BAND_BLOCK)[:, None] + nprev * BAND_BLOCK - np.arange(nk)[None, :]
    kpos = np.arange(nblk)[:, None, None] * BAND_BLOCK - nprev * BAND_BLOCK + np.arange(nk)[None, None, :]
    mask = (rel >= 0) & (rel <= max_dist) & (kpos >= 0)
    bias = tab.astype(jnp.float32)[:, :, t5_bucket(jnp.asarray(rel * dist_scale))]
    s = jnp.einsum('bhgnqd,bhnkd->bhgnqk', qb, kb).astype(jnp.float32) * SCALE + bias[None, :, :, None]
    if sink is not None:
        sink = sink.astype(jnp.float32)[None, :, :, None, None, None]
    p, lse = masked_softmax(s, mask, sink)
    o = jnp.einsum('bhgnqk,bhnkd->bhgnqd', p.astype(v.dtype), vb).reshape(b, hk, g, Lp, hd)[:, :, :, :L]
    return o, lse.reshape(b, hk, g, Lp)[:, :, :, :L]


def dilated_attention(q, k, v, tab):
    b, h, s, hd = q.shape
    outs, lses = [], []
    for window, r in DILATED_CONFIGS:
        def strided(t):
            return t.reshape(b, h, s // r, r, hd).transpose(0, 3, 1, 2, 4).reshape(b * r, h, s // r, hd)
        o, lse = banded_attention(strided(q)[:, :, None], strided(k), strided(v), window // r, tab[:, None], r)
        outs.append(o[:, :, 0].reshape(b, r, h, s // r, hd).transpose(0, 2, 3, 1, 4).reshape(b, h, s, hd))
        lses.append(lse[:, :, 0].reshape(b, r, h, s // r).transpose(0, 2, 3, 1).reshape(b, h, s))
    w = jax.nn.softmax(jnp.stack(lses, axis=0), axis=0)
    o = w[0][..., None] * outs[0] + w[1][..., None] * outs[1] + w[2][..., None] * outs[2]
    return o.astype(q.dtype)


def moba_attention(q, k, v, tab):
    b, h, s, hd = q.shape
    nb = -(-s // MOBA_BLOCK)
    sp = nb * MOBA_BLOCK
    pad = ((0, 0), (0, 0), (0, sp - s), (0, 0))
    q, k, v = jnp.pad(q, pad), jnp.pad(k, pad), jnp.pad(v, pad)
    kb = k.reshape(b, h, nb, MOBA_BLOCK, hd)
    vb = v.reshape(b, h, nb, MOBA_BLOCK, hd)
    kmean = jnp.mean(kb, axis=-2)
    nch = sp // QUERY_CHUNK
    qc = jnp.moveaxis(q.reshape(b, h, nch, QUERY_CHUNK, hd), 2, 0)
    starts = jnp.arange(nch, dtype=jnp.int32) * QUERY_CHUNK
    ksel = min(MOBA_TOPK, nb)
    bi = jnp.arange(b)[:, None, None, None]
    hi = jnp.arange(h)[None, :, None, None]
    tabf = tab.astype(jnp.float32)

    def chunk(args):
        qi, start = args
        pos = start + jnp.arange(QUERY_CHUNK, dtype=jnp.int32)
        cur = start // MOBA_BLOCK
        gate = jnp.einsum('bhqd,bhnd->bhqn', qi, kmean).astype(jnp.float32)
        gate = jnp.where(jnp.arange(nb) < cur, gate, NEG_INF)
        _, sel = lax.top_k(gate, ksel)
        ks = kb[bi, hi, sel]
        vs = vb[bi, hi, sel]
        s_sel = jnp.einsum('bhqd,bhqnkd->bhqnk', qi, ks).reshape(b, h, QUERY_CHUNK, ksel * MOBA_BLOCK)
        kpos_sel = (sel[..., None] * MOBA_BLOCK + jnp.arange(MOBA_BLOCK)).reshape(b, h, QUERY_CHUNK, ksel * MOBA_BLOCK)
        mask_sel = jnp.repeat(sel < cur, MOBA_BLOCK, axis=-1)
        ko = lax.dynamic_index_in_dim(kb, cur, axis=2, keepdims=False)
        vo = lax.dynamic_index_in_dim(vb, cur, axis=2, keepdims=False)
        s_own = jnp.einsum('bhqd,bhkd->bhqk', qi, ko)
        kpos_own = cur * MOBA_BLOCK + jnp.arange(MOBA_BLOCK, dtype=jnp.int32)
        shp = (b, h, QUERY_CHUNK, MOBA_BLOCK)
        mask_own = jnp.broadcast_to(kpos_own[None, :] <= pos[:, None], shp)
        kpos = jnp.concatenate([kpos_sel, jnp.broadcast_to(kpos_own, shp)], axis=-1)
        mask = jnp.concatenate([mask_sel, mask_own], axis=-1)
        bias = tabf[hi, t5_bucket(pos[:, None] - kpos)]
        sc = jnp.concatenate([s_sel, s_own], axis=-1).astype(jnp.float32) * SCALE + bias
        p, _ = masked_softmax(sc, mask)
        p = p.astype(v.dtype)
        o = (jnp.einsum('bhqnk,bhqnkd->bhqd', p[..., :ksel * MOBA_BLOCK].reshape(b, h, QUERY_CHUNK, ksel, MOBA_BLOCK), vs)
             + jnp.einsum('bhqk,bhkd->bhqd', p[..., ksel * MOBA_BLOCK:], vo))
        return o

    o = lax.map(chunk, (qc, starts))
    return jnp.moveaxis(o, 0, 2).reshape(b, h, sp, hd)[:, :, :s]


def nsa_attention(q, k_cmp, v_cmp, k_slc, v_slc, k_win, v_win, gates, pe_k, pe_v, w1k, w2k, w1v, w2v, tab):
    b, hk, g, s, hd = q.shape
    n_cmp = (s - NSA_CMP_BLOCK) // NSA_CMP_STRIDE + 1
    cidx = np.arange(n_cmp)[:, None] * NSA_CMP_STRIDE + np.arange(NSA_CMP_BLOCK)[None, :]

    def compress(t, pe, w1, w2):
        blk = (t[:, :, cidx] + pe).reshape(b, hk, n_cmp, NSA_CMP_BLOCK * hd)
        return jax.nn.gelu(blk @ w1) @ w2

    kc = compress(k_cmp, pe_k, w1k, w2k)
    vc = compress(v_cmp, pe_v, w1v, w2v)
    pos = np.arange(s)
    mask_c = (np.arange(n_cmp) * NSA_CMP_STRIDE + NSA_CMP_BLOCK - 1)[None, :] <= pos[:, None]
    s_c = jnp.einsum('bhgqd,bhnd->bhgqn', q, kc).astype(jnp.float32) * SCALE
    p_c, _ = masked_softmax(s_c, mask_c)
    o_cmp = jnp.einsum('bhgqn,bhnd->bhgqd', p_c.astype(vc.dtype), vc)
    n_slc = s // NSA_SLC_BLOCK
    c_start = np.arange(n_cmp) * NSA_CMP_STRIDE
    j_start = np.arange(n_slc) * NSA_SLC_BLOCK
    overlap = ((c_start[:, None] < j_start[None, :] + NSA_SLC_BLOCK)
               & (c_start[:, None] + NSA_CMP_BLOCK > j_start[None, :])).astype(np.float32)
    imp = jnp.einsum('bhgqn,nj->bhqj', p_c, jnp.asarray(overlap))
    cur = pos // NSA_SLC_BLOCK
    jj = np.arange(n_slc)[None, :]
    forced = (jj == 0) | (jj == cur[:, None]) | (jj == cur[:, None] - 1)
    imp = jnp.where(forced, -NEG_INF, jnp.where(jj <= cur[:, None], imp, NEG_INF))
    n_sel = min(NSA_SLC_TOPN, n_slc)
    _, sel = lax.top_k(imp, n_sel)
    ksb = k_slc.reshape(b, hk, n_slc, NSA_SLC_BLOCK, hd)
    vsb = v_slc.reshape(b, hk, n_slc, NSA_SLC_BLOCK, hd)
    nch = s // QUERY_CHUNK
    qc = jnp.moveaxis(q.reshape(b, hk, g, nch, QUERY_CHUNK, hd), 3, 0)
    selc = jnp.moveaxis(sel.reshape(b, hk, nch, QUERY_CHUNK, n_sel), 2, 0)
    starts = jnp.arange(nch, dtype=jnp.int32) * QUERY_CHUNK
    bi = jnp.arange(b)[:, None, None, None]
    hi = jnp.arange(hk)[None, :, None, None]
    tab_t = tab.astype(jnp.float32).transpose(0, 2, 1)

    def chunk(args):
        qi, si, start = args
        pos_q = start + jnp.arange(QUERY_CHUNK, dtype=jnp.int32)
        kk = ksb[bi, hi, si]
        vv = vsb[bi, hi, si]
        nkeys = n_sel * NSA_SLC_BLOCK
        sc = jnp.einsum('bhgqd,bhqnkd->bhgqnk', qi, kk).reshape(b, hk, g, QUERY_CHUNK, nkeys).astype(jnp.float32) * SCALE
        kpos = (si[..., None] * NSA_SLC_BLOCK + jnp.arange(NSA_SLC_BLOCK)).reshape(b, hk, QUERY_CHUNK, nkeys)
        mask = (kpos <= pos_q[:, None])[:, :, None]
        bias = jnp.moveaxis(tab_t[hi, t5_bucket(pos_q[:, None] - kpos)], -1, 2)
        p, _ = masked_softmax(sc + bias, mask)
        return jnp.einsum('bhgqnk,bhqnkd->bhgqd',
                          p.reshape(b, hk, g, QUERY_CHUNK, n_sel, NSA_SLC_BLOCK).astype(vv.dtype), vv)

    o_slc = jnp.moveaxis(lax.map(chunk, (qc, selc, starts)), 0, 3).reshape(b, hk, g, s, hd)
    o_win, _ = banded_attention(q, k_win, v_win, NSA_WINDOW - 1, tab, 1)
    gates = gates.astype(q.dtype)
    return gates[..., 0:1] * o_cmp + gates[..., 1:2] * o_slc + gates[..., 2:3] * o_win


def _heads(t, n):
    b, s, _ = t.shape
    return t.reshape(b, s, n, HEAD_DIM).transpose(0, 2, 1, 3)


def _seq(o):
    b, n, s, d = o.shape
    return o.transpose(0, 2, 1, 3).reshape(b, s, n * d)


def token_mixer(h, w_in_l, rel_bias, sink_l, pe_k, pe_v, w1k, w2k, w1v, w2v, mix_g, w_out_l):
    b, s, _ = h.shape
    H = HEADS_PER_MIXER
    proj = h @ w_in_l
    offs = [int(o) for o in np.cumsum(IN_SIZES)[:-1]]
    (q_a, k_a, v_a, q_b, k_b, v_b, q_c, k_c, v_c,
     q_d, k_dc, v_dc, k_ds, v_ds, k_dw, v_dw, g_d) = jnp.split(proj, offs, axis=-1)
    tabs = [rel_bias[:, i * H:(i + 1) * H].T for i in range(N_MIXERS)]
    o_a = dilated_attention(_heads(q_a, H), _heads(k_a, H), _heads(v_a, H), tabs[0])
    gb = H // SWA_KV_HEADS
    o_b, _ = banded_attention(_heads(q_b, H).reshape(b, SWA_KV_HEADS, gb, s, HEAD_DIM),
                              _heads(k_b, SWA_KV_HEADS), _heads(v_b, SWA_KV_HEADS),
                              SWA_WINDOW - 1, tabs[1].reshape(SWA_KV_HEADS, gb, REL_BUCKETS), 1,
                              sink_l.reshape(SWA_KV_HEADS, gb))
    o_b = o_b.reshape(b, H, s, HEAD_DIM)
    o_c = moba_attention(_heads(q_c, H), _heads(k_c, H), _heads(v_c, H), tabs[2])
    gd = H // NSA_KV_HEADS
    gates = jax.nn.sigmoid(g_d.reshape(b, s, H, 3)).transpose(0, 2, 1, 3).reshape(b, NSA_KV_HEADS, gd, s, 3)
    kvh = lambda t: _heads(t, NSA_KV_HEADS)
    o_d = nsa_attention(_heads(q_d, H).reshape(b, NSA_KV_HEADS, gd, s, HEAD_DIM),
                        kvh(k_dc), kvh(v_dc), kvh(k_ds), kvh(v_ds), kvh(k_dw), kvh(v_dw), gates,
                        pe_k, pe_v, w1k, w2k, w1v, w2v, tabs[3].reshape(NSA_KV_HEADS, gd, REL_BUCKETS))
    o_d = o_d.reshape(b, H, s, HEAD_DIM)
    outs = [rmsnorm(_seq(o), mix_g[i]) for i, o in enumerate((o_a, o_b, o_c, o_d))]
    return jnp.concatenate(outs, axis=-1) @ w_out_l


def moe_ffn(h, router_w, router_bias, w_gate, w_up, w_down):
    b, s, d = h.shape
    t = h.reshape(b * s, d)
    aff = jax.nn.sigmoid((t @ router_w).astype(jnp.float32))
    score = aff + router_bias.astype(jnp.float32)
    per_group = N_EXPERTS // N_EXPERT_GROUPS
    group_score = lax.top_k(score.reshape(-1, N_EXPERT_GROUPS, per_group), 2)[0].sum(-1)
    _, gsel = lax.top_k(group_score, TOPK_GROUPS)
    gmask = jnp.any(jnp.arange(N_EXPERT_GROUPS)[None, None, :] == gsel[..., None], axis=1)
    emask = jnp.repeat(gmask, per_group, axis=-1)
    _, esel = lax.top_k(jnp.where(emask, score, NEG_INF), TOPK_EXPERTS)
    w = jnp.take_along_axis(aff, esel, axis=-1)
    w = w / jnp.sum(w, axis=-1, keepdims=True)
    combine = jnp.sum((esel[..., None] == jnp.arange(N_EXPERTS)) * w[..., None], axis=1).astype(t.dtype)
    y = jnp.zeros_like(t)
    for e in range(N_EXPERTS):
        he = jax.nn.silu(t @ w_gate[e]) * (t @ w_up[e])
        y = y + combine[:, e:e + 1] * (he @ w_down[e])
    return y.reshape(b, s, d)


def setup_inputs(seed: int = 0) -> dict:
    key = jax.random.key(seed)
    ks = jax.random.split(key, 24)
    f32 = jnp.float32

    def nrm(k, shape, sc):
        return jax.random.normal(k, shape, f32) * sc

    D, E, F = D_MODEL, N_EXPERTS, D_EXPERT
    cin = NSA_CMP_BLOCK * HEAD_DIM
    return {
        "x": nrm(ks[0], (BATCH, SEQ, D), 1.0),
        "c": nrm(ks[1], (BATCH, D), 1.0),
        "rel_bias": nrm(ks[2], (REL_BUCKETS, N_HEADS_TOTAL), 0.5),
        "router_w": nrm(ks[3], (D, E), D ** -0.5),
        "router_bias": nrm(ks[4], (E,), 0.01),
        "norm1_g": 1.0 + nrm(ks[5], (DEPTH, D), 0.02),
        "norm2_g": 1.0 + nrm(ks[6], (DEPTH, D), 0.02),
        "ada_w": nrm(ks[7], (DEPTH, D, 6 * D), 0.3 * D ** -0.5),
        "ada_b": nrm(ks[8], (DEPTH, 6 * D), 0.02),
        "w_in": nrm(ks[9], (DEPTH, D, IN_COLS), D ** -0.5),
        "nsa_pe_k": nrm(ks[10], (DEPTH, NSA_CMP_BLOCK, HEAD_DIM), 0.5),
        "nsa_pe_v": nrm(ks[11], (DEPTH, NSA_CMP_BLOCK, HEAD_DIM), 0.5),
        "nsa_cmp_w1_k": nrm(ks[12], (DEPTH, cin, NSA_CMP_HIDDEN), cin ** -0.5),
        "nsa_cmp_w2_k": nrm(ks[13], (DEPTH, NSA_CMP_HIDDEN, HEAD_DIM), NSA_CMP_HIDDEN ** -0.5),
        "nsa_cmp_w1_v": nrm(ks[14], (DEPTH, cin, NSA_CMP_HIDDEN), cin ** -0.5),
        "nsa_cmp_w2_v": nrm(ks[15], (DEPTH, NSA_CMP_HIDDEN, HEAD_DIM), NSA_CMP_HIDDEN ** -0.5),
        "sinks": nrm(ks[16], (DEPTH, HEADS_PER_MIXER), 0.5),
        "mix_norm_g": 1.0 + nrm(ks[17], (DEPTH, N_MIXERS, GROUP_WIDTH), 0.02),
        "w_out": nrm(ks[18], (DEPTH, MIX_WIDTH, D), MIX_WIDTH ** -0.5),
        "exp_w_gate": nrm(ks[19], (DEPTH, E, D, F), D ** -0.5),
        "exp_w_up": nrm(ks[20], (DEPTH, E, D, F), D ** -0.5),
        "exp_w_down": nrm(ks[21], (DEPTH, E, F, D), F ** -0.5),
        "final_g": 1.0 + nrm(ks[22], (D,), 0.02),
    }


def reference(x, c, rel_bias, router_w, router_bias, norm1_g, norm2_g, ada_w, ada_b, w_in,
              nsa_pe_k, nsa_pe_v, nsa_cmp_w1_k, nsa_cmp_w2_k, nsa_cmp_w1_v, nsa_cmp_w2_v,
              sinks, mix_norm_g, w_out, exp_w_gate, exp_w_up, exp_w_down, final_g):
    c_act = jax.nn.silu(c)
    for l in range(DEPTH):
        mod = c_act @ ada_w[l] + ada_b[l]
        sh1, sc1, g1, sh2, sc2, g2 = jnp.split(mod, 6, axis=-1)
        h = rmsnorm(x, norm1_g[l]) * (1.0 + sc1[:, None]) + sh1[:, None]
        x = x + g1[:, None] * token_mixer(h, w_in[l], rel_bias, sinks[l], nsa_pe_k[l], nsa_pe_v[l],
                                          nsa_cmp_w1_k[l], nsa_cmp_w2_k[l], nsa_cmp_w1_v[l], nsa_cmp_w2_v[l],
                                          mix_norm_g[l], w_out[l])
        h = rmsnorm(x, norm2_g[l]) * (1.0 + sc2[:, None]) + sh2[:, None]
        x = x + g2[:, None] * moe_ffn(h, router_w, router_bias, exp_w_gate[l], exp_w_up[l], exp_w_down[l])
    return rmsnorm(x, final_g)
```

```python
import functools
import math

import numpy as np
import jax
import jax.numpy as jnp
from jax import lax
from jax.experimental import pallas as pl
from jax.experimental.pallas import tpu as pltpu

F32 = jnp.float32
BF16 = jnp.bfloat16

D_MODEL = 2048
DEPTH = 4
HEAD_DIM = 64
N_MIXERS = 4
HEADS = 8
GROUP_WIDTH = HEADS * HEAD_DIM
SCALE = HEAD_DIM ** -0.5
REL_BUCKETS = 32
REL_MAX_DIST = 2048
DILATED_CONFIGS = ((128, 1), (512, 4), (2048, 16))
SWA_WINDOW = 128
KV_HEADS = 2
GQA = HEADS // KV_HEADS
MOBA_BLOCK = 256
MOBA_TOPK = 3
NSA_CMP_BLOCK = 32
NSA_CMP_STRIDE = 16
NSA_CMP_HIDDEN = 128
NSA_SLC_BLOCK = 64
NSA_SLC_TOPN = 16
NSA_WINDOW = 512
N_EXPERTS = 16
N_EXPERT_GROUPS = 4
PER_GROUP = N_EXPERTS // N_EXPERT_GROUPS
D_EXPERT = 512
NORM_EPS = 1e-6
NEG_INF = -1e30
TINY = float(np.finfo(np.float32).tiny)

N_PROJ_HEADS = 80
PROJ_COLS = N_PROJ_HEADS * HEAD_DIM
N_GATE_COLS = HEADS * 3
LANES = 128
QA, KA, VA = 0, 8, 16
QB, KB, VB = 24, 32, 34
QC, KC, VC = 36, 44, 52
QD, KDC, VDC, KDS, VDS, KDW, VDW = 60, 68, 70, 72, 74, 76, 78

VMEM_LIMIT = 56 * 1024 * 1024


def _cparams(n_axes):
    return pltpu.CompilerParams(dimension_semantics=("arbitrary",) * n_axes,
                                vmem_limit_bytes=VMEM_LIMIT)


def _t5_bucket(dist):
    dist = jnp.maximum(dist, 0)
    max_exact = REL_BUCKETS // 2
    large = max_exact + (jnp.log(jnp.maximum(dist, 1).astype(F32) / max_exact)
                         / math.log(REL_MAX_DIST / max_exact) * (REL_BUCKETS - max_exact)).astype(jnp.int32)
    large = jnp.minimum(large, REL_BUCKETS - 1)
    return jnp.where(dist < max_exact, dist, large)


def _mod_kernel(c_ref, w_ref, b_ref, o_ref):
    c = c_ref[...]
    ca = (c * jax.nn.sigmoid(c)).astype(BF16)
    o_ref[0] = jnp.dot(ca, w_ref[0].astype(BF16), preferred_element_type=F32) + b_ref[0]


def _modulation(c, ada_w, ada_b):
    depth, d, n = ada_w.shape
    b = c.shape[0]
    bp = 8
    tn = 1024
    cp = jnp.pad(c, ((0, bp - b), (0, 0)))
    out = pl.pallas_call(
        _mod_kernel,
        grid=(depth, n // tn),
        in_specs=[pl.BlockSpec((bp, d), lambda l, j: (0, 0)),
                  pl.BlockSpec((1, d, tn), lambda l, j: (l, 0, j)),
                  pl.BlockSpec((1, 1, tn), lambda l, j: (l, 0, j))],
        out_specs=pl.BlockSpec((1, bp, tn), lambda l, j: (l, 0, j)),
        out_shape=jax.ShapeDtypeStruct((depth, bp, n), F32),
        compiler_params=_cparams(2),
        name="adaln_mod",
    )(cp, ada_w, ada_b.reshape(depth, 1, n))
    return out[:, :b]


def _inproj_kernel(x_ref, g_ref, sc_ref, sh_ref, w_ref, wg_ref, p_ref, gate_ref, h_scr):
    n = pl.program_id(2)

    @pl.when(n == 0)
    def _():
        x = x_ref[0]
        ms = jnp.mean(x * x, axis=-1, keepdims=True)
        h = (x * lax.rsqrt(ms + NORM_EPS)) * g_ref[...] * (1.0 + sc_ref[0]) + sh_ref[0]
        hb = h.astype(BF16)
        h_scr[...] = hb
        gate_ref[0] = jax.nn.sigmoid(jnp.dot(hb, wg_ref[...], preferred_element_type=F32))

    res = jnp.dot(h_scr[...], w_ref[...], preferred_element_type=F32)
    for j in range(HEADS):
        p_ref[0, j] = res[:, j * HEAD_DIM:(j + 1) * HEAD_DIM].astype(BF16)


def _in_projection(x, g, sc, sh, w_heads, w_gate):
    b, s, d = x.shape
    tm = min(1024, s)
    tn = HEADS * HEAD_DIM
    return pl.pallas_call(
        _inproj_kernel,
        grid=(b, s // tm, PROJ_COLS // tn),
        in_specs=[pl.BlockSpec((1, tm, d), lambda i, m, n: (i, m, 0)),
                  pl.BlockSpec((1, d), lambda i, m, n: (0, 0)),
                  pl.BlockSpec((1, 1, d), lambda i, m, n: (i, 0, 0)),
                  pl.BlockSpec((1, 1, d), lambda i, m, n: (i, 0, 0)),
                  pl.BlockSpec((d, tn), lambda i, m, n: (0, n)),
                  pl.BlockSpec((d, LANES), lambda i, m, n: (0, 0))],
        out_specs=[pl.BlockSpec((1, HEADS, tm, HEAD_DIM), lambda i, m, n: (i, n, m, 0)),
                   pl.BlockSpec((1, tm, LANES), lambda i, m, n: (i, m, 0))],
        out_shape=[jax.ShapeDtypeStruct((b, N_PROJ_HEADS, s, HEAD_DIM), BF16),
                   jax.ShapeDtypeStruct((b, s, LANES), F32)],
        scratch_shapes=[pltpu.VMEM((tm, d), BF16)],
        compiler_params=_cparams(3),
        name="in_proj",
    )(x, g.reshape(1, d), sc.reshape(b, 1, d), sh.reshape(b, 1, d), w_heads, w_gate)


def _banded_kernel(*refs, rows, tq, nk, w, has_sink, emit_lse):
    q_ref, k_ref, v_ref, b_ref = refs[:4]
    pos = 4
    sink_ref = None
    if has_sink:
        sink_ref = refs[pos]
        pos += 1
    o_ref = refs[pos]
    qi = pl.program_id(2)
    kstart = pl.multiple_of(jnp.maximum(qi * tq - w, 0), tq)
    q = q_ref[0].reshape(rows, HEAD_DIM)
    k = k_ref[0, 0, pl.ds(kstart, nk), :]
    v = v_ref[0, 0, pl.ds(kstart, nk), :]
    s = lax.dot_general(q, k, (((1,), (1,)), ((), ())), preferred_element_type=F32) * SCALE + b_ref[0, 0]
    m = jnp.max(s, axis=-1, keepdims=True)
    if has_sink:
        m = jnp.maximum(m, sink_ref[0])
    e = jnp.exp(s - m)
    den = jnp.sum(e, axis=-1, keepdims=True)
    if has_sink:
        den = den + jnp.exp(sink_ref[0] - m)
    den = jnp.maximum(den, TINY)
    o = jnp.dot(e.astype(BF16), v, preferred_element_type=F32) / den
    if emit_lse:
        lse = jnp.broadcast_to(m + jnp.log(den), (rows, HEAD_DIM))
        o = jnp.concatenate([o, lse], axis=-1)
    o_ref[0] = o.reshape(o_ref.shape[1:])


def _band_bias_tiles(tab, max_dist, w, tq, dist_scale):
    nb, r_heads, _ = tab.shape
    nk = w + tq
    r = np.arange(tq)[:, None]
    c = np.arange(nk)[None, :]
    tiles = []
    for t in range(w // tq + 1):
        rel = t * tq + r - c
        valid = (rel >= 0) & (rel <= max_dist)
        bucket = _t5_bucket(jnp.asarray(np.maximum(rel, 0) * dist_scale))
        bias = jnp.where(jnp.asarray(valid), tab.astype(F32)[:, :, bucket], NEG_INF)
        tiles.append(bias.reshape(nb, r_heads * tq, nk))
    return jnp.stack(tiles, axis=1)


def _banded_attention(q_arr, k_arr, v_arr, bias, *, grid01, q_map, kv_maps, bias_map, r_heads, w,
                      out_heads, sink=None, emit_lse=False, name):
    seq = q_arr.shape[2]
    tq = 128
    nk = w + tq
    assert nk <= seq and seq % tq == 0 and w % tq == 0
    n_off = w // tq
    rows = r_heads * tq
    k_map, v_map = kv_maps
    in_specs = [pl.BlockSpec((1, r_heads, tq, HEAD_DIM), lambda i, j, t: q_map(i, j) + (t, 0)),
                pl.BlockSpec((1, 1, seq, HEAD_DIM), lambda i, j, t: k_map(i, j) + (0, 0)),
                pl.BlockSpec((1, 1, seq, HEAD_DIM), lambda i, j, t: v_map(i, j) + (0, 0)),
                pl.BlockSpec((1, 1, rows, nk), lambda i, j, t: (bias_map(i, j), jnp.minimum(t, n_off), 0, 0))]
    args = [q_arr, k_arr, v_arr, bias]
    if sink is not None:
        in_specs.append(pl.BlockSpec((1, rows, 1), lambda i, j, t: (bias_map(i, j), 0, 0)))
        args.append(sink)
    width = 2 * HEAD_DIM if emit_lse else HEAD_DIM
    n0, n1 = grid01
    out_shape = jax.ShapeDtypeStruct((n0, out_heads, seq, width), F32)
    return pl.pallas_call(
        functools.partial(_banded_kernel, rows=rows, tq=tq, nk=nk, w=w, has_sink=sink is not None,
                          emit_lse=emit_lse),
        grid=(n0, n1, seq // tq),
        in_specs=in_specs,
        out_specs=pl.BlockSpec((1, r_heads, tq, width), lambda i, j, t: (i, j, t, 0)),
        out_shape=out_shape,
        compiler_params=_cparams(3),
        name=name,
    )(*args)


def _rank_lt(vals, limit):
    n = vals.shape[-1]
    col = lax.broadcasted_iota(jnp.int32, vals.shape, 1)
    rank = jnp.zeros(vals.shape, jnp.int32)
    for i in range(n):
        vi = vals[:, i:i + 1]
        ahead = (vi > vals) | ((vi == vals) & (col > i))
        rank = rank + ahead.astype(jnp.int32)
    return rank < limit


def _flash_selected(q, sel, qi, k_ref, v_ref, e_ref, b_ref, tq):
    def tile(kj):
        ks = pl.multiple_of(kj * tq, tq)
        k = k_ref[0, 0, pl.ds(ks, tq), :]
        v = v_ref[0, 0, pl.ds(ks, tq), :]
        s = lax.dot_general(q, k, (((1,), (1,)), ((), ())), preferred_element_type=F32) * SCALE
        s = s + b_ref[0, qi - kj]
        selx = jnp.dot(sel, e_ref[:, pl.ds(ks, tq)], preferred_element_type=F32)
        return jnp.where(selx > 0.5, s, NEG_INF), v

    s, v = tile(qi)
    m = jnp.max(s, axis=-1, keepdims=True)
    e = jnp.exp(s - m)
    l = jnp.sum(e, axis=-1, keepdims=True)
    acc = jnp.dot(e.astype(BF16), v, preferred_element_type=F32)

    def body(kj, carry):
        m, l, acc = carry
        s, v = tile(kj)
        m_new = jnp.maximum(m, jnp.max(s, axis=-1, keepdims=True))
        alpha = jnp.exp(m - m_new)
        e = jnp.exp(s - m_new)
        l = alpha * l + jnp.sum(e, axis=-1, keepdims=True)
        acc = alpha * acc + jnp.dot(e.astype(BF16), v, preferred_element_type=F32)
        return m_new, l, acc

    m, l, acc = lax.fori_loop(0, qi, body, (m, l, acc))
    return acc / jnp.maximum(l, TINY)


def _moba_kernel(q_ref, k_ref, v_ref, pool_ref, e_ref, b_ref, o_ref, kmean_scr, *, tq, nblk):
    qi = pl.program_id(2)

    @pl.when(qi == 0)
    def _():
        kmean_scr[...] = jnp.dot(pool_ref[...], k_ref[0, 0], preferred_element_type=F32).astype(BF16)

    q = q_ref[0, 0]
    gate = lax.dot_general(q, kmean_scr[...], (((1,), (1,)), ((), ())), preferred_element_type=F32)
    col = lax.broadcasted_iota(jnp.int32, (tq, nblk), 1)
    gate = jnp.where(col < qi, gate, NEG_INF)
    sel = (_rank_lt(gate, MOBA_TOPK) & (col < qi)) | (col == qi)
    sel = jnp.where(sel, 1.0, 0.0).astype(BF16)
    o_ref[0, 0] = _flash_selected(q, sel, qi, k_ref, v_ref, e_ref, b_ref, tq)


def _slc_kernel(q_ref, k_ref, v_ref, sel_ref, e_ref, b_ref, o_ref, *, tq):
    qi = pl.program_id(2)
    o_ref[0, 0] = _flash_selected(q_ref[0, 0], sel_ref[0, 0], qi, k_ref, v_ref, e_ref, b_ref, tq)


def _flash_bias_tiles(tab, seq, tq):
    r = np.arange(tq)[:, None]
    c = np.arange(tq)[None, :]
    rel = (np.arange(seq // tq)[:, None, None] * tq + r - c)
    bucket = _t5_bucket(jnp.asarray(np.maximum(rel, 0)))
    return jnp.where(jnp.asarray(rel >= 0), tab.astype(F32)[:, bucket], NEG_INF)


def _expand_matrix(nblk, seq):
    blk = seq // nblk
    return jnp.asarray((np.arange(seq)[None, :] // blk == np.arange(nblk)[:, None]).astype(np.float32), BF16)


def _moba_attention(p, bias):
    b, _, s, _ = p.shape
    tq = MOBA_BLOCK
    nblk = s // tq
    pool = jnp.asarray((np.arange(s)[None, :] // tq == np.arange(nblk)[:, None]).astype(np.float32) / tq, BF16)
    expand = _expand_matrix(nblk, s)
    return pl.pallas_call(
        functools.partial(_moba_kernel, tq=tq, nblk=nblk),
        grid=(b, HEADS, s // tq),
        in_specs=[pl.BlockSpec((1, 1, tq, HEAD_DIM), lambda i, h, t: (i, QC + h, t, 0)),
                  pl.BlockSpec((1, 1, s, HEAD_DIM), lambda i, h, t: (i, KC + h, 0, 0)),
                  pl.BlockSpec((1, 1, s, HEAD_DIM), lambda i, h, t: (i, VC + h, 0, 0)),
                  pl.BlockSpec((nblk, s), lambda i, h, t: (0, 0)),
                  pl.BlockSpec((nblk, s), lambda i, h, t: (0, 0)),
                  pl.BlockSpec((1, s // tq, tq, tq), lambda i, h, t: (h, 0, 0, 0))],
        out_specs=pl.BlockSpec((1, 1, tq, HEAD_DIM), lambda i, h, t: (i, h, t, 0)),
        out_shape=jax.ShapeDtypeStruct((b, HEADS, s, HEAD_DIM), F32),
        scratch_shapes=[pltpu.VMEM((nblk, HEAD_DIM), BF16)],
        compiler_params=_cparams(3),
        name="moba",
    )(p, p, p, pool, expand, bias)


def _nsa_selected_attention(p, sel, bias):
    b, _, s, _ = p.shape
    tq = 256
    nblk = s // NSA_SLC_BLOCK
    expand = _expand_matrix(nblk, s)
    return pl.pallas_call(
        functools.partial(_slc_kernel, tq=tq),
        grid=(b, HEADS, s // tq),
        in_specs=[pl.BlockSpec((1, 1, tq, HEAD_DIM), lambda i, h, t: (i, QD + h, t, 0)),
                  pl.BlockSpec((1, 1, s, HEAD_DIM), lambda i, h, t: (i, KDS + h // GQA, 0, 0)),
                  pl.BlockSpec((1, 1, s, HEAD_DIM), lambda i, h, t: (i, VDS + h // GQA, 0, 0)),
                  pl.BlockSpec((1, 1, tq, nblk), lambda i, h, t: (i, h // GQA, t, 0)),
                  pl.BlockSpec((nblk, s), lambda i, h, t: (0, 0)),
                  pl.BlockSpec((1, s // tq, tq, tq), lambda i, h, t: (h, 0, 0, 0))],
        out_specs=pl.BlockSpec((1, 1, tq, HEAD_DIM), lambda i, h, t: (i, h, t, 0)),
        out_shape=jax.ShapeDtypeStruct((b, HEADS, s, HEAD_DIM), F32),
        compiler_params=_cparams(3),
        name="nsa_selected",
    )(p, p, p, sel, expand, bias)


def _compress_kernel(t_ref, pe_ref, w1_ref, w2_ref, o_ref):
    t = t_ref[0, 0].astype(F32)
    half = t.shape[-1]
    top = (t + pe_ref[0, 0:1, :]).astype(BF16)
    bot = (t + pe_ref[0, 1:2, :]).astype(BF16)
    u = jnp.dot(top, w1_ref[0, :half, :], preferred_element_type=F32)
    vv = jnp.dot(bot, w1_ref[0, half:, :], preferred_element_type=F32)
    pre = u + pltpu.roll(vv, vv.shape[0] - 1, axis=0)
    hid = jax.nn.gelu(pre).astype(BF16)
    o_ref[0, 0] = jnp.dot(hid, w2_ref[0], preferred_element_type=F32).astype(BF16)


def _nsa_compress(p, pe, w1, w2):
    b, _, s, _ = p.shape
    n_half = s // NSA_CMP_STRIDE
    half = NSA_CMP_STRIDE * HEAD_DIM
    t = p[:, KDC:KDC + 2 * KV_HEADS].reshape(b, 2 * KV_HEADS, n_half, half)
    pe2 = pe.reshape(2, 2, half)
    return pl.pallas_call(
        _compress_kernel,
        grid=(b, 2 * KV_HEADS),
        in_specs=[pl.BlockSpec((1, 1, n_half, half), lambda i, j: (i, j, 0, 0)),
                  pl.BlockSpec((1, 2, half), lambda i, j: (j // KV_HEADS, 0, 0)),
                  pl.BlockSpec((1, 2 * half, NSA_CMP_HIDDEN), lambda i, j: (j // KV_HEADS, 0, 0)),
                  pl.BlockSpec((1, NSA_CMP_HIDDEN, HEAD_DIM), lambda i, j: (j // KV_HEADS, 0, 0))],
        out_specs=pl.BlockSpec((1, 1, n_half, HEAD_DIM), lambda i, j: (i, j, 0, 0)),
        out_shape=jax.ShapeDtypeStruct((b, 2 * KV_HEADS, n_half, HEAD_DIM), BF16),
        compiler_params=_cparams(2),
        name="nsa_compress",
    )(t, pe2, w1.astype(BF16), w2.astype(BF16))


def _cmp_attn_kernel(q_ref, kc_ref, vc_ref, ov_ref, o_ref, sel_ref, *, tq, n_slc):
    qi = pl.program_id(2)
    rows = GQA * tq
    q = q_ref[0].reshape(rows, HEAD_DIM)
    kc = kc_ref[0, 0]
    n_cmp = kc.shape[0]
    s = lax.dot_general(q, kc, (((1,), (1,)), ((), ())), preferred_element_type=F32) * SCALE
    row = lax.broadcasted_iota(jnp.int32, (rows, n_cmp), 0)
    pos = qi * tq + (row & (tq - 1))
    blk_end = lax.broadcasted_iota(jnp.int32, (rows, n_cmp), 1) * NSA_CMP_STRIDE + (NSA_CMP_BLOCK - 1)
    mask = blk_end <= pos
    s = jnp.where(mask, s, NEG_INF)
    m = jnp.max(s, axis=-1, keepdims=True)
    e = jnp.where(mask, jnp.exp(s - m), 0.0)
    den = jnp.maximum(jnp.sum(e, axis=-1, keepdims=True), TINY)
    pr = e / den
    o_ref[0] = jnp.dot(pr.astype(BF16), vc_ref[0, 0], preferred_element_type=F32).reshape(GQA, tq, HEAD_DIM)
    psum = pr[0:tq]
    for g in range(1, GQA):
        psum = psum + pr[g * tq:(g + 1) * tq]
    ov = ov_ref[...]
    hi = psum.astype(BF16)
    r1 = psum - hi.astype(F32)
    mid = r1.astype(BF16)
    lo = (r1 - mid.astype(F32)).astype(BF16)
    imp = (jnp.dot(hi, ov, preferred_element_type=F32) + jnp.dot(mid, ov, preferred_element_type=F32)
           + jnp.dot(lo, ov, preferred_element_type=F32))
    jj = lax.broadcasted_iota(jnp.int32, (tq, n_slc), 1)
    cur = (qi * tq + lax.broadcasted_iota(jnp.int32, (tq, n_slc), 0)) // NSA_SLC_BLOCK
    forced = (jj == 0) | (jj == cur) | (jj == cur - 1)
    imp = jnp.where(forced, -NEG_INF, jnp.where(jj <= cur, imp, NEG_INF))
    sel = _rank_lt(imp, NSA_SLC_TOPN) & (jj <= cur)
    sel_ref[0, 0] = jnp.where(sel, 1.0, 0.0).astype(BF16)


def _nsa_compressed_attention(p, kvc):
    b, _, s, _ = p.shape
    tq = 128
    n_cmp = kvc.shape[2]
    n_slc = s // NSA_SLC_BLOCK
    c_start = np.arange(n_cmp) * NSA_CMP_STRIDE
    j_start = np.arange(n_slc) * NSA_SLC_BLOCK
    overlap = ((c_start[:, None] < j_start[None, :] + NSA_SLC_BLOCK)
               & (c_start[:, None] + NSA_CMP_BLOCK > j_start[None, :])).astype(np.float32)
    overlap[n_cmp - 1] = 0.0
    return pl.pallas_call(
        functools.partial(_cmp_attn_kernel, tq=tq, n_slc=n_slc),
        grid=(b, KV_HEADS, s // tq),
        in_specs=[pl.BlockSpec((1, GQA, tq, HEAD_DIM), lambda i, h, t: (i, QD // GQA + h, t, 0)),
                  pl.BlockSpec((1, 1, n_cmp, HEAD_DIM), lambda i, h, t: (i, h, 0, 0)),
                  pl.BlockSpec((1, 1, n_cmp, HEAD_DIM), lambda i, h, t: (i, KV_HEADS + h, 0, 0)),
                  pl.BlockSpec((n_cmp, n_slc), lambda i, h, t: (0, 0))],
        out_specs=[pl.BlockSpec((1, GQA, tq, HEAD_DIM), lambda i, h, t: (i, h, t, 0)),
                   pl.BlockSpec((1, 1, tq, n_slc), lambda i, h, t: (i, h, t, 0))],
        out_shape=[jax.ShapeDtypeStruct((b, HEADS, s, HEAD_DIM), F32),
                   jax.ShapeDtypeStruct((b, KV_HEADS, s, n_slc), BF16)],
        compiler_params=_cparams(3),
        name="nsa_compressed",
    )(p, kvc, kvc, jnp.asarray(overlap, BF16))


def _heads_to_lanes(ref, width=HEAD_DIM, offset=0):
    return jnp.concatenate([ref[0, h][:, offset:offset + width] for h in range(HEADS)], axis=-1)


def _group_norm(o, g):
    ms = jnp.mean(o * o, axis=-1, keepdims=True)
    return (o * lax.rsqrt(ms + NORM_EPS)) * g


def _route(logits_t, bias_t):
    aff = jax.nn.sigmoid(logits_t)
    score = aff + bias_t
    rows = [score[e:e + 1] for e in range(N_EXPERTS)]
    affs = [aff[e:e + 1] for e in range(N_EXPERTS)]
    gscore = []
    for g in range(N_EXPERT_GROUPS):
        mem = rows[g * PER_GROUP:(g + 1) * PER_GROUP]
        best = None
        for a in range(PER_GROUP):
            for c in range(a + 1, PER_GROUP):
                pair = mem[a] + mem[c]
                best = pair if best is None else jnp.maximum(best, pair)
        gscore.append(best)
    gsel = jnp.zeros_like(gscore[0], dtype=jnp.int32)
    gbest = gscore[0]
    for g in range(1, N_EXPERT_GROUPS):
        better = gscore[g] > gbest
        gsel = jnp.where(better, g, gsel)
        gbest = jnp.where(better, gscore[g], gbest)
    masked = [jnp.where(gsel == e // PER_GROUP, rows[e], NEG_INF) for e in range(N_EXPERTS)]

    def argbest(vals):
        idx = jnp.zeros_like(gsel)
        best = vals[0]
        for e in range(1, N_EXPERTS):
            better = vals[e] > best
            idx = jnp.where(better, e, idx)
            best = jnp.where(better, vals[e], best)
        return idx

    e1 = argbest(masked)
    e2 = argbest([jnp.where(e1 == e, -jnp.inf, masked[e]) for e in range(N_EXPERTS)])
    w1 = sum(jnp.where(e1 == e, affs[e], 0.0) for e in range(N_EXPERTS))
    w2 = sum(jnp.where(e2 == e, affs[e], 0.0) for e in range(N_EXPERTS))
    tot = w1 + w2
    w1 = w1 / tot
    w2 = w2 / tot
    return jnp.concatenate([jnp.where(e1 == e, w1, 0.0) + jnp.where(e2 == e, w2, 0.0)
                            for e in range(N_EXPERTS)], axis=0)


def _out_kernel(x_ref, a0_ref, a1_ref, a2_ref, ob_ref, oc_ref, dc_ref, ds_ref, dw_ref, gate_ref,
                mixg_ref, wout_ref, g1_ref, n2g_ref, sc2_ref, sh2_ref, rw_ref, rb_ref,
                xo_ref, h2_ref, comb_ref):
    outs = [_heads_to_lanes(r) for r in (a0_ref, a1_ref, a2_ref)]
    lses = [_heads_to_lanes(r, offset=HEAD_DIM) for r in (a0_ref, a1_ref, a2_ref)]
    mx = jnp.maximum(jnp.maximum(lses[0], lses[1]), lses[2])
    ws = [jnp.exp(l - mx) for l in lses]
    tot = ws[0] + ws[1] + ws[2]
    o_a = (ws[0] / tot) * outs[0] + (ws[1] / tot) * outs[1] + (ws[2] / tot) * outs[2]
    o_b = _heads_to_lanes(ob_ref)
    o_c = _heads_to_lanes(oc_ref)
    gates = gate_ref[0]
    tm = gates.shape[0]

    def gate_cols(i):
        return jnp.concatenate([jnp.broadcast_to(gates[:, h * 3 + i:h * 3 + i + 1], (tm, HEAD_DIM))
                                for h in range(HEADS)], axis=-1)

    o_d = (gate_cols(0) * _heads_to_lanes(dc_ref) + gate_cols(1) * _heads_to_lanes(ds_ref)
           + gate_cols(2) * _heads_to_lanes(dw_ref))
    mixed = jnp.concatenate([_group_norm(o, mixg_ref[i:i + 1, :]).astype(BF16)
                             for i, o in enumerate((o_a, o_b, o_c, o_d))], axis=-1)
    y = jnp.dot(mixed, wout_ref[...], preferred_element_type=F32)
    x = x_ref[0] + g1_ref[0] * y
    xo_ref[0] = x
    ms = jnp.mean(x * x, axis=-1, keepdims=True)
    h2 = ((x * lax.rsqrt(ms + NORM_EPS)) * n2g_ref[...] * (1.0 + sc2_ref[0]) + sh2_ref[0]).astype(BF16)
    h2_ref[0] = h2
    logits_t = lax.dot_general(rw_ref[...], h2, (((1,), (1,)), ((), ())), preferred_element_type=F32)
    comb_ref[0] = _route(logits_t, rb_ref[...])


def _mixer_out(x, a_outs, o_b, o_c, o_dc, o_ds, o_dw, gates, mix_g, w_out, g1, n2g, sc2, sh2, rw_t, rb):
    b, s, d = x.shape
    tm = min(256, s)
    head_spec = lambda width: pl.BlockSpec((1, HEADS, tm, width), lambda i, m: (i, 0, m, 0))
    vec_b = pl.BlockSpec((1, 1, d), lambda i, m: (i, 0, 0))
    return pl.pallas_call(
        _out_kernel,
        grid=(b, s // tm),
        in_specs=[pl.BlockSpec((1, tm, d), lambda i, m: (i, m, 0)),
                  head_spec(2 * HEAD_DIM), head_spec(2 * HEAD_DIM), head_spec(2 * HEAD_DIM),
                  head_spec(HEAD_DIM), head_spec(HEAD_DIM), head_spec(HEAD_DIM), head_spec(HEAD_DIM),
                  head_spec(HEAD_DIM),
                  pl.BlockSpec((1, tm, LANES), lambda i, m: (i, m, 0)),
                  pl.BlockSpec((N_MIXERS, GROUP_WIDTH), lambda i, m: (0, 0)),
                  pl.BlockSpec((N_MIXERS * GROUP_WIDTH, d), lambda i, m: (0, 0)),
                  vec_b,
                  pl.BlockSpec((1, d), lambda i, m: (0, 0)),
                  vec_b, vec_b,
                  pl.BlockSpec((N_EXPERTS, d), lambda i, m: (0, 0)),
                  pl.BlockSpec((N_EXPERTS, 1), lambda i, m: (0, 0))],
        out_specs=[pl.BlockSpec((1, tm, d), lambda i, m: (i, m, 0)),
                   pl.BlockSpec((1, tm, d), lambda i, m: (i, m, 0)),
                   pl.BlockSpec((1, N_EXPERTS, tm), lambda i, m: (i, 0, m))],
        out_shape=[jax.ShapeDtypeStruct((b, s, d), F32),
                   jax.ShapeDtypeStruct((b, s, d), BF16),
                   jax.ShapeDtypeStruct((b, N_EXPERTS, s), F32)],
        compiler_params=_cparams(2),
        name="mixer_out",
    )(x, *a_outs, o_b, o_c, o_dc, o_ds, o_dw, gates, mix_g, w_out, g1.reshape(b, 1, d), n2g.reshape(1, d),
      sc2.reshape(b, 1, d), sh2.reshape(b, 1, d), rw_t, rb.reshape(N_EXPERTS, 1))


def _expert_kernel(x_ref, h_ref, comb_ref, wgu_ref, wd_ref, g2_ref, fg_ref, o_ref, acc_ref, *, final_norm):
    e = pl.program_id(2)

    @pl.when(e == 0)
    def _():
        acc_ref[...] = jnp.zeros_like(acc_ref)

    h = h_ref[0]
    gu = jnp.dot(h, wgu_ref[0], preferred_element_type=F32)
    gate = gu[:, :D_EXPERT]
    he = (gate * jax.nn.sigmoid(gate)) * gu[:, D_EXPERT:]
    col = lax.broadcasted_iota(jnp.int32, comb_ref.shape[1:], 1)
    w = jnp.sum(jnp.where(col == e, comb_ref[0], 0.0), axis=-1, keepdims=True)
    acc_ref[...] += w * jnp.dot(he.astype(BF16), wd_ref[0], preferred_element_type=F32)

    @pl.when(e == N_EXPERTS - 1)
    def _():
        x = x_ref[0] + g2_ref[0] * acc_ref[...]
        if final_norm:
            ms = jnp.mean(x * x, axis=-1, keepdims=True)
            x = (x * lax.rsqrt(ms + NORM_EPS)) * fg_ref[...]
        o_ref[0] = x


def _experts(x, h2, comb, w_gu, w_d, g2, final_g, final_norm):
    b, s, d = x.shape
    tm = min(512, s)
    vec_b = pl.BlockSpec((1, 1, d), lambda i, m, e: (i, 0, 0))
    return pl.pallas_call(
        functools.partial(_expert_kernel, final_norm=final_norm),
        grid=(b, s // tm, N_EXPERTS),
        in_specs=[pl.BlockSpec((1, tm, d), lambda i, m, e: (i, m, 0)),
                  pl.BlockSpec((1, tm, d), lambda i, m, e: (i, m, 0)),
                  pl.BlockSpec((1, tm, N_EXPERTS), lambda i, m, e: (i, m, 0)),
                  pl.BlockSpec((1, d, 2 * D_EXPERT), lambda i, m, e: (e, 0, 0)),
                  pl.BlockSpec((1, D_EXPERT, d), lambda i, m, e: (e, 0, 0)),
                  vec_b,
                  pl.BlockSpec((1, d), lambda i, m, e: (0, 0))],
        out_specs=pl.BlockSpec((1, tm, d), lambda i, m, e: (i, m, 0)),
        out_shape=jax.ShapeDtypeStruct((b, s, d), F32),
        scratch_shapes=[pltpu.VMEM((tm, d), F32)],
        compiler_params=_cparams(3),
        name="experts",
    )(x, h2, comb, w_gu, w_d, g2.reshape(b, 1, d), final_g.reshape(1, d))


def _bias_tables(rel_bias, s):
    tabs = [rel_bias[:, i * HEADS:(i + 1) * HEADS].T for i in range(N_MIXERS)]
    t = {}
    for window, r in DILATED_CONFIGS:
        t["a%d" % r] = _band_bias_tiles(tabs[0][:, None], window // r, 128, 128, r)
    t["b"] = _band_bias_tiles(tabs[1].reshape(KV_HEADS, GQA, REL_BUCKETS), SWA_WINDOW - 1, 128, 128, 1)
    t["c"] = _flash_bias_tiles(tabs[2], s, MOBA_BLOCK)
    t["ds"] = _flash_bias_tiles(tabs[3], s, 256)
    t["dw"] = _band_bias_tiles(tabs[3].reshape(KV_HEADS, GQA, REL_BUCKETS), NSA_WINDOW - 1, NSA_WINDOW, 128, 1)
    return t


def _token_mixer_branches(p, bias, sink, pe, w1, w2):
    b, _, s, _ = p.shape
    a_outs = []
    for window, r in DILATED_CONFIGS:
        if r == 1:
            o = _banded_attention(p, p, p, bias["a1"], grid01=(b, HEADS),
                                  q_map=lambda i, j: (i, QA + j), kv_maps=(lambda i, j: (i, KA + j),
                                                                           lambda i, j: (i, VA + j)),
                                  bias_map=lambda i, j: j, r_heads=1, w=128, out_heads=HEADS,
                                  emit_lse=True, name="dilated_r1")
        else:
            qkv = p[:, QA:QA + 3 * HEADS].reshape(b, 3 * HEADS, s // r, r, HEAD_DIM)
            qkv = qkv.transpose(0, 3, 1, 2, 4).reshape(b * r, 3 * HEADS, s // r, HEAD_DIM)
            o = _banded_attention(qkv, qkv, qkv, bias["a%d" % r], grid01=(b * r, HEADS),
                                  q_map=lambda i, j: (i, j), kv_maps=(lambda i, j: (i, HEADS + j),
                                                                      lambda i, j: (i, 2 * HEADS + j)),
                                  bias_map=lambda i, j: j, r_heads=1, w=128, out_heads=HEADS,
                                  emit_lse=True, name="dilated_r%d" % r)
            o = o.reshape(b, r, HEADS, s // r, 2 * HEAD_DIM).transpose(0, 2, 3, 1, 4)
            o = o.reshape(b, HEADS, s, 2 * HEAD_DIM)
        a_outs.append(o)
    o_b = _banded_attention(p, p, p, bias["b"], grid01=(b, KV_HEADS),
                            q_map=lambda i, j: (i, QB // GQA + j), kv_maps=(lambda i, j: (i, KB + j),
                                                                           lambda i, j: (i, VB + j)),
                            bias_map=lambda i, j: j, r_heads=GQA, w=128, out_heads=HEADS, sink=sink,
                            name="swa")
    o_c = _moba_attention(p, bias["c"])
    kvc = _nsa_compress(p, pe, w1, w2)
    o_dc, sel = _nsa_compressed_attention(p, kvc)
    o_ds = _nsa_selected_attention(p, sel, bias["ds"])
    o_dw = _banded_attention(p, p, p, bias["dw"], grid01=(b, KV_HEADS),
                             q_map=lambda i, j: (i, QD // GQA + j), kv_maps=(lambda i, j: (i, KDW + j),
                                                                            lambda i, j: (i, VDW + j)),
                             bias_map=lambda i, j: j, r_heads=GQA, w=NSA_WINDOW, out_heads=HEADS,
                             name="nsa_window")
    return a_outs, o_b, o_c, o_dc, o_ds, o_dw


def _split_w_in(w_in_l):
    w_heads = w_in_l[:, :PROJ_COLS].astype(BF16)
    w_gate = jnp.pad(w_in_l[:, PROJ_COLS:], ((0, 0), (0, LANES - N_GATE_COLS))).astype(BF16)
    return w_heads, w_gate


def _sink_rows(sink_l):
    return jnp.repeat(sink_l.astype(F32).reshape(KV_HEADS, GQA), 128, axis=1)[..., None]


def kernel(x, c, rel_bias, router_w, router_bias, norm1_g, norm2_g, ada_w, ada_b, w_in, nsa_pe_k, nsa_pe_v,
           nsa_cmp_w1_k, nsa_cmp_w2_k, nsa_cmp_w1_v, nsa_cmp_w2_v, sinks, mix_norm_g, w_out, exp_w_gate,
           exp_w_up, exp_w_down, final_g):
    b, s, d = x.shape
    mod = _modulation(c, ada_w, ada_b)
    bias = _bias_tables(rel_bias, s)
    rw_t = router_w.T.astype(BF16)
    for l in range(DEPTH):
        sh1, sc1, g1, sh2, sc2, g2 = jnp.split(mod[l], 6, axis=-1)
        w_heads, w_gate = _split_w_in(w_in[l])
        p, gates = _in_projection(x, norm1_g[l], sc1, sh1, w_heads, w_gate)
        pe = jnp.stack([nsa_pe_k[l], nsa_pe_v[l]])
        w1 = jnp.stack([nsa_cmp_w1_k[l], nsa_cmp_w1_v[l]])
        w2 = jnp.stack([nsa_cmp_w2_k[l], nsa_cmp_w2_v[l]])
        a_outs, o_b, o_c, o_dc, o_ds, o_dw = _token_mixer_branches(p, bias, _sink_rows(sinks[l]), pe, w1, w2)
        x, h2, comb_t = _mixer_out(x, a_outs, o_b, o_c, o_dc, o_ds, o_dw, gates, mix_norm_g[l],
                                   w_out[l].astype(BF16), g1, norm2_g[l], sc2, sh2, rw_t, router_bias)
        comb = comb_t.transpose(0, 2, 1)
        w_gu = jnp.concatenate([exp_w_gate[l], exp_w_up[l]], axis=-1).astype(BF16)
        x = _experts(x, h2, comb, w_gu, exp_w_down[l].astype(BF16), g2, final_g, l == DEPTH - 1)
    return x
```

```python
import functools
import math

import numpy as np
import jax
import jax.numpy as jnp
from jax import lax
from jax.experimental import pallas as pl
from jax.experimental.pallas import tpu as pltpu

F32 = jnp.float32
BF16 = jnp.bfloat16

D_MODEL = 2048
DEPTH = 4
HEAD_DIM = 64
N_MIXERS = 4
HEADS = 8
GROUP_WIDTH = HEADS * HEAD_DIM
SCALE = HEAD_DIM ** -0.5
REL_BUCKETS = 32
REL_MAX_DIST = 2048
DILATED_CONFIGS = ((128, 1), (512, 4), (2048, 16))
SWA_WINDOW = 128
KV_HEADS = 2
GQA = HEADS // KV_HEADS
MOBA_BLOCK = 256
MOBA_TOPK = 3
NSA_CMP_BLOCK = 32
NSA_CMP_STRIDE = 16
NSA_CMP_HIDDEN = 128
NSA_SLC_BLOCK = 64
NSA_SLC_TOPN = 16
NSA_WINDOW = 512
N_EXPERTS = 16
N_EXPERT_GROUPS = 4
PER_GROUP = N_EXPERTS // N_EXPERT_GROUPS
D_EXPERT = 512
NORM_EPS = 1e-6
NEG_INF = -1e30
TINY = float(np.finfo(np.float32).tiny)

N_PROJ_HEADS = 80
PROJ_COLS = N_PROJ_HEADS * HEAD_DIM
N_GATE_COLS = HEADS * 3
LANES = 128
SUBLANES = 8
QA, KA, VA = 0, 8, 16
QB, KB, VB = 24, 32, 34
QC, KC, VC = 36, 44, 52
QD, KDC, VDC, KDS, VDS, KDW, VDW = 60, 68, 70, 72, 74, 76, 78

VMEM_LIMIT = 56 * 1024 * 1024


def _cparams(n_axes):
    return pltpu.CompilerParams(dimension_semantics=("arbitrary",) * n_axes,
                                vmem_limit_bytes=VMEM_LIMIT)


def _mod_kernel(c_ref, w_ref, b_ref, o_ref):
    c = c_ref[...]
    ca = (c * jax.nn.sigmoid(c)).astype(BF16)
    o_ref[0] = jnp.dot(ca, w_ref[0].astype(BF16), preferred_element_type=F32) + b_ref[0]


def _modulation(c, ada_w, ada_b):
    depth, d, n = ada_w.shape
    b = c.shape[0]
    bp = 8
    tn = 1024
    cp = jnp.pad(c, ((0, bp - b), (0, 0)))
    out = pl.pallas_call(
        _mod_kernel,
        grid=(depth, n // tn),
        in_specs=[pl.BlockSpec((bp, d), lambda l, j: (0, 0)),
                  pl.BlockSpec((1, d, tn), lambda l, j: (l, 0, j)),
                  pl.BlockSpec((1, 1, tn), lambda l, j: (l, 0, j))],
        out_specs=pl.BlockSpec((1, bp, tn), lambda l, j: (l, 0, j)),
        out_shape=jax.ShapeDtypeStruct((depth, bp, n), F32),
        compiler_params=_cparams(2),
        name="adaln_mod",
    )(cp, ada_w, ada_b.reshape(depth, 1, n))
    return out[:, :b]


def _inproj_kernel(x_ref, g_ref, sc_ref, sh_ref, w_ref, wg_ref, p_ref, gate_ref, h_scr):
    n = pl.program_id(2)

    @pl.when(n == 0)
    def _():
        x = x_ref[0]
        ms = jnp.mean(x * x, axis=-1, keepdims=True)
        h = (x * lax.rsqrt(ms + NORM_EPS)) * g_ref[...] * (1.0 + sc_ref[0]) + sh_ref[0]
        hb = h.astype(BF16)
        h_scr[...] = hb
        gate_ref[0] = jax.nn.sigmoid(jnp.dot(hb, wg_ref[...], preferred_element_type=F32))

    res = jnp.dot(h_scr[...], w_ref[...], preferred_element_type=F32)
    for j in range(HEADS):
        p_ref[0, j] = res[:, j * HEAD_DIM:(j + 1) * HEAD_DIM].astype(BF16)


def _in_projection(x, g, sc, sh, w_heads, w_gate):
    b, s, d = x.shape
    tm = min(1024, s)
    tn = HEADS * HEAD_DIM
    return pl.pallas_call(
        _inproj_kernel,
        grid=(b, s // tm, PROJ_COLS // tn),
        in_specs=[pl.BlockSpec((1, tm, d), lambda i, m, n: (i, m, 0)),
                  pl.BlockSpec((1, d), lambda i, m, n: (0, 0)),
                  pl.BlockSpec((1, 1, d), lambda i, m, n: (i, 0, 0)),
                  pl.BlockSpec((1, 1, d), lambda i, m, n: (i, 0, 0)),
                  pl.BlockSpec((d, tn), lambda i, m, n: (0, n)),
                  pl.BlockSpec((d, LANES), lambda i, m, n: (0, 0))],
        out_specs=[pl.BlockSpec((1, HEADS, tm, HEAD_DIM), lambda i, m, n: (i, n, m, 0)),
                   pl.BlockSpec((1, tm, LANES), lambda i, m, n: (i, m, 0))],
        out_shape=[jax.ShapeDtypeStruct((b, N_PROJ_HEADS, s, HEAD_DIM), BF16),
                   jax.ShapeDtypeStruct((b, s, LANES), F32)],
        scratch_shapes=[pltpu.VMEM((tm, d), BF16)],
        compiler_params=_cparams(3),
        name="in_proj",
    )(x, g.reshape(1, d), sc.reshape(b, 1, d), sh.reshape(b, 1, d), w_heads, w_gate)


def _bucket_thresholds():
    d = np.arange(REL_MAX_DIST + 1)
    max_exact = REL_BUCKETS // 2
    large = max_exact + (np.log(np.maximum(d, 1).astype(np.float32) / np.float32(max_exact))
                         / np.float32(math.log(REL_MAX_DIST / max_exact))
                         * np.float32(REL_BUCKETS - max_exact)).astype(np.int32)
    bucket = np.where(d < max_exact, d, np.minimum(large, REL_BUCKETS - 1))
    return [int(np.argmax(bucket >= b)) for b in range(REL_BUCKETS)]


def _bias_kernel(off_ref, cmin_ref, tab_ref, o_ref, *, max_dist, dist_scale, thresholds):
    h = pl.program_id(0)
    t = pl.program_id(1)
    shape = o_ref.shape[-2:]
    rel = off_ref[t] + lax.broadcasted_iota(jnp.int32, shape, 0) - lax.broadcasted_iota(jnp.int32, shape, 1)
    acc = jnp.full(shape, tab_ref[h, 0], F32)
    for b in range(1, REL_BUCKETS):
        acc = jnp.where(rel >= -(-thresholds[b] // dist_scale), tab_ref[h, b], acc)
    col = lax.broadcasted_iota(jnp.int32, shape, 1)
    valid = (rel >= 0) & (rel <= max_dist) & (col >= cmin_ref[t])
    o_ref[...] = jnp.where(valid, acc, NEG_INF).reshape(o_ref.shape)


def _bias_tiles(tab, offsets, col_mins, rows, cols, max_dist, dist_scale, group):
    n_heads = tab.shape[0]
    n_tiles = len(offsets)
    smem = pl.BlockSpec(memory_space=pltpu.SMEM)
    out = pl.pallas_call(
        functools.partial(_bias_kernel, max_dist=max_dist, dist_scale=dist_scale,
                          thresholds=_bucket_thresholds()),
        grid=(n_heads, n_tiles),
        in_specs=[smem, smem, smem],
        out_specs=pl.BlockSpec((1, 1, 1, rows, cols), lambda h, t: (h // group, t, h % group, 0, 0)),
        out_shape=jax.ShapeDtypeStruct((n_heads // group, n_tiles, group, rows, cols), F32),
        compiler_params=_cparams(2),
        name="bias_tiles",
    )(jnp.asarray(offsets, jnp.int32), jnp.asarray(col_mins, jnp.int32), tab.astype(F32))
    return out.reshape(n_heads // group, n_tiles, group * rows, cols)


BAND_TILE = 128
BAND_STEP = 512


def _banded_kernel(*refs, units, r_heads, nsub, w, has_sink, emit_lse):
    q_ref, kp_ref, kc_ref, vp_ref, vc_ref, b_ref = refs[:6]
    pos = 6
    sink_ref = None
    if has_sink:
        sink_ref = refs[pos]
        pos += 1
    o_ref, kw_scr, vw_scr = refs[pos:pos + 3]
    t = pl.program_id(2)
    tq = BAND_TILE
    rows = r_heads * tq
    nk = w + tq
    kw_scr[:, :w] = kp_ref[0]
    kw_scr[:, w:] = kc_ref[0]
    vw_scr[:, :w] = vp_ref[0]
    vw_scr[:, w:] = vc_ref[0]
    for u in range(units):
        heads = slice(u * r_heads, (u + 1) * r_heads)
        for j in range(nsub):
            q = q_ref[0, heads, j * tq:(j + 1) * tq, :].reshape(rows, HEAD_DIM) * SCALE
            k = kw_scr[u, j * tq:j * tq + nk, :]
            v = vw_scr[u, j * tq:j * tq + nk, :]
            tile = jnp.minimum(t * nsub + j, w // tq)
            s = lax.dot_general(q, k, (((1,), (1,)), ((), ())), preferred_element_type=F32) + b_ref[u, tile]
            m = jnp.max(s, axis=-1, keepdims=True)
            if has_sink:
                m = jnp.maximum(m, sink_ref[u])
            e = jnp.exp(s - m)
            den = jnp.sum(e, axis=-1, keepdims=True)
            if has_sink:
                den = den + jnp.exp(sink_ref[u] - m)
            den = jnp.maximum(den, TINY)
            o = jnp.dot(e.astype(BF16), v, preferred_element_type=F32) / den
            if emit_lse:
                o = jnp.concatenate([o, jnp.broadcast_to(m + jnp.log(den), (rows, HEAD_DIM))], axis=-1)
            o_ref[0, heads, j * tq:(j + 1) * tq, :] = o.reshape(r_heads, tq, o.shape[-1])


def _band_bias_tiles(tab, max_dist, w, dist_scale, group):
    tq = BAND_TILE
    n_early = w // tq
    return _bias_tiles(tab, [w] * (n_early + 1), [w - e * tq for e in range(n_early)] + [0], tq, w + tq,
                       max_dist, dist_scale, group)


def _banded_attention(q_arr, k_arr, v_arr, bias, *, grid01, q_map, kv_maps, bias_map, r_heads, units, w,
                      out_heads, sink=None, emit_lse=False, name):
    seq = q_arr.shape[2]
    tq = BAND_TILE
    step = min(BAND_STEP, seq)
    nsub = step // tq
    assert seq % step == 0 and w % tq == 0 and step % w == 0
    k_map, v_map = kv_maps
    prev = lambda t: jnp.maximum(t * (step // w) - 1, 0)
    in_specs = [pl.BlockSpec((1, units * r_heads, step, HEAD_DIM), lambda i, j, t: q_map(i, j) + (t, 0)),
                pl.BlockSpec((1, units, w, HEAD_DIM), lambda i, j, t: k_map(i, j) + (prev(t), 0)),
                pl.BlockSpec((1, units, step, HEAD_DIM), lambda i, j, t: k_map(i, j) + (t, 0)),
                pl.BlockSpec((1, units, w, HEAD_DIM), lambda i, j, t: v_map(i, j) + (prev(t), 0)),
                pl.BlockSpec((1, units, step, HEAD_DIM), lambda i, j, t: v_map(i, j) + (t, 0)),
                pl.BlockSpec((units, w // tq + 1, r_heads * tq, w + tq), lambda i, j, t: (bias_map(i, j), 0, 0, 0))]
    args = [q_arr, k_arr, k_arr, v_arr, v_arr, bias]
    if sink is not None:
        in_specs.append(pl.BlockSpec((units, r_heads * tq, 1), lambda i, j, t: (bias_map(i, j), 0, 0)))
        args.append(sink)
    width = 2 * HEAD_DIM if emit_lse else HEAD_DIM
    n0, n1 = grid01
    return pl.pallas_call(
        functools.partial(_banded_kernel, units=units, r_heads=r_heads, nsub=nsub, w=w,
                          has_sink=sink is not None, emit_lse=emit_lse),
        grid=(n0, n1, seq // step),
        in_specs=in_specs,
        out_specs=pl.BlockSpec((1, units * r_heads, step, width), lambda i, j, t: (i, j, t, 0)),
        out_shape=jax.ShapeDtypeStruct((n0, out_heads, seq, width), F32),
        scratch_shapes=[pltpu.VMEM((units, w + step, HEAD_DIM), BF16),
                        pltpu.VMEM((units, w + step, HEAD_DIM), BF16)],
        compiler_params=_cparams(3),
        name=name,
    )(*args)


def _ranks_along_rows(vals):
    n, t = vals.shape
    groups = [vals[g * SUBLANES:(g + 1) * SUBLANES] for g in range(n // SUBLANES)]
    row = lax.broadcasted_iota(jnp.int32, (SUBLANES, t), 0)
    later = [jnp.where(row > r, 1, 0) for r in range(SUBLANES)]
    cnt = [jnp.zeros((SUBLANES, t), jnp.int32) for _ in groups]
    for i in range(n):
        vi = vals[i:i + 1]
        for g, vg in enumerate(groups):
            if g > i // SUBLANES:
                cnt[g] = cnt[g] + jnp.where(vi >= vg, 1, 0)
            elif g < i // SUBLANES:
                cnt[g] = cnt[g] + jnp.where(vi > vg, 1, 0)
            else:
                cnt[g] = cnt[g] + jnp.where(vi > vg, 1, 0) + jnp.where(vi == vg, later[i % SUBLANES], 0)
    return jnp.concatenate(cnt, axis=0)


def _rows_to_lanes(x_t, width):
    n, t = x_t.shape
    padded = jnp.concatenate([x_t, jnp.zeros((LANES - n, t), x_t.dtype)], axis=0)
    return padded.T[:, :width]


def _for_tiles(n, fn, carry):
    def body4(j, c):
        for u in range(4):
            c = fn(j * 4 + u, c)
        return c

    carry = lax.fori_loop(0, n // 4, body4, carry)
    base = (n // 4) * 4
    carry = lax.cond((n & 2) != 0, lambda c: fn(base + 1, fn(base, c)), lambda c: c, carry)
    base = base + (n & 2)
    return lax.cond((n & 1) != 0, lambda c: fn(base, c), lambda c: c, carry)


def _flash_selected(q_augs, qi, kaug_scr, kv_of, v_ref, b_ref, s_scr, tq):
    nh = len(q_augs)
    half = tq // 2
    n = qi + 1

    def logits(kj, mruns):
        ks = pl.multiple_of(kj * tq, tq)
        out = []
        for h in range(nh):
            s = lax.dot_general(q_augs[h], kaug_scr[kv_of[h], pl.ds(ks, tq), :], (((1,), (1,)), ((), ())),
                                preferred_element_type=F32) + b_ref[h, qi - kj]
            s_scr[h, kj] = s
            out.append(jnp.maximum(mruns[h], jnp.maximum(s[:, :half], s[:, half:])))
        return tuple(out)

    mruns = _for_tiles(n, logits, tuple(jnp.full((tq, half), NEG_INF, F32) for _ in range(nh)))
    ms = [jnp.max(mr, axis=-1, keepdims=True) for mr in mruns]

    def weigh(kj, carry):
        ks = pl.multiple_of(kj * tq, tq)
        out = []
        for h in range(nh):
            lrun, acc = carry[h]
            e = jnp.exp(s_scr[h, kj] - ms[h])
            lrun = lrun + (e[:, :half] + e[:, half:])
            acc = acc + jnp.dot(e.astype(BF16), v_ref[0, kv_of[h], pl.ds(ks, tq), :], preferred_element_type=F32)
            out.append((lrun, acc))
        return tuple(out)

    init = tuple((jnp.zeros((tq, half), F32), jnp.zeros((tq, HEAD_DIM), F32)) for _ in range(nh))
    res = _for_tiles(n, weigh, init)
    return [acc / jnp.maximum(jnp.sum(lrun, axis=-1, keepdims=True), TINY) for lrun, acc in res]


def _fill_kaug(kaug_scr, k_ref, et_ref):
    for kv in range(kaug_scr.shape[0]):
        kaug_scr[kv, :, :HEAD_DIM] = k_ref[0, kv]
        kaug_scr[kv, :, HEAD_DIM:] = et_ref[...]


def _augment(q, pen):
    return jnp.concatenate([q * SCALE, pen.astype(BF16)], axis=-1)


def _moba_kernel(q_ref, k_ref, v_ref, pool_ref, et_ref, b_ref, o_ref, kaug_scr, kmean_scr, s_scr, *, tq, nblk, nh):
    qi = pl.program_id(2)

    @pl.when(qi == 0)
    def _():
        _fill_kaug(kaug_scr, k_ref, et_ref)
        for h in range(nh):
            kmean_scr[h] = jnp.dot(pool_ref[...], k_ref[0, h], preferred_element_type=F32).astype(BF16)

    q_augs = []
    for h in range(nh):
        q = q_ref[0, h]
        gate = lax.dot_general(kmean_scr[h], q, (((1,), (1,)), ((), ())), preferred_element_type=F32)
        blk = lax.broadcasted_iota(jnp.int32, (nblk, tq), 0)
        gate = jnp.where(blk < qi, gate, NEG_INF)
        picked = ((_ranks_along_rows(gate) < MOBA_TOPK) & (blk < qi)) | (blk == qi)
        pen_t = jnp.where(picked, 0.0, NEG_INF)
        q_augs.append(_augment(q, _rows_to_lanes(pen_t, HEAD_DIM)))
    outs = _flash_selected(q_augs, qi, kaug_scr, tuple(range(nh)), v_ref, b_ref, s_scr, tq)
    for h in range(nh):
        o_ref[0, h] = outs[h]


def _slc_kernel(q_ref, k_ref, v_ref, pen_ref, et_ref, b_ref, o_ref, kaug_scr, s_scr, *, tq, nh):
    qi = pl.program_id(2)

    @pl.when(qi == 0)
    def _():
        _fill_kaug(kaug_scr, k_ref, et_ref)

    q_augs = [_augment(q_ref[0, h], pen_ref[0, 0]) for h in range(nh)]
    outs = _flash_selected(q_augs, qi, kaug_scr, (0,) * nh, v_ref, b_ref, s_scr, tq)
    for h in range(nh):
        o_ref[0, h] = outs[h]


FLASH_TILE = 256
FLASH_HEADS = 2


def _flash_bias_tiles(tab, seq):
    n_off = seq // FLASH_TILE
    return _bias_tiles(tab, [o * FLASH_TILE for o in range(n_off)], [0] * n_off, FLASH_TILE, FLASH_TILE,
                       seq, 1, 1)


def _block_onehot_t(nblk, seq):
    blk = seq // nblk
    return jnp.asarray((np.arange(seq)[:, None] // blk == np.arange(HEAD_DIM)[None, :]).astype(np.float32), BF16)


def _moba_attention(p, bias):
    b, _, s, _ = p.shape
    tq, nh = FLASH_TILE, FLASH_HEADS
    assert tq == MOBA_BLOCK
    nblk = s // tq
    nrow = max(nblk, 16)
    pool = jnp.asarray((np.arange(s)[None, :] // tq == np.arange(nrow)[:, None]).astype(np.float32) / tq, BF16)
    kv_spec = lambda base: pl.BlockSpec((1, nh, s, HEAD_DIM), lambda i, h, t: (i, base // nh + h, 0, 0))
    return pl.pallas_call(
        functools.partial(_moba_kernel, tq=tq, nblk=nrow, nh=nh),
        grid=(b, HEADS // nh, s // tq),
        in_specs=[pl.BlockSpec((1, nh, tq, HEAD_DIM), lambda i, h, t: (i, QC // nh + h, t, 0)),
                  kv_spec(KC), kv_spec(VC),
                  pl.BlockSpec((nrow, s), lambda i, h, t: (0, 0)),
                  pl.BlockSpec((s, HEAD_DIM), lambda i, h, t: (0, 0)),
                  pl.BlockSpec((nh, s // tq, tq, tq), lambda i, h, t: (h, 0, 0, 0))],
        out_specs=pl.BlockSpec((1, nh, tq, HEAD_DIM), lambda i, h, t: (i, h, t, 0)),
        out_shape=jax.ShapeDtypeStruct((b, HEADS, s, HEAD_DIM), F32),
        scratch_shapes=[pltpu.VMEM((nh, s, 2 * HEAD_DIM), BF16),
                        pltpu.VMEM((nh, nrow, HEAD_DIM), BF16),
                        pltpu.VMEM((nh, s // tq, tq, tq), F32)],
        compiler_params=_cparams(3),
        name="moba",
    )(p, p, p, pool, _block_onehot_t(nblk, s), bias)


def _nsa_selected_attention(p, pen, bias):
    b, _, s, _ = p.shape
    tq, nh = FLASH_TILE, FLASH_HEADS
    nblk = s // NSA_SLC_BLOCK
    assert nblk <= HEAD_DIM and GQA % nh == 0
    kv_spec = lambda base: pl.BlockSpec((1, 1, s, HEAD_DIM), lambda i, h, t: (i, base + (h * nh) // GQA, 0, 0))
    return pl.pallas_call(
        functools.partial(_slc_kernel, tq=tq, nh=nh),
        grid=(b, HEADS // nh, s // tq),
        in_specs=[pl.BlockSpec((1, nh, tq, HEAD_DIM), lambda i, h, t: (i, QD // nh + h, t, 0)),
                  kv_spec(KDS), kv_spec(VDS),
                  pl.BlockSpec((1, 1, tq, HEAD_DIM), lambda i, h, t: (i, (h * nh) // GQA, t, 0)),
                  pl.BlockSpec((s, HEAD_DIM), lambda i, h, t: (0, 0)),
                  pl.BlockSpec((nh, s // tq, tq, tq), lambda i, h, t: (h, 0, 0, 0))],
        out_specs=pl.BlockSpec((1, nh, tq, HEAD_DIM), lambda i, h, t: (i, h, t, 0)),
        out_shape=jax.ShapeDtypeStruct((b, HEADS, s, HEAD_DIM), F32),
        scratch_shapes=[pltpu.VMEM((1, s, 2 * HEAD_DIM), BF16),
                        pltpu.VMEM((nh, s // tq, tq, tq), F32)],
        compiler_params=_cparams(3),
        name="nsa_selected",
    )(p, p, p, pen, _block_onehot_t(nblk, s), bias)


def _compress_kernel(t_ref, pe_ref, w1_ref, w2_ref, o_ref):
    t = t_ref[0, 0].astype(F32)
    half = t.shape[-1]
    top = (t + pe_ref[0, 0:1, :]).astype(BF16)
    bot = (t + pe_ref[0, 1:2, :]).astype(BF16)
    u = jnp.dot(top, w1_ref[0, :half, :], preferred_element_type=F32)
    vv = jnp.dot(bot, w1_ref[0, half:, :], preferred_element_type=F32)
    pre = u + pltpu.roll(vv, vv.shape[0] - 1, axis=0)
    hid = jax.nn.gelu(pre).astype(BF16)
    o_ref[0, 0] = jnp.dot(hid, w2_ref[0], preferred_element_type=F32).astype(BF16)


def _nsa_compress(p, pe, w1, w2):
    b, _, s, _ = p.shape
    n_half = s // NSA_CMP_STRIDE
    half = NSA_CMP_STRIDE * HEAD_DIM
    t = p[:, KDC:KDC + 2 * KV_HEADS].reshape(b, 2 * KV_HEADS, n_half, half)
    pe2 = pe.reshape(2, 2, half)
    return pl.pallas_call(
        _compress_kernel,
        grid=(b, 2 * KV_HEADS),
        in_specs=[pl.BlockSpec((1, 1, n_half, half), lambda i, j: (i, j, 0, 0)),
                  pl.BlockSpec((1, 2, half), lambda i, j: (j // KV_HEADS, 0, 0)),
                  pl.BlockSpec((1, 2 * half, NSA_CMP_HIDDEN), lambda i, j: (j // KV_HEADS, 0, 0)),
                  pl.BlockSpec((1, NSA_CMP_HIDDEN, HEAD_DIM), lambda i, j: (j // KV_HEADS, 0, 0))],
        out_specs=pl.BlockSpec((1, 1, n_half, HEAD_DIM), lambda i, j: (i, j, 0, 0)),
        out_shape=jax.ShapeDtypeStruct((b, 2 * KV_HEADS, n_half, HEAD_DIM), BF16),
        compiler_params=_cparams(2),
        name="nsa_compress",
    )(t, pe2, w1.astype(BF16), w2.astype(BF16))


def _cmp_attn_kernel(q_ref, kc_ref, vc_ref, ov_ref, o_ref, pen_ref, *, tq, n_slc):
    qi = pl.program_id(2)
    rows = GQA * tq
    q = q_ref[0].reshape(rows, HEAD_DIM)
    kc = kc_ref[0, 0]
    n_cmp = kc.shape[0]
    s = lax.dot_general(q, kc, (((1,), (1,)), ((), ())), preferred_element_type=F32) * SCALE
    row = lax.broadcasted_iota(jnp.int32, (rows, n_cmp), 0)
    pos = qi * tq + (row & (tq - 1))
    blk_end = lax.broadcasted_iota(jnp.int32, (rows, n_cmp), 1) * NSA_CMP_STRIDE + (NSA_CMP_BLOCK - 1)
    mask = blk_end <= pos
    s = jnp.where(mask, s, NEG_INF)
    m = jnp.max(s, axis=-1, keepdims=True)
    e = jnp.where(mask, jnp.exp(s - m), 0.0)
    den = jnp.maximum(jnp.sum(e, axis=-1, keepdims=True), TINY)
    pr = e / den
    o_ref[0] = jnp.dot(pr.astype(BF16), vc_ref[0, 0], preferred_element_type=F32).reshape(GQA, tq, HEAD_DIM)
    psum = pr[0:tq]
    for g in range(1, GQA):
        psum = psum + pr[g * tq:(g + 1) * tq]
    ov_t = ov_ref[...]
    hi = psum.astype(BF16)
    r1 = psum - hi.astype(F32)
    mid = r1.astype(BF16)
    lo = (r1 - mid.astype(F32)).astype(BF16)
    nt = (((1,), (1,)), ((), ()))
    imp_t = (lax.dot_general(ov_t, hi, nt, preferred_element_type=F32)
             + lax.dot_general(ov_t, mid, nt, preferred_element_type=F32)
             + lax.dot_general(ov_t, lo, nt, preferred_element_type=F32))
    jj = lax.broadcasted_iota(jnp.int32, (n_slc, tq), 0)
    cur = (qi * tq + lax.broadcasted_iota(jnp.int32, (n_slc, tq), 1)) // NSA_SLC_BLOCK
    forced = (jj == 0) | (jj == cur) | (jj == cur - 1)
    imp_t = jnp.where(forced, -NEG_INF, jnp.where(jj <= cur, imp_t, NEG_INF))
    picked = (_ranks_along_rows(imp_t) < NSA_SLC_TOPN) & (jj <= cur)
    pen_ref[0, 0] = _rows_to_lanes(jnp.where(picked, 0.0, NEG_INF), HEAD_DIM).astype(BF16)


def _nsa_compressed_attention(p, kvc):
    b, _, s, _ = p.shape
    tq = 128
    n_cmp = kvc.shape[2]
    n_slc = s // NSA_SLC_BLOCK
    assert n_slc == HEAD_DIM
    c_start = np.arange(n_cmp) * NSA_CMP_STRIDE
    j_start = np.arange(n_slc) * NSA_SLC_BLOCK
    overlap = ((c_start[:, None] < j_start[None, :] + NSA_SLC_BLOCK)
               & (c_start[:, None] + NSA_CMP_BLOCK > j_start[None, :])).astype(np.float32)
    overlap[n_cmp - 1] = 0.0
    return pl.pallas_call(
        functools.partial(_cmp_attn_kernel, tq=tq, n_slc=n_slc),
        grid=(b, KV_HEADS, s // tq),
        in_specs=[pl.BlockSpec((1, GQA, tq, HEAD_DIM), lambda i, h, t: (i, QD // GQA + h, t, 0)),
                  pl.BlockSpec((1, 1, n_cmp, HEAD_DIM), lambda i, h, t: (i, h, 0, 0)),
                  pl.BlockSpec((1, 1, n_cmp, HEAD_DIM), lambda i, h, t: (i, KV_HEADS + h, 0, 0)),
                  pl.BlockSpec((n_slc, n_cmp), lambda i, h, t: (0, 0))],
        out_specs=[pl.BlockSpec((1, GQA, tq, HEAD_DIM), lambda i, h, t: (i, h, t, 0)),
                   pl.BlockSpec((1, 1, tq, HEAD_DIM), lambda i, h, t: (i, h, t, 0))],
        out_shape=[jax.ShapeDtypeStruct((b, HEADS, s, HEAD_DIM), F32),
                   jax.ShapeDtypeStruct((b, KV_HEADS, s, HEAD_DIM), BF16)],
        compiler_params=_cparams(3),
        name="nsa_compressed",
    )(p, kvc, kvc, jnp.asarray(overlap.T, BF16))


def _heads_to_lanes(ref, width=HEAD_DIM, offset=0):
    return jnp.concatenate([ref[0, h][:, offset:offset + width] for h in range(HEADS)], axis=-1)


def _group_norm(o, g):
    ms = jnp.mean(o * o, axis=-1, keepdims=True)
    return (o * lax.rsqrt(ms + NORM_EPS)) * g


def _route(logits_t, bias_t):
    aff = jax.nn.sigmoid(logits_t)
    score = aff + bias_t
    rows = [score[e:e + 1] for e in range(N_EXPERTS)]
    affs = [aff[e:e + 1] for e in range(N_EXPERTS)]
    gscore = []
    for g in range(N_EXPERT_GROUPS):
        mem = rows[g * PER_GROUP:(g + 1) * PER_GROUP]
        best = None
        for a in range(PER_GROUP):
            for c in range(a + 1, PER_GROUP):
                pair = mem[a] + mem[c]
                best = pair if best is None else jnp.maximum(best, pair)
        gscore.append(best)
    gsel = jnp.zeros_like(gscore[0], dtype=jnp.int32)
    gbest = gscore[0]
    for g in range(1, N_EXPERT_GROUPS):
        better = gscore[g] > gbest
        gsel = jnp.where(better, g, gsel)
        gbest = jnp.where(better, gscore[g], gbest)
    masked = [jnp.where(gsel == e // PER_GROUP, rows[e], NEG_INF) for e in range(N_EXPERTS)]

    def argbest(vals):
        idx = jnp.zeros_like(gsel)
        best = vals[0]
        for e in range(1, N_EXPERTS):
            better = vals[e] > best
            idx = jnp.where(better, e, idx)
            best = jnp.where(better, vals[e], best)
        return idx

    e1 = argbest(masked)
    e2 = argbest([jnp.where(e1 == e, -jnp.inf, masked[e]) for e in range(N_EXPERTS)])
    w1 = sum(jnp.where(e1 == e, affs[e], 0.0) for e in range(N_EXPERTS))
    w2 = sum(jnp.where(e2 == e, affs[e], 0.0) for e in range(N_EXPERTS))
    tot = w1 + w2
    w1 = w1 / tot
    w2 = w2 / tot
    return jnp.concatenate([jnp.where(e1 == e, w1, 0.0) + jnp.where(e2 == e, w2, 0.0)
                            for e in range(N_EXPERTS)], axis=0)


def _out_kernel(x_ref, a0_ref, a1_ref, a2_ref, ob_ref, oc_ref, dc_ref, ds_ref, dw_ref, gate_ref,
                mixg_ref, wout_ref, g1_ref, n2g_ref, sc2_ref, sh2_ref, rw_ref, rb_ref,
                xo_ref, h2_ref, comb_ref):
    outs = [_heads_to_lanes(r) for r in (a0_ref, a1_ref, a2_ref)]
    lses = [_heads_to_lanes(r, offset=HEAD_DIM) for r in (a0_ref, a1_ref, a2_ref)]
    mx = jnp.maximum(jnp.maximum(lses[0], lses[1]), lses[2])
    ws = [jnp.exp(l - mx) for l in lses]
    tot = ws[0] + ws[1] + ws[2]
    o_a = (ws[0] / tot) * outs[0] + (ws[1] / tot) * outs[1] + (ws[2] / tot) * outs[2]
    o_b = _heads_to_lanes(ob_ref)
    o_c = _heads_to_lanes(oc_ref)
    gates = gate_ref[0]
    tm = gates.shape[0]

    def gate_cols(i):
        return jnp.concatenate([jnp.broadcast_to(gates[:, h * 3 + i:h * 3 + i + 1], (tm, HEAD_DIM))
                                for h in range(HEADS)], axis=-1)

    o_d = (gate_cols(0) * _heads_to_lanes(dc_ref) + gate_cols(1) * _heads_to_lanes(ds_ref)
           + gate_cols(2) * _heads_to_lanes(dw_ref))
    mixed = jnp.concatenate([_group_norm(o, mixg_ref[i:i + 1, :]).astype(BF16)
                             for i, o in enumerate((o_a, o_b, o_c, o_d))], axis=-1)
    y = jnp.dot(mixed, wout_ref[...], preferred_element_type=F32)
    x = x_ref[0] + g1_ref[0] * y
    xo_ref[0] = x
    ms = jnp.mean(x * x, axis=-1, keepdims=True)
    h2 = ((x * lax.rsqrt(ms + NORM_EPS)) * n2g_ref[...] * (1.0 + sc2_ref[0]) + sh2_ref[0]).astype(BF16)
    h2_ref[0] = h2
    logits_t = lax.dot_general(rw_ref[...], h2, (((1,), (1,)), ((), ())), preferred_element_type=F32)
    comb_ref[0] = _route(logits_t, rb_ref[...])


def _mixer_out(x, a_outs, o_b, o_c, o_dc, o_ds, o_dw, gates, mix_g, w_out, g1, n2g, sc2, sh2, rw_t, rb):
    b, s, d = x.shape
    tm = min(256, s)
    head_spec = lambda width: pl.BlockSpec((1, HEADS, tm, width), lambda i, m: (i, 0, m, 0))
    vec_b = pl.BlockSpec((1, 1, d), lambda i, m: (i, 0, 0))
    return pl.pallas_call(
        _out_kernel,
        grid=(b, s // tm),
        in_specs=[pl.BlockSpec((1, tm, d), lambda i, m: (i, m, 0)),
                  head_spec(2 * HEAD_DIM), head_spec(2 * HEAD_DIM), head_spec(2 * HEAD_DIM),
                  head_spec(HEAD_DIM), head_spec(HEAD_DIM), head_spec(HEAD_DIM), head_spec(HEAD_DIM),
                  head_spec(HEAD_DIM),
                  pl.BlockSpec((1, tm, LANES), lambda i, m: (i, m, 0)),
                  pl.BlockSpec((N_MIXERS, GROUP_WIDTH), lambda i, m: (0, 0)),
                  pl.BlockSpec((N_MIXERS * GROUP_WIDTH, d), lambda i, m: (0, 0)),
                  vec_b,
                  pl.BlockSpec((1, d), lambda i, m: (0, 0)),
                  vec_b, vec_b,
                  pl.BlockSpec((N_EXPERTS, d), lambda i, m: (0, 0)),
                  pl.BlockSpec((N_EXPERTS, 1), lambda i, m: (0, 0))],
        out_specs=[pl.BlockSpec((1, tm, d), lambda i, m: (i, m, 0)),
                   pl.BlockSpec((1, tm, d), lambda i, m: (i, m, 0)),
                   pl.BlockSpec((1, N_EXPERTS, tm), lambda i, m: (i, 0, m))],
        out_shape=[jax.ShapeDtypeStruct((b, s, d), F32),
                   jax.ShapeDtypeStruct((b, s, d), BF16),
                   jax.ShapeDtypeStruct((b, N_EXPERTS, s), F32)],
        compiler_params=_cparams(2),
        name="mixer_out",
    )(x, *a_outs, o_b, o_c, o_dc, o_ds, o_dw, gates, mix_g, w_out, g1.reshape(b, 1, d), n2g.reshape(1, d),
      sc2.reshape(b, 1, d), sh2.reshape(b, 1, d), rw_t, rb.reshape(N_EXPERTS, 1))


def _expert_kernel(x_ref, h_ref, comb_ref, wgu_ref, wd_ref, g2_ref, fg_ref, o_ref, acc_ref, *, final_norm):
    e = pl.program_id(2)

    @pl.when(e == 0)
    def _():
        acc_ref[...] = jnp.zeros_like(acc_ref)

    h = h_ref[0]
    gu = jnp.dot(h, wgu_ref[0], preferred_element_type=F32)
    gate = gu[:, :D_EXPERT]
    he = (gate * jax.nn.sigmoid(gate)) * gu[:, D_EXPERT:]
    col = lax.broadcasted_iota(jnp.int32, comb_ref.shape[1:], 1)
    w = jnp.sum(jnp.where(col == e, comb_ref[0], 0.0), axis=-1, keepdims=True)
    acc_ref[...] += w * jnp.dot(he.astype(BF16), wd_ref[0], preferred_element_type=F32)

    @pl.when(e == N_EXPERTS - 1)
    def _():
        x = x_ref[0] + g2_ref[0] * acc_ref[...]
        if final_norm:
            ms = jnp.mean(x * x, axis=-1, keepdims=True)
            x = (x * lax.rsqrt(ms + NORM_EPS)) * fg_ref[...]
        o_ref[0] = x


def _experts(x, h2, comb, w_gu, w_d, g2, final_g, final_norm):
    b, s, d = x.shape
    tm = min(512, s)
    vec_b = pl.BlockSpec((1, 1, d), lambda i, m, e: (i, 0, 0))
    return pl.pallas_call(
        functools.partial(_expert_kernel, final_norm=final_norm),
        grid=(b, s // tm, N_EXPERTS),
        in_specs=[pl.BlockSpec((1, tm, d), lambda i, m, e: (i, m, 0)),
                  pl.BlockSpec((1, tm, d), lambda i, m, e: (i, m, 0)),
                  pl.BlockSpec((1, tm, N_EXPERTS), lambda i, m, e: (i, m, 0)),
                  pl.BlockSpec((1, d, 2 * D_EXPERT), lambda i, m, e: (e, 0, 0)),
                  pl.BlockSpec((1, D_EXPERT, d), lambda i, m, e: (e, 0, 0)),
                  vec_b,
                  pl.BlockSpec((1, d), lambda i, m, e: (0, 0))],
        out_specs=pl.BlockSpec((1, tm, d), lambda i, m, e: (i, m, 0)),
        out_shape=jax.ShapeDtypeStruct((b, s, d), F32),
        scratch_shapes=[pltpu.VMEM((tm, d), F32)],
        compiler_params=_cparams(3),
        name="experts",
    )(x, h2, comb, w_gu, w_d, g2.reshape(b, 1, d), final_g.reshape(1, d))


def _bias_tables(rel_bias, s):
    tabs = [rel_bias[:, i * HEADS:(i + 1) * HEADS].T for i in range(N_MIXERS)]
    t = {}
    for window, r in DILATED_CONFIGS:
        t["a%d" % r] = _band_bias_tiles(tabs[0], window // r, 128, r, 1)
    t["b"] = _band_bias_tiles(tabs[1], SWA_WINDOW - 1, 128, 1, GQA)
    t["c"] = _flash_bias_tiles(tabs[2], s)
    t["ds"] = _flash_bias_tiles(tabs[3], s)
    t["dw"] = _band_bias_tiles(tabs[3], NSA_WINDOW - 1, NSA_WINDOW, 1, GQA)
    return t


def _token_mixer_branches(p, bias, sink, pe, w1, w2):
    b, _, s, _ = p.shape
    a_units = 4
    a_outs = []
    for window, r in DILATED_CONFIGS:
        if r == 1:
            src, lead, (q0, k0, v0) = p, b, (QA, KA, VA)
        else:
            src = p[:, QA:QA + 3 * HEADS].reshape(b, 3 * HEADS, s // r, r, HEAD_DIM)
            src = src.transpose(0, 3, 1, 2, 4).reshape(b * r, 3 * HEADS, s // r, HEAD_DIM)
            lead, (q0, k0, v0) = b * r, (0, HEADS, 2 * HEADS)
        o = _banded_attention(src, src, src, bias["a%d" % r], grid01=(lead, HEADS // a_units),
                              q_map=lambda i, j, q0=q0: (i, q0 // a_units + j),
                              kv_maps=(lambda i, j, k0=k0: (i, k0 // a_units + j),
                                       lambda i, j, v0=v0: (i, v0 // a_units + j)),
                              bias_map=lambda i, j: j, r_heads=1, units=a_units, w=128, out_heads=HEADS,
                              emit_lse=True, name="dilated_r%d" % r)
        if r > 1:
            o = o.reshape(b, r, HEADS, s // r, 2 * HEAD_DIM).transpose(0, 2, 3, 1, 4)
            o = o.reshape(b, HEADS, s, 2 * HEAD_DIM)
        a_outs.append(o)
    o_b = _banded_attention(p, p, p, bias["b"], grid01=(b, 1),
                            q_map=lambda i, j: (i, QB // HEADS), kv_maps=(lambda i, j: (i, KB // KV_HEADS),
                                                                         lambda i, j: (i, VB // KV_HEADS)),
                            bias_map=lambda i, j: 0, r_heads=GQA, units=KV_HEADS, w=128, out_heads=HEADS,
                            sink=sink, name="swa")
    o_c = _moba_attention(p, bias["c"])
    kvc = _nsa_compress(p, pe, w1, w2)
    o_dc, pen = _nsa_compressed_attention(p, kvc)
    o_ds = _nsa_selected_attention(p, pen, bias["ds"])
    o_dw = _banded_attention(p, p, p, bias["dw"], grid01=(b, KV_HEADS),
                             q_map=lambda i, j: (i, QD // GQA + j), kv_maps=(lambda i, j: (i, KDW + j),
                                                                            lambda i, j: (i, VDW + j)),
                             bias_map=lambda i, j: j, r_heads=GQA, units=1, w=NSA_WINDOW, out_heads=HEADS,
                             name="nsa_window")
    return a_outs, o_b, o_c, o_dc, o_ds, o_dw


def _split_w_in(w_in_l):
    w_heads = w_in_l[:, :PROJ_COLS].astype(BF16)
    w_gate = jnp.pad(w_in_l[:, PROJ_COLS:], ((0, 0), (0, LANES - N_GATE_COLS))).astype(BF16)
    return w_heads, w_gate


def _sink_rows(sink_l):
    return jnp.repeat(sink_l.astype(F32).reshape(KV_HEADS, GQA), BAND_TILE, axis=1)[..., None]


def kernel(x, c, rel_bias, router_w, router_bias, norm1_g, norm2_g, ada_w, ada_b, w_in, nsa_pe_k, nsa_pe_v,
           nsa_cmp_w1_k, nsa_cmp_w2_k, nsa_cmp_w1_v, nsa_cmp_w2_v, sinks, mix_norm_g, w_out, exp_w_gate,
           exp_w_up, exp_w_down, final_g):
    b, s, d = x.shape
    mod = _modulation(c, ada_w, ada_b)
    bias = _bias_tables(rel_bias, s)
    rw_t = router_w.T.astype(BF16)
    for l in range(DEPTH):
        sh1, sc1, g1, sh2, sc2, g2 = jnp.split(mod[l], 6, axis=-1)
        w_heads, w_gate = _split_w_in(w_in[l])
        p, gates = _in_projection(x, norm1_g[l], sc1, sh1, w_heads, w_gate)
        pe = jnp.stack([nsa_pe_k[l], nsa_pe_v[l]])
        w1 = jnp.stack([nsa_cmp_w1_k[l], nsa_cmp_w1_v[l]])
        w2 = jnp.stack([nsa_cmp_w2_k[l], nsa_cmp_w2_v[l]])
        a_outs, o_b, o_c, o_dc, o_ds, o_dw = _token_mixer_branches(p, bias, _sink_rows(sinks[l]), pe, w1, w2)
        x, h2, comb_t = _mixer_out(x, a_outs, o_b, o_c, o_dc, o_ds, o_dw, gates, mix_norm_g[l],
                                   w_out[l].astype(BF16), g1, norm2_g[l], sc2, sh2, rw_t, router_bias)
        comb = comb_t.transpose(0, 2, 1)
        w_gu = jnp.concatenate([exp_w_gate[l], exp_w_up[l]], axis=-1).astype(BF16)
        x = _experts(x, h2, comb, w_gu, exp_w_down[l].astype(BF16), g2, final_g, l == DEPTH - 1)
    return x
```

```python
import functools
import math

import numpy as np
import jax
import jax.numpy as jnp
from jax import lax
from jax.experimental import pallas as pl
from jax.experimental.pallas import tpu as pltpu

F32 = jnp.float32
BF16 = jnp.bfloat16

D_MODEL = 2048
DEPTH = 4
HEAD_DIM = 64
N_MIXERS = 4
HEADS = 8
GROUP_WIDTH = HEADS * HEAD_DIM
SCALE = HEAD_DIM ** -0.5
REL_BUCKETS = 32
REL_MAX_DIST = 2048
DILATED_CONFIGS = ((128, 1), (512, 4), (2048, 16))
SWA_WINDOW = 128
KV_HEADS = 2
GQA = HEADS // KV_HEADS
MOBA_BLOCK = 256
MOBA_TOPK = 3
NSA_CMP_BLOCK = 32
NSA_CMP_STRIDE = 16
NSA_CMP_HIDDEN = 128
NSA_SLC_BLOCK = 64
NSA_SLC_TOPN = 16
NSA_WINDOW = 512
N_EXPERTS = 16
N_EXPERT_GROUPS = 4
PER_GROUP = N_EXPERTS // N_EXPERT_GROUPS
D_EXPERT = 512
NORM_EPS = 1e-6
NEG_INF = -1e30
TINY = float(np.finfo(np.float32).tiny)

N_PROJ_HEADS = 80
PROJ_COLS = N_PROJ_HEADS * HEAD_DIM
N_GATE_COLS = HEADS * 3
LANES = 128
SUBLANES = 8
QA, KA, VA = 0, 8, 16
QB, KB, VB = 24, 32, 34
QC, KC, VC = 36, 44, 52
QD, KDC, VDC, KDS, VDS, KDW, VDW = 60, 68, 70, 72, 74, 76, 78

VMEM_LIMIT = 56 * 1024 * 1024


def _cparams(n_axes):
    return pltpu.CompilerParams(dimension_semantics=("arbitrary",) * n_axes,
                                vmem_limit_bytes=VMEM_LIMIT)


def _mod_kernel(c_ref, w_ref, b_ref, o_ref):
    c = c_ref[...]
    ca = (c * jax.nn.sigmoid(c)).astype(BF16)
    o_ref[0] = jnp.dot(ca, w_ref[0].astype(BF16), preferred_element_type=F32) + b_ref[0]


def _modulation(c, ada_w, ada_b):
    depth, d, n = ada_w.shape
    b = c.shape[0]
    bp = 8
    tn = 1024
    cp = jnp.pad(c, ((0, bp - b), (0, 0)))
    out = pl.pallas_call(
        _mod_kernel,
        grid=(depth, n // tn),
        in_specs=[pl.BlockSpec((bp, d), lambda l, j: (0, 0)),
                  pl.BlockSpec((1, d, tn), lambda l, j: (l, 0, j)),
                  pl.BlockSpec((1, 1, tn), lambda l, j: (l, 0, j))],
        out_specs=pl.BlockSpec((1, bp, tn), lambda l, j: (l, 0, j)),
        out_shape=jax.ShapeDtypeStruct((depth, bp, n), F32),
        compiler_params=_cparams(2),
        name="adaln_mod",
    )(cp, ada_w, ada_b.reshape(depth, 1, n))
    return out[:, :b]


def _inproj_kernel(x_ref, g_ref, sc_ref, sh_ref, w_ref, wg_ref, p_ref, gate_ref, h_scr):
    n = pl.program_id(2)

    @pl.when(n == 0)
    def _():
        x = x_ref[0]
        ms = jnp.mean(x * x, axis=-1, keepdims=True)
        h = (x * lax.rsqrt(ms + NORM_EPS)) * g_ref[...] * (1.0 + sc_ref[0]) + sh_ref[0]
        hb = h.astype(BF16)
        h_scr[...] = hb
        gate_ref[0] = jax.nn.sigmoid(jnp.dot(hb, wg_ref[...], preferred_element_type=F32))

    res = jnp.dot(h_scr[...], w_ref[...], preferred_element_type=F32)
    for j in range(HEADS):
        p_ref[0, j] = res[:, j * HEAD_DIM:(j + 1) * HEAD_DIM].astype(BF16)


def _in_projection(x, g, sc, sh, w_heads, w_gate):
    b, s, d = x.shape
    tm = min(1024, s)
    tn = HEADS * HEAD_DIM
    return pl.pallas_call(
        _inproj_kernel,
        grid=(b, s // tm, PROJ_COLS // tn),
        in_specs=[pl.BlockSpec((1, tm, d), lambda i, m, n: (i, m, 0)),
                  pl.BlockSpec((1, d), lambda i, m, n: (0, 0)),
                  pl.BlockSpec((1, 1, d), lambda i, m, n: (i, 0, 0)),
                  pl.BlockSpec((1, 1, d), lambda i, m, n: (i, 0, 0)),
                  pl.BlockSpec((d, tn), lambda i, m, n: (0, n)),
                  pl.BlockSpec((d, LANES), lambda i, m, n: (0, 0))],
        out_specs=[pl.BlockSpec((1, HEADS, tm, HEAD_DIM), lambda i, m, n: (i, n, m, 0)),
                   pl.BlockSpec((1, tm, LANES), lambda i, m, n: (i, m, 0))],
        out_shape=[jax.ShapeDtypeStruct((b, N_PROJ_HEADS, s, HEAD_DIM), BF16),
                   jax.ShapeDtypeStruct((b, s, LANES), F32)],
        scratch_shapes=[pltpu.VMEM((tm, d), BF16)],
        compiler_params=_cparams(3),
        name="in_proj",
    )(x, g.reshape(1, d), sc.reshape(b, 1, d), sh.reshape(b, 1, d), w_heads, w_gate)


def _bucket_thresholds():
    d = np.arange(REL_MAX_DIST + 1)
    max_exact = REL_BUCKETS // 2
    large = max_exact + (np.log(np.maximum(d, 1).astype(np.float32) / np.float32(max_exact))
                         / np.float32(math.log(REL_MAX_DIST / max_exact))
                         * np.float32(REL_BUCKETS - max_exact)).astype(np.int32)
    bucket = np.where(d < max_exact, d, np.minimum(large, REL_BUCKETS - 1))
    return [int(np.argmax(bucket >= b)) for b in range(REL_BUCKETS)]


def _bias_kernel(off_ref, cmin_ref, tab_ref, o_ref, *, max_dist, dist_scale, thresholds):
    h = pl.program_id(0)
    t = pl.program_id(1)
    shape = o_ref.shape[-2:]
    rel = off_ref[t] + lax.broadcasted_iota(jnp.int32, shape, 0) - lax.broadcasted_iota(jnp.int32, shape, 1)
    acc = jnp.full(shape, tab_ref[h, 0], F32)
    for b in range(1, REL_BUCKETS):
        acc = jnp.where(rel >= -(-thresholds[b] // dist_scale), tab_ref[h, b], acc)
    col = lax.broadcasted_iota(jnp.int32, shape, 1)
    valid = (rel >= 0) & (rel <= max_dist) & (col >= cmin_ref[t])
    o_ref[...] = jnp.where(valid, acc, NEG_INF).reshape(o_ref.shape)


def _bias_tiles(tab, offsets, col_mins, rows, cols, max_dist, dist_scale, group):
    n_heads = tab.shape[0]
    n_tiles = len(offsets)
    smem = pl.BlockSpec(memory_space=pltpu.SMEM)
    out = pl.pallas_call(
        functools.partial(_bias_kernel, max_dist=max_dist, dist_scale=dist_scale,
                          thresholds=_bucket_thresholds()),
        grid=(n_heads, n_tiles),
        in_specs=[smem, smem, smem],
        out_specs=pl.BlockSpec((1, 1, 1, rows, cols), lambda h, t: (h // group, t, h % group, 0, 0)),
        out_shape=jax.ShapeDtypeStruct((n_heads // group, n_tiles, group, rows, cols), F32),
        compiler_params=_cparams(2),
        name="bias_tiles",
    )(jnp.asarray(offsets, jnp.int32), jnp.asarray(col_mins, jnp.int32), tab.astype(F32))
    return out.reshape(n_heads // group, n_tiles, group * rows, cols)


BAND_TILE = 128
BAND_STEP = 512


def _banded_kernel(*refs, units, r_heads, nsub, w, has_sink, emit_lse):
    q_ref, kp_ref, kc_ref, vp_ref, vc_ref, b_ref = refs[:6]
    pos = 6
    sink_ref = None
    if has_sink:
        sink_ref = refs[pos]
        pos += 1
    o_ref, kw_scr, vw_scr = refs[pos:pos + 3]
    t = pl.program_id(2)
    tq = BAND_TILE
    rows = r_heads * tq
    nk = w + tq
    kw_scr[:, :w] = kp_ref[0]
    kw_scr[:, w:] = kc_ref[0]
    vw_scr[:, :w] = vp_ref[0]
    vw_scr[:, w:] = vc_ref[0]
    for u in range(units):
        heads = slice(u * r_heads, (u + 1) * r_heads)
        for j in range(nsub):
            q = q_ref[0, heads, j * tq:(j + 1) * tq, :].reshape(rows, HEAD_DIM) * SCALE
            k = kw_scr[u, j * tq:j * tq + nk, :]
            v = vw_scr[u, j * tq:j * tq + nk, :]
            tile = jnp.minimum(t * nsub + j, w // tq)
            s = lax.dot_general(q, k, (((1,), (1,)), ((), ())), preferred_element_type=F32) + b_ref[u, tile]
            m = jnp.max(s, axis=-1, keepdims=True)
            if has_sink:
                m = jnp.maximum(m, sink_ref[u])
            e = jnp.exp(s - m)
            den = jnp.sum(e, axis=-1, keepdims=True)
            if has_sink:
                den = den + jnp.exp(sink_ref[u] - m)
            den = jnp.maximum(den, TINY)
            o = jnp.dot(e.astype(BF16), v, preferred_element_type=F32) / den
            if emit_lse:
                o = jnp.concatenate([o, jnp.broadcast_to(m + jnp.log(den), (rows, HEAD_DIM))], axis=-1)
            o_ref[0, heads, j * tq:(j + 1) * tq, :] = o.reshape(r_heads, tq, o.shape[-1])


def _band_bias_tiles(tab, max_dist, w, dist_scale, group):
    tq = BAND_TILE
    n_early = w // tq
    return _bias_tiles(tab, [w] * (n_early + 1), [w - e * tq for e in range(n_early)] + [0], tq, w + tq,
                       max_dist, dist_scale, group)


def _banded_attention(q_arr, k_arr, v_arr, bias, *, grid01, q_map, kv_maps, bias_map, r_heads, units, w,
                      out_heads, sink=None, emit_lse=False, name):
    seq = q_arr.shape[2]
    tq = BAND_TILE
    step = min(BAND_STEP, seq)
    nsub = step // tq
    assert seq % step == 0 and w % tq == 0 and step % w == 0
    k_map, v_map = kv_maps
    prev = lambda t: jnp.maximum(t * (step // w) - 1, 0)
    in_specs = [pl.BlockSpec((1, units * r_heads, step, HEAD_DIM), lambda i, j, t: q_map(i, j) + (t, 0)),
                pl.BlockSpec((1, units, w, HEAD_DIM), lambda i, j, t: k_map(i, j) + (prev(t), 0)),
                pl.BlockSpec((1, units, step, HEAD_DIM), lambda i, j, t: k_map(i, j) + (t, 0)),
                pl.BlockSpec((1, units, w, HEAD_DIM), lambda i, j, t: v_map(i, j) + (prev(t), 0)),
                pl.BlockSpec((1, units, step, HEAD_DIM), lambda i, j, t: v_map(i, j) + (t, 0)),
                pl.BlockSpec((units, w // tq + 1, r_heads * tq, w + tq), lambda i, j, t: (bias_map(i, j), 0, 0, 0))]
    args = [q_arr, k_arr, k_arr, v_arr, v_arr, bias]
    if sink is not None:
        in_specs.append(pl.BlockSpec((units, r_heads * tq, 1), lambda i, j, t: (bias_map(i, j), 0, 0)))
        args.append(sink)
    width = 2 * HEAD_DIM if emit_lse else HEAD_DIM
    n0, n1 = grid01
    return pl.pallas_call(
        functools.partial(_banded_kernel, units=units, r_heads=r_heads, nsub=nsub, w=w,
                          has_sink=sink is not None, emit_lse=emit_lse),
        grid=(n0, n1, seq // step),
        in_specs=in_specs,
        out_specs=pl.BlockSpec((1, units * r_heads, step, width), lambda i, j, t: (i, j, t, 0)),
        out_shape=jax.ShapeDtypeStruct((n0, out_heads, seq, width), F32),
        scratch_shapes=[pltpu.VMEM((units, w + step, HEAD_DIM), BF16),
                        pltpu.VMEM((units, w + step, HEAD_DIM), BF16)],
        compiler_params=_cparams(3),
        name=name,
    )(*args)


def _ranks_along_rows(vals):
    n, t = vals.shape
    groups = [vals[g * SUBLANES:(g + 1) * SUBLANES] for g in range(n // SUBLANES)]
    row = lax.broadcasted_iota(jnp.int32, (SUBLANES, t), 0)
    later = [jnp.where(row > r, 1, 0) for r in range(SUBLANES)]
    cnt = [jnp.zeros((SUBLANES, t), jnp.int32) for _ in groups]
    for i in range(n):
        vi = vals[i:i + 1]
        for g, vg in enumerate(groups):
            if g > i // SUBLANES:
                cnt[g] = cnt[g] + jnp.where(vi >= vg, 1, 0)
            elif g < i // SUBLANES:
                cnt[g] = cnt[g] + jnp.where(vi > vg, 1, 0)
            else:
                cnt[g] = cnt[g] + jnp.where(vi > vg, 1, 0) + jnp.where(vi == vg, later[i % SUBLANES], 0)
    return jnp.concatenate(cnt, axis=0)


def _rows_to_lanes(x_t, width):
    n, t = x_t.shape
    padded = jnp.concatenate([x_t, jnp.zeros((LANES - n, t), x_t.dtype)], axis=0)
    return padded.T[:, :width]


def _for_tiles(n, fn, carry):
    def body4(j, c):
        for u in range(4):
            c = fn(j * 4 + u, c)
        return c

    carry = lax.fori_loop(0, n // 4, body4, carry)
    base = (n // 4) * 4
    carry = lax.cond((n & 2) != 0, lambda c: fn(base + 1, fn(base, c)), lambda c: c, carry)
    base = base + (n & 2)
    return lax.cond((n & 1) != 0, lambda c: fn(base, c), lambda c: c, carry)


def _flash_selected(q_augs, qi, kaug_scr, kv_of, v_ref, b_ref, s_scr, tq):
    nh = len(q_augs)
    half = tq // 2
    n = qi + 1

    def logits(kj, mruns):
        ks = pl.multiple_of(kj * tq, tq)
        out = []
        for h in range(nh):
            s = lax.dot_general(q_augs[h], kaug_scr[kv_of[h], pl.ds(ks, tq), :], (((1,), (1,)), ((), ())),
                                preferred_element_type=F32) + b_ref[h, qi - kj]
            s_scr[h, kj] = s
            out.append(jnp.maximum(mruns[h], jnp.maximum(s[:, :half], s[:, half:])))
        return tuple(out)

    mruns = _for_tiles(n, logits, tuple(jnp.full((tq, half), NEG_INF, F32) for _ in range(nh)))
    ms = [jnp.max(mr, axis=-1, keepdims=True) for mr in mruns]

    def weigh(kj, carry):
        ks = pl.multiple_of(kj * tq, tq)
        out = []
        for h in range(nh):
            lrun, acc = carry[h]
            e = jnp.exp(s_scr[h, kj] - ms[h])
            lrun = lrun + (e[:, :half] + e[:, half:])
            acc = acc + jnp.dot(e.astype(BF16), v_ref[0, kv_of[h], pl.ds(ks, tq), :], preferred_element_type=F32)
            out.append((lrun, acc))
        return tuple(out)

    init = tuple((jnp.zeros((tq, half), F32), jnp.zeros((tq, HEAD_DIM), F32)) for _ in range(nh))
    res = _for_tiles(n, weigh, init)
    return [acc / jnp.maximum(jnp.sum(lrun, axis=-1, keepdims=True), TINY) for lrun, acc in res]


def _fill_kaug(kaug_scr, k_ref, et_ref):
    for kv in range(kaug_scr.shape[0]):
        kaug_scr[kv, :, :HEAD_DIM] = k_ref[0, kv]
        kaug_scr[kv, :, HEAD_DIM:] = et_ref[...]


def _augment(q, pen):
    return jnp.concatenate([q * SCALE, pen.astype(BF16)], axis=-1)


def _moba_kernel(q_ref, k_ref, v_ref, pool_ref, et_ref, b_ref, o_ref, kaug_scr, kmean_scr, s_scr, *, tq, nblk, nh):
    qi = pl.program_id(2)

    @pl.when(qi == 0)
    def _():
        _fill_kaug(kaug_scr, k_ref, et_ref)
        for h in range(nh):
            kmean_scr[h] = jnp.dot(pool_ref[...], k_ref[0, h], preferred_element_type=F32).astype(BF16)

    q_augs = []
    for h in range(nh):
        q = q_ref[0, h]
        gate = lax.dot_general(kmean_scr[h], q, (((1,), (1,)), ((), ())), preferred_element_type=F32)
        blk = lax.broadcasted_iota(jnp.int32, (nblk, tq), 0)
        gate = jnp.where(blk < qi, gate, NEG_INF)
        picked = ((_ranks_along_rows(gate) < MOBA_TOPK) & (blk < qi)) | (blk == qi)
        pen_t = jnp.where(picked, 0.0, NEG_INF)
        q_augs.append(_augment(q, _rows_to_lanes(pen_t, HEAD_DIM)))
    outs = _flash_selected(q_augs, qi, kaug_scr, tuple(range(nh)), v_ref, b_ref, s_scr, tq)
    for h in range(nh):
        o_ref[0, h] = outs[h]


def _slc_kernel(q_ref, k_ref, v_ref, pen_ref, et_ref, b_ref, o_ref, kaug_scr, s_scr, *, tq, nh):
    qi = pl.program_id(2)

    @pl.when(qi == 0)
    def _():
        _fill_kaug(kaug_scr, k_ref, et_ref)

    q_augs = [_augment(q_ref[0, h], pen_ref[0, 0]) for h in range(nh)]
    outs = _flash_selected(q_augs, qi, kaug_scr, (0,) * nh, v_ref, b_ref, s_scr, tq)
    for h in range(nh):
        o_ref[0, h] = outs[h]


FLASH_TILE = 256
FLASH_HEADS = 2


def _flash_bias_tiles(tab, seq):
    n_off = seq // FLASH_TILE
    return _bias_tiles(tab, [o * FLASH_TILE for o in range(n_off)], [0] * n_off, FLASH_TILE, FLASH_TILE,
                       seq, 1, 1)


def _block_onehot_t(nblk, seq):
    blk = seq // nblk
    return jnp.asarray((np.arange(seq)[:, None] // blk == np.arange(HEAD_DIM)[None, :]).astype(np.float32), BF16)


def _moba_attention(p, bias):
    b, _, s, _ = p.shape
    tq, nh = FLASH_TILE, FLASH_HEADS
    assert tq == MOBA_BLOCK
    nblk = s // tq
    nrow = max(nblk, 16)
    pool = jnp.asarray((np.arange(s)[None, :] // tq == np.arange(nrow)[:, None]).astype(np.float32) / tq, BF16)
    kv_spec = lambda base: pl.BlockSpec((1, nh, s, HEAD_DIM), lambda i, h, t: (i, base // nh + h, 0, 0))
    return pl.pallas_call(
        functools.partial(_moba_kernel, tq=tq, nblk=nrow, nh=nh),
        grid=(b, HEADS // nh, s // tq),
        in_specs=[pl.BlockSpec((1, nh, tq, HEAD_DIM), lambda i, h, t: (i, QC // nh + h, t, 0)),
                  kv_spec(KC), kv_spec(VC),
                  pl.BlockSpec((nrow, s), lambda i, h, t: (0, 0)),
                  pl.BlockSpec((s, HEAD_DIM), lambda i, h, t: (0, 0)),
                  pl.BlockSpec((nh, s // tq, tq, tq), lambda i, h, t: (h, 0, 0, 0))],
        out_specs=pl.BlockSpec((1, nh, tq, HEAD_DIM), lambda i, h, t: (i, h, t, 0)),
        out_shape=jax.ShapeDtypeStruct((b, HEADS, s, HEAD_DIM), F32),
        scratch_shapes=[pltpu.VMEM((nh, s, 2 * HEAD_DIM), BF16),
                        pltpu.VMEM((nh, nrow, HEAD_DIM), BF16),
                        pltpu.VMEM((nh, s // tq, tq, tq), F32)],
        compiler_params=_cparams(3),
        name="moba",
    )(p, p, p, pool, _block_onehot_t(nblk, s), bias)


def _nsa_selected_attention(p, pen, bias):
    b, _, s, _ = p.shape
    tq, nh = FLASH_TILE, FLASH_HEADS
    nblk = s // NSA_SLC_BLOCK
    assert nblk <= HEAD_DIM and GQA % nh == 0
    kv_spec = lambda base: pl.BlockSpec((1, 1, s, HEAD_DIM), lambda i, h, t: (i, base + (h * nh) // GQA, 0, 0))
    return pl.pallas_call(
        functools.partial(_slc_kernel, tq=tq, nh=nh),
        grid=(b, HEADS // nh, s // tq),
        in_specs=[pl.BlockSpec((1, nh, tq, HEAD_DIM), lambda i, h, t: (i, QD // nh + h, t, 0)),
                  kv_spec(KDS), kv_spec(VDS),
                  pl.BlockSpec((1, 1, tq, HEAD_DIM), lambda i, h, t: (i, (h * nh) // GQA, t, 0)),
                  pl.BlockSpec((s, HEAD_DIM), lambda i, h, t: (0, 0)),
                  pl.BlockSpec((nh, s // tq, tq, tq), lambda i, h, t: (h, 0, 0, 0))],
        out_specs=pl.BlockSpec((1, nh, tq, HEAD_DIM), lambda i, h, t: (i, h, t, 0)),
        out_shape=jax.ShapeDtypeStruct((b, HEADS, s, HEAD_DIM), F32),
        scratch_shapes=[pltpu.VMEM((1, s, 2 * HEAD_DIM), BF16),
                        pltpu.VMEM((nh, s // tq, tq, tq), F32)],
        compiler_params=_cparams(3),
        name="nsa_selected",
    )(p, p, p, pen, _block_onehot_t(nblk, s), bias)


def _compress_kernel(t_ref, pe_ref, w1_ref, w2_ref, o_ref):
    t = t_ref[0, 0].astype(F32)
    half = t.shape[-1]
    top = (t + pe_ref[0, 0:1, :]).astype(BF16)
    bot = (t + pe_ref[0, 1:2, :]).astype(BF16)
    u = jnp.dot(top, w1_ref[0, :half, :], preferred_element_type=F32)
    vv = jnp.dot(bot, w1_ref[0, half:, :], preferred_element_type=F32)
    pre = u + pltpu.roll(vv, vv.shape[0] - 1, axis=0)
    hid = jax.nn.gelu(pre).astype(BF16)
    o_ref[0, 0] = jnp.dot(hid, w2_ref[0], preferred_element_type=F32).astype(BF16)


def _nsa_compress(p, pe, w1, w2):
    b, _, s, _ = p.shape
    n_half = s // NSA_CMP_STRIDE
    half = NSA_CMP_STRIDE * HEAD_DIM
    t = p[:, KDC:KDC + 2 * KV_HEADS].reshape(b, 2 * KV_HEADS, n_half, half)
    pe2 = pe.reshape(2, 2, half)
    return pl.pallas_call(
        _compress_kernel,
        grid=(b, 2 * KV_HEADS),
        in_specs=[pl.BlockSpec((1, 1, n_half, half), lambda i, j: (i, j, 0, 0)),
                  pl.BlockSpec((1, 2, half), lambda i, j: (j // KV_HEADS, 0, 0)),
                  pl.BlockSpec((1, 2 * half, NSA_CMP_HIDDEN), lambda i, j: (j // KV_HEADS, 0, 0)),
                  pl.BlockSpec((1, NSA_CMP_HIDDEN, HEAD_DIM), lambda i, j: (j // KV_HEADS, 0, 0))],
        out_specs=pl.BlockSpec((1, 1, n_half, HEAD_DIM), lambda i, j: (i, j, 0, 0)),
        out_shape=jax.ShapeDtypeStruct((b, 2 * KV_HEADS, n_half, HEAD_DIM), BF16),
        compiler_params=_cparams(2),
        name="nsa_compress",
    )(t, pe2, w1.astype(BF16), w2.astype(BF16))


def _cmp_attn_kernel(q_ref, kc_ref, vc_ref, ov_ref, o_ref, pen_ref, *, tq, n_slc):
    qi = pl.program_id(2)
    rows = GQA * tq
    q = q_ref[0].reshape(rows, HEAD_DIM)
    kc = kc_ref[0, 0]
    n_cmp = kc.shape[0]
    s = lax.dot_general(q, kc, (((1,), (1,)), ((), ())), preferred_element_type=F32) * SCALE
    row = lax.broadcasted_iota(jnp.int32, (rows, n_cmp), 0)
    pos = qi * tq + (row & (tq - 1))
    blk_end = lax.broadcasted_iota(jnp.int32, (rows, n_cmp), 1) * NSA_CMP_STRIDE + (NSA_CMP_BLOCK - 1)
    mask = blk_end <= pos
    s = jnp.where(mask, s, NEG_INF)
    m = jnp.max(s, axis=-1, keepdims=True)
    e = jnp.where(mask, jnp.exp(s - m), 0.0)
    den = jnp.maximum(jnp.sum(e, axis=-1, keepdims=True), TINY)
    pr = e / den
    o_ref[0] = jnp.dot(pr.astype(BF16), vc_ref[0, 0], preferred_element_type=F32).reshape(GQA, tq, HEAD_DIM)
    psum = pr[0:tq]
    for g in range(1, GQA):
        psum = psum + pr[g * tq:(g + 1) * tq]
    ov_t = ov_ref[...]
    hi = psum.astype(BF16)
    r1 = psum - hi.astype(F32)
    mid = r1.astype(BF16)
    lo = (r1 - mid.astype(F32)).astype(BF16)
    nt = (((1,), (1,)), ((), ()))
    imp_t = (lax.dot_general(ov_t, hi, nt, preferred_element_type=F32)
             + lax.dot_general(ov_t, mid, nt, preferred_element_type=F32)
             + lax.dot_general(ov_t, lo, nt, preferred_element_type=F32))
    jj = lax.broadcasted_iota(jnp.int32, (n_slc, tq), 0)
    cur = (qi * tq + lax.broadcasted_iota(jnp.int32, (n_slc, tq), 1)) // NSA_SLC_BLOCK
    forced = (jj == 0) | (jj == cur) | (jj == cur - 1)
    imp_t = jnp.where(forced, -NEG_INF, jnp.where(jj <= cur, imp_t, NEG_INF))
    picked = (_ranks_along_rows(imp_t) < NSA_SLC_TOPN) & (jj <= cur)
    pen_ref[0, 0] = _rows_to_lanes(jnp.where(picked, 0.0, NEG_INF), HEAD_DIM).astype(BF16)


def _nsa_compressed_attention(p, kvc):
    b, _, s, _ = p.shape
    tq = 128
    n_cmp = kvc.shape[2]
    n_slc = s // NSA_SLC_BLOCK
    assert n_slc == HEAD_DIM
    c_start = np.arange(n_cmp) * NSA_CMP_STRIDE
    j_start = np.arange(n_slc) * NSA_SLC_BLOCK
    overlap = ((c_start[:, None] < j_start[None, :] + NSA_SLC_BLOCK)
               & (c_start[:, None] + NSA_CMP_BLOCK > j_start[None, :])).astype(np.float32)
    overlap[n_cmp - 1] = 0.0
    return pl.pallas_call(
        functools.partial(_cmp_attn_kernel, tq=tq, n_slc=n_slc),
        grid=(b, KV_HEADS, s // tq),
        in_specs=[pl.BlockSpec((1, GQA, tq, HEAD_DIM), lambda i, h, t: (i, QD // GQA + h, t, 0)),
                  pl.BlockSpec((1, 1, n_cmp, HEAD_DIM), lambda i, h, t: (i, h, 0, 0)),
                  pl.BlockSpec((1, 1, n_cmp, HEAD_DIM), lambda i, h, t: (i, KV_HEADS + h, 0, 0)),
                  pl.BlockSpec((n_slc, n_cmp), lambda i, h, t: (0, 0))],
        out_specs=[pl.BlockSpec((1, GQA, tq, HEAD_DIM), lambda i, h, t: (i, h, t, 0)),
                   pl.BlockSpec((1, 1, tq, HEAD_DIM), lambda i, h, t: (i, h, t, 0))],
        out_shape=[jax.ShapeDtypeStruct((b, HEADS, s, HEAD_DIM), F32),
                   jax.ShapeDtypeStruct((b, KV_HEADS, s, HEAD_DIM), BF16)],
        compiler_params=_cparams(3),
        name="nsa_compressed",
    )(p, kvc, kvc, jnp.asarray(overlap.T, BF16))


def _heads_to_lanes(ref, width=HEAD_DIM, offset=0):
    return jnp.concatenate([ref[0, h][:, offset:offset + width] for h in range(HEADS)], axis=-1)


def _group_norm(o, g):
    ms = jnp.mean(o * o, axis=-1, keepdims=True)
    return (o * lax.rsqrt(ms + NORM_EPS)) * g


def _route(logits_t, bias_t):
    aff = jax.nn.sigmoid(logits_t)
    score = aff + bias_t
    rows = [score[e:e + 1] for e in range(N_EXPERTS)]
    affs = [aff[e:e + 1] for e in range(N_EXPERTS)]
    gscore = []
    for g in range(N_EXPERT_GROUPS):
        mem = rows[g * PER_GROUP:(g + 1) * PER_GROUP]
        best = None
        for a in range(PER_GROUP):
            for c in range(a + 1, PER_GROUP):
                pair = mem[a] + mem[c]
                best = pair if best is None else jnp.maximum(best, pair)
        gscore.append(best)
    gsel = jnp.zeros_like(gscore[0], dtype=jnp.int32)
    gbest = gscore[0]
    for g in range(1, N_EXPERT_GROUPS):
        better = gscore[g] > gbest
        gsel = jnp.where(better, g, gsel)
        gbest = jnp.where(better, gscore[g], gbest)
    masked = [jnp.where(gsel == e // PER_GROUP, rows[e], NEG_INF) for e in range(N_EXPERTS)]

    def argbest(vals):
        idx = jnp.zeros_like(gsel)
        best = vals[0]
        for e in range(1, N_EXPERTS):
            better = vals[e] > best
            idx = jnp.where(better, e, idx)
            best = jnp.where(better, vals[e], best)
        return idx

    e1 = argbest(masked)
    e2 = argbest([jnp.where(e1 == e, -jnp.inf, masked[e]) for e in range(N_EXPERTS)])
    w1 = sum(jnp.where(e1 == e, affs[e], 0.0) for e in range(N_EXPERTS))
    w2 = sum(jnp.where(e2 == e, affs[e], 0.0) for e in range(N_EXPERTS))
    tot = w1 + w2
    w1 = w1 / tot
    w2 = w2 / tot
    comb = jnp.concatenate([jnp.where(e1 == e, w1, 0.0) + jnp.where(e2 == e, w2, 0.0)
                            for e in range(N_EXPERTS)], axis=0)
    return comb, gsel


def _out_kernel(x_ref, a0_ref, a1_ref, a2_ref, ob_ref, oc_ref, dc_ref, ds_ref, dw_ref, gate_ref,
                mixg_ref, wout_ref, g1_ref, n2g_ref, sc2_ref, sh2_ref, rw_ref, rb_ref,
                xo_ref, h2_ref, gsel_ref):
    outs = [_heads_to_lanes(r) for r in (a0_ref, a1_ref, a2_ref)]
    lses = [_heads_to_lanes(r, offset=HEAD_DIM) for r in (a0_ref, a1_ref, a2_ref)]
    mx = jnp.maximum(jnp.maximum(lses[0], lses[1]), lses[2])
    ws = [jnp.exp(l - mx) for l in lses]
    tot = ws[0] + ws[1] + ws[2]
    o_a = (ws[0] / tot) * outs[0] + (ws[1] / tot) * outs[1] + (ws[2] / tot) * outs[2]
    o_b = _heads_to_lanes(ob_ref)
    o_c = _heads_to_lanes(oc_ref)
    gates = gate_ref[0]
    tm = gates.shape[0]

    def gate_cols(i):
        return jnp.concatenate([jnp.broadcast_to(gates[:, h * 3 + i:h * 3 + i + 1], (tm, HEAD_DIM))
                                for h in range(HEADS)], axis=-1)

    o_d = (gate_cols(0) * _heads_to_lanes(dc_ref) + gate_cols(1) * _heads_to_lanes(ds_ref)
           + gate_cols(2) * _heads_to_lanes(dw_ref))
    mixed = jnp.concatenate([_group_norm(o, mixg_ref[i:i + 1, :]).astype(BF16)
                             for i, o in enumerate((o_a, o_b, o_c, o_d))], axis=-1)
    y = jnp.dot(mixed, wout_ref[...], preferred_element_type=F32)
    x = x_ref[0] + g1_ref[0] * y
    xo_ref[0] = x
    ms = jnp.mean(x * x, axis=-1, keepdims=True)
    h2 = (x * lax.rsqrt(ms + NORM_EPS)) * n2g_ref[...] * (1.0 + sc2_ref[0]) + sh2_ref[0]
    logits_t = lax.dot_general(rw_ref[...], h2.astype(BF16), (((1,), (1,)), ((), ())),
                               preferred_element_type=F32)
    comb_t, gsel = _route(logits_t, rb_ref[...])
    h2_ref[0] = jnp.concatenate([h2, _rows_to_lanes(comb_t, LANES)], axis=-1)
    gsel_ref[0] = gsel


def _mixer_out(x, a_outs, o_b, o_c, o_dc, o_ds, o_dw, gates, mix_g, w_out, g1, n2g, sc2, sh2, rw_t, rb):
    b, s, d = x.shape
    tm = min(256, s)
    head_spec = lambda width: pl.BlockSpec((1, HEADS, tm, width), lambda i, m: (i, 0, m, 0))
    vec_b = pl.BlockSpec((1, 1, d), lambda i, m: (i, 0, 0))
    return pl.pallas_call(
        _out_kernel,
        grid=(b, s // tm),
        in_specs=[pl.BlockSpec((1, tm, d), lambda i, m: (i, m, 0)),
                  head_spec(2 * HEAD_DIM), head_spec(2 * HEAD_DIM), head_spec(2 * HEAD_DIM),
                  head_spec(HEAD_DIM), head_spec(HEAD_DIM), head_spec(HEAD_DIM), head_spec(HEAD_DIM),
                  head_spec(HEAD_DIM),
                  pl.BlockSpec((1, tm, LANES), lambda i, m: (i, m, 0)),
                  pl.BlockSpec((N_MIXERS, GROUP_WIDTH), lambda i, m: (0, 0)),
                  pl.BlockSpec((N_MIXERS * GROUP_WIDTH, d), lambda i, m: (0, 0)),
                  vec_b,
                  pl.BlockSpec((1, d), lambda i, m: (0, 0)),
                  vec_b, vec_b,
                  pl.BlockSpec((N_EXPERTS, d), lambda i, m: (0, 0)),
                  pl.BlockSpec((N_EXPERTS, 1), lambda i, m: (0, 0))],
        out_specs=[pl.BlockSpec((1, tm, d), lambda i, m: (i, m, 0)),
                   pl.BlockSpec((1, tm, d + LANES), lambda i, m: (i, m, 0)),
                   pl.BlockSpec((1, 1, tm), lambda i, m: (i, 0, m))],
        out_shape=[jax.ShapeDtypeStruct((b, s, d), F32),
                   jax.ShapeDtypeStruct((b, s, d + LANES), F32),
                   jax.ShapeDtypeStruct((b, 1, s), jnp.int32)],
        compiler_params=_cparams(2),
        name="mixer_out",
    )(x, *a_outs, o_b, o_c, o_dc, o_ds, o_dw, gates, mix_g, w_out, g1.reshape(b, 1, d), n2g.reshape(1, d),
      sc2.reshape(b, 1, d), sh2.reshape(b, 1, d), rw_t, rb.reshape(N_EXPERTS, 1))


EXPERT_TILE = 512


def _dispatch_plan(gsel, tm):
    gid = gsel.reshape(-1)
    n_tok = gid.shape[0]
    onehot = (gid[:, None] == jnp.arange(N_EXPERT_GROUPS, dtype=jnp.int32)[None, :]).astype(jnp.int32)
    csum = jnp.cumsum(onehot, axis=0)
    rank = jnp.sum((csum - onehot) * onehot, axis=1)
    padded = ((csum[-1] + tm - 1) // tm) * tm
    ends = jnp.cumsum(padded)
    dest = jnp.sum(onehot * (ends - padded)[None, :], axis=1) + rank
    n_rows = n_tok + N_EXPERT_GROUPS * tm
    src = jnp.zeros((n_rows,), jnp.int32).at[dest].set(jnp.arange(n_tok, dtype=jnp.int32), unique_indices=True)
    tile_start = jnp.arange(n_rows // tm, dtype=jnp.int32) * tm
    tile_gid = jnp.minimum(jnp.sum((tile_start[:, None] >= ends[None, :]).astype(jnp.int32), axis=1),
                           N_EXPERT_GROUPS - 1)
    return dest.astype(jnp.int32), src, tile_gid, (ends[-1] // tm).reshape(1).astype(jnp.int32)


def _row_copy(src_hbm, row, buf, r, sem):
    return pltpu.make_async_copy(src_hbm.at[pl.ds(row, 1)], buf.at[pl.ds(r, 1)], sem)


def _gather_rows_into(idx_ref, base, n, src_hbm, buf, sem):
    def issue(r, c):
        _row_copy(src_hbm, idx_ref[base + r], buf, r, sem).start()
        return c

    lax.fori_loop(0, n, issue, 0)

    def wait(r, c):
        _row_copy(src_hbm, 0, buf, r, sem).wait()
        return c

    lax.fori_loop(0, n, wait, 0)


def _dispatch_kernel(src_ref, h_hbm, o_ref, sem, *, tm):
    _gather_rows_into(src_ref, pl.program_id(0) * tm, tm, h_hbm, o_ref, sem)


def _dispatch(h2ext, src, tm):
    n_rows = src.shape[0]
    width = h2ext.shape[1]
    return pl.pallas_call(
        functools.partial(_dispatch_kernel, tm=tm),
        grid_spec=pltpu.PrefetchScalarGridSpec(
            num_scalar_prefetch=1,
            grid=(n_rows // tm,),
            in_specs=[pl.BlockSpec(memory_space=pl.ANY)],
            out_specs=pl.BlockSpec((tm, width), lambda t, src_ref: (t, 0)),
            scratch_shapes=[pltpu.SemaphoreType.DMA(())]),
        out_shape=jax.ShapeDtypeStruct((n_rows, width), F32),
        compiler_params=_cparams(1),
        name="moe_dispatch",
    )(src, h2ext)


def _expert_kernel(gid_ref, nvalid_ref, hs_ref, wgu_ref, wd_ref, o_ref, h_scr, acc_ref):
    t = pl.program_id(0)
    e = pl.program_id(1)

    @pl.when(t < nvalid_ref[0])
    def _():
        @pl.when(e == 0)
        def _():
            h_scr[...] = hs_ref[:, :D_MODEL].astype(BF16)
            acc_ref[...] = jnp.zeros_like(acc_ref)

        gu = jnp.dot(h_scr[...], wgu_ref[0], preferred_element_type=F32)
        gate = gu[:, :D_EXPERT]
        he = (gate * jax.nn.sigmoid(gate)) * gu[:, D_EXPERT:]
        cw = hs_ref[:, D_MODEL:]
        col = lax.broadcasted_iota(jnp.int32, cw.shape, 1)
        w = jnp.sum(jnp.where(col == gid_ref[t] * PER_GROUP + e, cw, 0.0), axis=-1, keepdims=True)
        acc_ref[...] += w * jnp.dot(he.astype(BF16), wd_ref[0], preferred_element_type=F32)

        @pl.when(e == PER_GROUP - 1)
        def _():
            o_ref[...] = acc_ref[...]

    @pl.when((t >= nvalid_ref[0]) & (e == PER_GROUP - 1))
    def _():
        o_ref[...] = jnp.zeros_like(o_ref)


def _experts(hs, tile_gid, n_valid, w_gu, w_d, tm):
    n_rows = hs.shape[0]
    d = D_MODEL
    expert = lambda t, e, gid, nv: (gid[t] * PER_GROUP + e, 0, 0)
    return pl.pallas_call(
        _expert_kernel,
        grid_spec=pltpu.PrefetchScalarGridSpec(
            num_scalar_prefetch=2,
            grid=(n_rows // tm, PER_GROUP),
            in_specs=[pl.BlockSpec((tm, d + LANES), lambda t, e, gid, nv: (t, 0)),
                      pl.BlockSpec((1, d, 2 * D_EXPERT), expert),
                      pl.BlockSpec((1, D_EXPERT, d), expert)],
            out_specs=pl.BlockSpec((tm, d), lambda t, e, gid, nv: (t, 0)),
            scratch_shapes=[pltpu.VMEM((tm, d), BF16), pltpu.VMEM((tm, d), F32)]),
        out_shape=jax.ShapeDtypeStruct((n_rows, d), F32),
        compiler_params=_cparams(2),
        name="experts",
    )(tile_gid, n_valid, hs, w_gu, w_d)


def _combine_kernel(dest_ref, ys_hbm, x_ref, g2_ref, fg_ref, o_ref, buf, sem, *, tm, seq, final_norm):
    base = pl.program_id(0) * seq + pl.program_id(1) * tm
    _gather_rows_into(dest_ref, base, tm, ys_hbm, buf, sem)
    x = x_ref[0] + g2_ref[0] * buf[...]
    if final_norm:
        ms = jnp.mean(x * x, axis=-1, keepdims=True)
        x = (x * lax.rsqrt(ms + NORM_EPS)) * fg_ref[...]
    o_ref[0] = x


def _combine(x, ys, dest, g2, final_g, final_norm):
    b, s, d = x.shape
    tm = min(512, s)
    return pl.pallas_call(
        functools.partial(_combine_kernel, tm=tm, seq=s, final_norm=final_norm),
        grid_spec=pltpu.PrefetchScalarGridSpec(
            num_scalar_prefetch=1,
            grid=(b, s // tm),
            in_specs=[pl.BlockSpec(memory_space=pl.ANY),
                      pl.BlockSpec((1, tm, d), lambda i, m, dest_ref: (i, m, 0)),
                      pl.BlockSpec((1, 1, d), lambda i, m, dest_ref: (i, 0, 0)),
                      pl.BlockSpec((1, d), lambda i, m, dest_ref: (0, 0))],
            out_specs=pl.BlockSpec((1, tm, d), lambda i, m, dest_ref: (i, m, 0)),
            scratch_shapes=[pltpu.VMEM((tm, d), F32), pltpu.SemaphoreType.DMA(())]),
        out_shape=jax.ShapeDtypeStruct((b, s, d), F32),
        compiler_params=_cparams(2),
        name="moe_combine",
    )(dest, ys, x, g2.reshape(b, 1, d), final_g.reshape(1, d))


def _moe(x, h2ext, gsel, w_gu, w_d, g2, final_g, final_norm):
    b, s, _ = x.shape
    dest, src, tile_gid, n_valid = _dispatch_plan(gsel, EXPERT_TILE)
    hs = _dispatch(h2ext.reshape(b * s, -1), src, EXPERT_TILE)
    ys = _experts(hs, tile_gid, n_valid, w_gu, w_d, EXPERT_TILE)
    return _combine(x, ys, dest, g2, final_g, final_norm)


def _bias_tables(rel_bias, s):
    tabs = [rel_bias[:, i * HEADS:(i + 1) * HEADS].T for i in range(N_MIXERS)]
    t = {}
    for window, r in DILATED_CONFIGS:
        t["a%d" % r] = _band_bias_tiles(tabs[0], window // r, 128, r, 1)
    t["b"] = _band_bias_tiles(tabs[1], SWA_WINDOW - 1, 128, 1, GQA)
    t["c"] = _flash_bias_tiles(tabs[2], s)
    t["ds"] = _flash_bias_tiles(tabs[3], s)
    t["dw"] = _band_bias_tiles(tabs[3], NSA_WINDOW - 1, NSA_WINDOW, 1, GQA)
    return t


def _token_mixer_branches(p, bias, sink, pe, w1, w2):
    b, _, s, _ = p.shape
    a_units = 4
    a_outs = []
    for window, r in DILATED_CONFIGS:
        if r == 1:
            src, lead, (q0, k0, v0) = p, b, (QA, KA, VA)
        else:
            src = p[:, QA:QA + 3 * HEADS].reshape(b, 3 * HEADS, s // r, r, HEAD_DIM)
            src = src.transpose(0, 3, 1, 2, 4).reshape(b * r, 3 * HEADS, s // r, HEAD_DIM)
            lead, (q0, k0, v0) = b * r, (0, HEADS, 2 * HEADS)
        o = _banded_attention(src, src, src, bias["a%d" % r], grid01=(lead, HEADS // a_units),
                              q_map=lambda i, j, q0=q0: (i, q0 // a_units + j),
                              kv_maps=(lambda i, j, k0=k0: (i, k0 // a_units + j),
                                       lambda i, j, v0=v0: (i, v0 // a_units + j)),
                              bias_map=lambda i, j: j, r_heads=1, units=a_units, w=128, out_heads=HEADS,
                              emit_lse=True, name="dilated_r%d" % r)
        if r > 1:
            o = o.reshape(b, r, HEADS, s // r, 2 * HEAD_DIM).transpose(0, 2, 3, 1, 4)
            o = o.reshape(b, HEADS, s, 2 * HEAD_DIM)
        a_outs.append(o)
    o_b = _banded_attention(p, p, p, bias["b"], grid01=(b, 1),
                            q_map=lambda i, j: (i, QB // HEADS), kv_maps=(lambda i, j: (i, KB // KV_HEADS),
                                                                         lambda i, j: (i, VB // KV_HEADS)),
                            bias_map=lambda i, j: 0, r_heads=GQA, units=KV_HEADS, w=128, out_heads=HEADS,
                            sink=sink, name="swa")
    o_c = _moba_attention(p, bias["c"])
    kvc = _nsa_compress(p, pe, w1, w2)
    o_dc, pen = _nsa_compressed_attention(p, kvc)
    o_ds = _nsa_selected_attention(p, pen, bias["ds"])
    o_dw = _banded_attention(p, p, p, bias["dw"], grid01=(b, KV_HEADS),
                             q_map=lambda i, j: (i, QD // GQA + j), kv_maps=(lambda i, j: (i, KDW + j),
                                                                            lambda i, j: (i, VDW + j)),
                             bias_map=lambda i, j: j, r_heads=GQA, units=1, w=NSA_WINDOW, out_heads=HEADS,
                             name="nsa_window")
    return a_outs, o_b, o_c, o_dc, o_ds, o_dw


def _split_w_in(w_in_l):
    w_heads = w_in_l[:, :PROJ_COLS].astype(BF16)
    w_gate = jnp.pad(w_in_l[:, PROJ_COLS:], ((0, 0), (0, LANES - N_GATE_COLS))).astype(BF16)
    return w_heads, w_gate


def _sink_rows(sink_l):
    return jnp.repeat(sink_l.astype(F32).reshape(KV_HEADS, GQA), BAND_TILE, axis=1)[..., None]


def kernel(x, c, rel_bias, router_w, router_bias, norm1_g, norm2_g, ada_w, ada_b, w_in, nsa_pe_k, nsa_pe_v,
           nsa_cmp_w1_k, nsa_cmp_w2_k, nsa_cmp_w1_v, nsa_cmp_w2_v, sinks, mix_norm_g, w_out, exp_w_gate,
           exp_w_up, exp_w_down, final_g):
    b, s, d = x.shape
    mod = _modulation(c, ada_w, ada_b)
    bias = _bias_tables(rel_bias, s)
    rw_t = router_w.T.astype(BF16)
    for l in range(DEPTH):
        sh1, sc1, g1, sh2, sc2, g2 = jnp.split(mod[l], 6, axis=-1)
        w_heads, w_gate = _split_w_in(w_in[l])
        p, gates = _in_projection(x, norm1_g[l], sc1, sh1, w_heads, w_gate)
        pe = jnp.stack([nsa_pe_k[l], nsa_pe_v[l]])
        w1 = jnp.stack([nsa_cmp_w1_k[l], nsa_cmp_w1_v[l]])
        w2 = jnp.stack([nsa_cmp_w2_k[l], nsa_cmp_w2_v[l]])
        a_outs, o_b, o_c, o_dc, o_ds, o_dw = _token_mixer_branches(p, bias, _sink_rows(sinks[l]), pe, w1, w2)
        x, h2ext, gsel = _mixer_out(x, a_outs, o_b, o_c, o_dc, o_ds, o_dw, gates, mix_norm_g[l],
                                    w_out[l].astype(BF16), g1, norm2_g[l], sc2, sh2, rw_t, router_bias)
        w_gu = jnp.concatenate([exp_w_gate[l], exp_w_up[l]], axis=-1).astype(BF16)
        x = _moe(x, h2ext, gsel, w_gu, exp_w_down[l].astype(BF16), g2, final_g, l == DEPTH - 1)
    return x
```

```python
import functools
import math

import numpy as np
import jax
import jax.numpy as jnp
from jax import lax
from jax.experimental import pallas as pl
from jax.experimental.pallas import tpu as pltpu

F32 = jnp.float32
BF16 = jnp.bfloat16

D_MODEL = 2048
DEPTH = 4
HEAD_DIM = 64
N_MIXERS = 4
HEADS = 8
GROUP_WIDTH = HEADS * HEAD_DIM
SCALE = HEAD_DIM ** -0.5
REL_BUCKETS = 32
REL_MAX_DIST = 2048
DILATED_CONFIGS = ((128, 1), (512, 4), (2048, 16))
SWA_WINDOW = 128
KV_HEADS = 2
GQA = HEADS // KV_HEADS
MOBA_BLOCK = 256
MOBA_TOPK = 3
NSA_CMP_BLOCK = 32
NSA_CMP_STRIDE = 16
NSA_CMP_HIDDEN = 128
NSA_SLC_BLOCK = 64
NSA_SLC_TOPN = 16
NSA_WINDOW = 512
N_EXPERTS = 16
N_EXPERT_GROUPS = 4
PER_GROUP = N_EXPERTS // N_EXPERT_GROUPS
D_EXPERT = 512
NORM_EPS = 1e-6
NEG_INF = -1e30
TINY = float(np.finfo(np.float32).tiny)

N_PROJ_HEADS = 80
PROJ_COLS = N_PROJ_HEADS * HEAD_DIM
N_GATE_COLS = HEADS * 3
LANES = 128
SUBLANES = 8
QA, KA, VA = 0, 8, 16
QB, KB, VB = 24, 32, 34
QC, KC, VC = 36, 44, 52
QD, KDC, VDC, KDS, VDS, KDW, VDW = 60, 68, 70, 72, 74, 76, 78

VMEM_LIMIT = 56 * 1024 * 1024


def _cparams(n_axes):
    return pltpu.CompilerParams(dimension_semantics=("arbitrary",) * n_axes,
                                vmem_limit_bytes=VMEM_LIMIT)


def _mod_kernel(c_ref, w_ref, b_ref, o_ref):
    c = c_ref[...]
    ca = (c * jax.nn.sigmoid(c)).astype(BF16)
    o_ref[0] = jnp.dot(ca, w_ref[0].astype(BF16), preferred_element_type=F32) + b_ref[0]


def _modulation(c, ada_w, ada_b):
    depth, d, n = ada_w.shape
    b = c.shape[0]
    bp = 8
    tn = 1024
    cp = jnp.pad(c, ((0, bp - b), (0, 0)))
    out = pl.pallas_call(
        _mod_kernel,
        grid=(depth, n // tn),
        in_specs=[pl.BlockSpec((bp, d), lambda l, j: (0, 0)),
                  pl.BlockSpec((1, d, tn), lambda l, j: (l, 0, j)),
                  pl.BlockSpec((1, 1, tn), lambda l, j: (l, 0, j))],
        out_specs=pl.BlockSpec((1, bp, tn), lambda l, j: (l, 0, j)),
        out_shape=jax.ShapeDtypeStruct((depth, bp, n), F32),
        compiler_params=_cparams(2),
        name="adaln_mod",
    )(cp, ada_w, ada_b.reshape(depth, 1, n))
    return out[:, :b]


def _inproj_kernel(x_ref, g_ref, sc_ref, sh_ref, w_ref, wg_ref, p_ref, gate_ref, h_scr):
    n = pl.program_id(2)

    @pl.when(n == 0)
    def _():
        x = x_ref[0]
        ms = jnp.mean(x * x, axis=-1, keepdims=True)
        h = (x * lax.rsqrt(ms + NORM_EPS)) * g_ref[...] * (1.0 + sc_ref[0]) + sh_ref[0]
        hb = h.astype(BF16)
        h_scr[...] = hb
        gate_ref[0] = jax.nn.sigmoid(jnp.dot(hb, wg_ref[...], preferred_element_type=F32))

    res = jnp.dot(h_scr[...], w_ref[...], preferred_element_type=F32)
    for j in range(HEADS):
        p_ref[0, j] = res[:, j * HEAD_DIM:(j + 1) * HEAD_DIM].astype(BF16)


def _in_projection(x, g, sc, sh, w_heads, w_gate):
    b, s, d = x.shape
    tm = min(1024, s)
    tn = HEADS * HEAD_DIM
    return pl.pallas_call(
        _inproj_kernel,
        grid=(b, s // tm, PROJ_COLS // tn),
        in_specs=[pl.BlockSpec((1, tm, d), lambda i, m, n: (i, m, 0)),
                  pl.BlockSpec((1, d), lambda i, m, n: (0, 0)),
                  pl.BlockSpec((1, 1, d), lambda i, m, n: (i, 0, 0)),
                  pl.BlockSpec((1, 1, d), lambda i, m, n: (i, 0, 0)),
                  pl.BlockSpec((d, tn), lambda i, m, n: (0, n)),
                  pl.BlockSpec((d, LANES), lambda i, m, n: (0, 0))],
        out_specs=[pl.BlockSpec((1, HEADS, tm, HEAD_DIM), lambda i, m, n: (i, n, m, 0)),
                   pl.BlockSpec((1, tm, LANES), lambda i, m, n: (i, m, 0))],
        out_shape=[jax.ShapeDtypeStruct((b, N_PROJ_HEADS, s, HEAD_DIM), BF16),
                   jax.ShapeDtypeStruct((b, s, LANES), F32)],
        scratch_shapes=[pltpu.VMEM((tm, d), BF16)],
        compiler_params=_cparams(3),
        name="in_proj",
    )(x, g.reshape(1, d), sc.reshape(b, 1, d), sh.reshape(b, 1, d), w_heads, w_gate)


def _bucket_thresholds():
    d = np.arange(REL_MAX_DIST + 1)
    max_exact = REL_BUCKETS // 2
    large = max_exact + (np.log(np.maximum(d, 1).astype(np.float32) / np.float32(max_exact))
                         / np.float32(math.log(REL_MAX_DIST / max_exact))
                         * np.float32(REL_BUCKETS - max_exact)).astype(np.int32)
    bucket = np.where(d < max_exact, d, np.minimum(large, REL_BUCKETS - 1))
    return [int(np.argmax(bucket >= b)) for b in range(REL_BUCKETS)]


def _bias_kernel(off_ref, cmin_ref, tab_ref, o_ref, *, max_dist, dist_scale, thresholds):
    h = pl.program_id(0)
    t = pl.program_id(1)
    shape = o_ref.shape[-2:]
    rel = off_ref[t] + lax.broadcasted_iota(jnp.int32, shape, 0) - lax.broadcasted_iota(jnp.int32, shape, 1)
    acc = jnp.full(shape, tab_ref[h, 0], F32)
    for b in range(1, REL_BUCKETS):
        acc = jnp.where(rel >= -(-thresholds[b] // dist_scale), tab_ref[h, b], acc)
    col = lax.broadcasted_iota(jnp.int32, shape, 1)
    valid = (rel >= 0) & (rel <= max_dist) & (col >= cmin_ref[t])
    o_ref[...] = jnp.where(valid, acc, NEG_INF).reshape(o_ref.shape)


def _bias_tiles(tab, offsets, col_mins, rows, cols, max_dist, dist_scale, group):
    n_heads = tab.shape[0]
    n_tiles = len(offsets)
    smem = pl.BlockSpec(memory_space=pltpu.SMEM)
    out = pl.pallas_call(
        functools.partial(_bias_kernel, max_dist=max_dist, dist_scale=dist_scale,
                          thresholds=_bucket_thresholds()),
        grid=(n_heads, n_tiles),
        in_specs=[smem, smem, smem],
        out_specs=pl.BlockSpec((1, 1, 1, rows, cols), lambda h, t: (h // group, t, h % group, 0, 0)),
        out_shape=jax.ShapeDtypeStruct((n_heads // group, n_tiles, group, rows, cols), F32),
        compiler_params=_cparams(2),
        name="bias_tiles",
    )(jnp.asarray(offsets, jnp.int32), jnp.asarray(col_mins, jnp.int32), tab.astype(F32))
    return out.reshape(n_heads // group, n_tiles, group * rows, cols)


BAND_TILE = 128
BAND_STEP = 512


def _banded_kernel(*refs, units, r_heads, nsub, w, has_sink, emit_lse):
    q_ref, kp_ref, kc_ref, vp_ref, vc_ref, b_ref = refs[:6]
    pos = 6
    sink_ref = None
    if has_sink:
        sink_ref = refs[pos]
        pos += 1
    o_ref, kw_scr, vw_scr = refs[pos:pos + 3]
    t = pl.program_id(2)
    tq = BAND_TILE
    rows = r_heads * tq
    nk = w + tq
    kw_scr[:, :w] = kp_ref[0]
    kw_scr[:, w:] = kc_ref[0]
    vw_scr[:, :w] = vp_ref[0]
    vw_scr[:, w:] = vc_ref[0]
    for u in range(units):
        heads = slice(u * r_heads, (u + 1) * r_heads)
        for j in range(nsub):
            q = q_ref[0, heads, j * tq:(j + 1) * tq, :].reshape(rows, HEAD_DIM) * SCALE
            k = kw_scr[u, j * tq:j * tq + nk, :]
            v = vw_scr[u, j * tq:j * tq + nk, :]
            tile = jnp.minimum(t * nsub + j, w // tq)
            s = lax.dot_general(q, k, (((1,), (1,)), ((), ())), preferred_element_type=F32) + b_ref[u, tile]
            m = jnp.max(s, axis=-1, keepdims=True)
            if has_sink:
                m = jnp.maximum(m, sink_ref[u])
            e = jnp.exp(s - m)
            den = jnp.sum(e, axis=-1, keepdims=True)
            if has_sink:
                den = den + jnp.exp(sink_ref[u] - m)
            den = jnp.maximum(den, TINY)
            o = jnp.dot(e.astype(BF16), v, preferred_element_type=F32) / den
            if emit_lse:
                o = jnp.concatenate([o, jnp.broadcast_to(m + jnp.log(den), (rows, HEAD_DIM))], axis=-1)
            o_ref[0, heads, j * tq:(j + 1) * tq, :] = o.reshape(r_heads, tq, o.shape[-1])


def _band_bias_tiles(tab, max_dist, w, dist_scale, group):
    tq = BAND_TILE
    n_early = w // tq
    return _bias_tiles(tab, [w] * (n_early + 1), [w - e * tq for e in range(n_early)] + [0], tq, w + tq,
                       max_dist, dist_scale, group)


def _banded_attention(q_arr, k_arr, v_arr, bias, *, grid01, q_map, kv_maps, bias_map, r_heads, units, w,
                      out_heads, sink=None, emit_lse=False, name):
    seq = q_arr.shape[2]
    tq = BAND_TILE
    step = min(BAND_STEP, seq)
    nsub = step // tq
    assert seq % step == 0 and w % tq == 0 and step % w == 0
    k_map, v_map = kv_maps
    prev = lambda t: jnp.maximum(t * (step // w) - 1, 0)
    in_specs = [pl.BlockSpec((1, units * r_heads, step, HEAD_DIM), lambda i, j, t: q_map(i, j) + (t, 0)),
                pl.BlockSpec((1, units, w, HEAD_DIM), lambda i, j, t: k_map(i, j) + (prev(t), 0)),
                pl.BlockSpec((1, units, step, HEAD_DIM), lambda i, j, t: k_map(i, j) + (t, 0)),
                pl.BlockSpec((1, units, w, HEAD_DIM), lambda i, j, t: v_map(i, j) + (prev(t), 0)),
                pl.BlockSpec((1, units, step, HEAD_DIM), lambda i, j, t: v_map(i, j) + (t, 0)),
                pl.BlockSpec((units, w // tq + 1, r_heads * tq, w + tq), lambda i, j, t: (bias_map(i, j), 0, 0, 0))]
    args = [q_arr, k_arr, k_arr, v_arr, v_arr, bias]
    if sink is not None:
        in_specs.append(pl.BlockSpec((units, r_heads * tq, 1), lambda i, j, t: (bias_map(i, j), 0, 0)))
        args.append(sink)
    width = 2 * HEAD_DIM if emit_lse else HEAD_DIM
    n0, n1 = grid01
    return pl.pallas_call(
        functools.partial(_banded_kernel, units=units, r_heads=r_heads, nsub=nsub, w=w,
                          has_sink=sink is not None, emit_lse=emit_lse),
        grid=(n0, n1, seq // step),
        in_specs=in_specs,
        out_specs=pl.BlockSpec((1, units * r_heads, step, width), lambda i, j, t: (i, j, t, 0)),
        out_shape=jax.ShapeDtypeStruct((n0, out_heads, seq, width), F32),
        scratch_shapes=[pltpu.VMEM((units, w + step, HEAD_DIM), BF16),
                        pltpu.VMEM((units, w + step, HEAD_DIM), BF16)],
        compiler_params=_cparams(3),
        name=name,
    )(*args)


def _ranks_along_rows(vals):
    n, t = vals.shape
    groups = [vals[g * SUBLANES:(g + 1) * SUBLANES] for g in range(n // SUBLANES)]
    row = lax.broadcasted_iota(jnp.int32, (SUBLANES, t), 0)
    later = [jnp.where(row > r, 1, 0) for r in range(SUBLANES)]
    cnt = [jnp.zeros((SUBLANES, t), jnp.int32) for _ in groups]
    for i in range(n):
        vi = vals[i:i + 1]
        for g, vg in enumerate(groups):
            if g > i // SUBLANES:
                cnt[g] = cnt[g] + jnp.where(vi >= vg, 1, 0)
            elif g < i // SUBLANES:
                cnt[g] = cnt[g] + jnp.where(vi > vg, 1, 0)
            else:
                cnt[g] = cnt[g] + jnp.where(vi > vg, 1, 0) + jnp.where(vi == vg, later[i % SUBLANES], 0)
    return jnp.concatenate(cnt, axis=0)


def _rows_to_lanes(x_t, width):
    n, t = x_t.shape
    padded = jnp.concatenate([x_t, jnp.zeros((LANES - n, t), x_t.dtype)], axis=0)
    return padded.T[:, :width]


def _tile_groups(n, fn):
    def body(j, c):
        fn(j * 4, 4)
        return c

    lax.fori_loop(0, n // 4, body, 0)
    base = (n // 4) * 4

    @pl.when((n & 2) != 0)
    def _():
        fn(base, 2)

    @pl.when((n & 1) != 0)
    def _():
        fn(base + (n & 2), 1)


def _flash_selected(qa_scr, qi, kaug_scr, kv_of, v_ref, b_ref, s_scr, m_scr, l_scr, acc_scr, tq):
    nh = qa_scr.shape[0]
    half = tq // 2
    n = qi + 1
    nt = (((1,), (1,)), ((), ()))
    m_scr[...] = jnp.full(m_scr.shape, NEG_INF, F32)

    def logits(kj0, cnt):
        ks = pl.multiple_of(kj0 * tq, tq)
        for h in range(nh):
            s = lax.dot_general(qa_scr[h], kaug_scr[kv_of[h], pl.ds(ks, cnt * tq), :], nt,
                                preferred_element_type=F32)
            mloc = None
            for u in range(cnt):
                su = s[:, u * tq:(u + 1) * tq] + b_ref[h, qi - kj0 - u]
                s_scr[h, kj0 + u] = su
                mu = jnp.maximum(su[:, :half], su[:, half:])
                mloc = mu if mloc is None else jnp.maximum(mloc, mu)
            m_scr[h] = jnp.maximum(m_scr[h], mloc)

    _tile_groups(n, logits)
    for h in range(nh):
        m_scr[h] = jnp.broadcast_to(jnp.max(m_scr[h], axis=-1, keepdims=True), (tq, half))
    l_scr[...] = jnp.zeros(l_scr.shape, F32)
    acc_scr[...] = jnp.zeros(acc_scr.shape, F32)

    def weigh(kj0, cnt):
        ks = pl.multiple_of(kj0 * tq, tq)
        for h in range(nh):
            mb = m_scr[h]
            parts = []
            lloc = None
            for u in range(cnt):
                su = s_scr[h, kj0 + u]
                e_lo = jnp.exp(su[:, :half] - mb)
                e_hi = jnp.exp(su[:, half:] - mb)
                lu = e_lo + e_hi
                lloc = lu if lloc is None else lloc + lu
                parts += [e_lo.astype(BF16), e_hi.astype(BF16)]
            acc_scr[h] += jnp.dot(jnp.concatenate(parts, axis=-1), v_ref[0, kv_of[h], pl.ds(ks, cnt * tq), :],
                                  preferred_element_type=F32)
            l_scr[h] += lloc

    _tile_groups(n, weigh)
    return [acc_scr[h] / jnp.maximum(jnp.sum(l_scr[h], axis=-1, keepdims=True), TINY) for h in range(nh)]


def _fill_kaug(kaug_scr, k_ref, et_ref):
    for kv in range(kaug_scr.shape[0]):
        kaug_scr[kv, :, :HEAD_DIM] = k_ref[0, kv]
        kaug_scr[kv, :, HEAD_DIM:] = et_ref[...]


def _augment(q, pen):
    return jnp.concatenate([q * SCALE, pen.astype(BF16)], axis=-1)


def _moba_kernel(q_ref, k_ref, v_ref, pool_ref, et_ref, b_ref, o_ref, kaug_scr, kmean_scr, qa_scr, s_scr, m_scr,
                 l_scr, acc_scr, *, tq, nblk, nh):
    qi = pl.program_id(2)

    @pl.when(qi == 0)
    def _():
        _fill_kaug(kaug_scr, k_ref, et_ref)
        for h in range(nh):
            kmean_scr[h] = jnp.dot(pool_ref[...], k_ref[0, h], preferred_element_type=F32).astype(BF16)

    for h in range(nh):
        q = q_ref[0, h]
        gate = lax.dot_general(kmean_scr[h], q, (((1,), (1,)), ((), ())), preferred_element_type=F32)
        blk = lax.broadcasted_iota(jnp.int32, (nblk, tq), 0)
        gate = jnp.where(blk < qi, gate, NEG_INF)
        picked = ((_ranks_along_rows(gate) < MOBA_TOPK) & (blk < qi)) | (blk == qi)
        pen_t = jnp.where(picked, 0.0, NEG_INF)
        qa_scr[h] = _augment(q, _rows_to_lanes(pen_t, HEAD_DIM))
    outs = _flash_selected(qa_scr, qi, kaug_scr, tuple(range(nh)), v_ref, b_ref, s_scr, m_scr, l_scr, acc_scr, tq)
    for h in range(nh):
        o_ref[0, h] = outs[h]


def _slc_kernel(q_ref, k_ref, v_ref, pen_ref, et_ref, b_ref, o_ref, kaug_scr, qa_scr, s_scr, m_scr, l_scr, acc_scr,
                *, tq, nh):
    qi = pl.program_id(2)

    @pl.when(qi == 0)
    def _():
        _fill_kaug(kaug_scr, k_ref, et_ref)

    for h in range(nh):
        qa_scr[h] = _augment(q_ref[0, h], pen_ref[0, 0])
    outs = _flash_selected(qa_scr, qi, kaug_scr, (0,) * nh, v_ref, b_ref, s_scr, m_scr, l_scr, acc_scr, tq)
    for h in range(nh):
        o_ref[0, h] = outs[h]


def _flash_state(nh, n_tiles, tq):
    return [pltpu.VMEM((nh, tq, 2 * HEAD_DIM), BF16),
            pltpu.VMEM((nh, n_tiles, tq, tq), F32),
            pltpu.VMEM((nh, tq, tq // 2), F32),
            pltpu.VMEM((nh, tq, tq // 2), F32),
            pltpu.VMEM((nh, tq, HEAD_DIM), F32)]


FLASH_TILE = 256
FLASH_HEADS = 2


def _flash_bias_tiles(tab, seq):
    n_off = seq // FLASH_TILE
    return _bias_tiles(tab, [o * FLASH_TILE for o in range(n_off)], [0] * n_off, FLASH_TILE, FLASH_TILE,
                       seq, 1, 1)


def _block_onehot_t(nblk, seq):
    blk = seq // nblk
    return jnp.asarray((np.arange(seq)[:, None] // blk == np.arange(HEAD_DIM)[None, :]).astype(np.float32), BF16)


def _moba_attention(p, bias):
    b, _, s, _ = p.shape
    tq, nh = FLASH_TILE, FLASH_HEADS
    assert tq == MOBA_BLOCK
    nblk = s // tq
    nrow = max(nblk, 16)
    pool = jnp.asarray((np.arange(s)[None, :] // tq == np.arange(nrow)[:, None]).astype(np.float32) / tq, BF16)
    kv_spec = lambda base: pl.BlockSpec((1, nh, s, HEAD_DIM), lambda i, h, t: (i, base // nh + h, 0, 0))
    return pl.pallas_call(
        functools.partial(_moba_kernel, tq=tq, nblk=nrow, nh=nh),
        grid=(b, HEADS // nh, s // tq),
        in_specs=[pl.BlockSpec((1, nh, tq, HEAD_DIM), lambda i, h, t: (i, QC // nh + h, t, 0)),
                  kv_spec(KC), kv_spec(VC),
                  pl.BlockSpec((nrow, s), lambda i, h, t: (0, 0)),
                  pl.BlockSpec((s, HEAD_DIM), lambda i, h, t: (0, 0)),
                  pl.BlockSpec((nh, s // tq, tq, tq), lambda i, h, t: (h, 0, 0, 0))],
        out_specs=pl.BlockSpec((1, nh, tq, HEAD_DIM), lambda i, h, t: (i, h, t, 0)),
        out_shape=jax.ShapeDtypeStruct((b, HEADS, s, HEAD_DIM), F32),
        scratch_shapes=[pltpu.VMEM((nh, s, 2 * HEAD_DIM), BF16),
                        pltpu.VMEM((nh, nrow, HEAD_DIM), BF16)] + _flash_state(nh, s // tq, tq),
        compiler_params=_cparams(3),
        name="moba",
    )(p, p, p, pool, _block_onehot_t(nblk, s), bias)


def _nsa_selected_attention(p, pen, bias):
    b, _, s, _ = p.shape
    tq, nh = FLASH_TILE, FLASH_HEADS
    nblk = s // NSA_SLC_BLOCK
    assert nblk <= HEAD_DIM and GQA % nh == 0
    kv_spec = lambda base: pl.BlockSpec((1, 1, s, HEAD_DIM), lambda i, h, t: (i, base + (h * nh) // GQA, 0, 0))
    return pl.pallas_call(
        functools.partial(_slc_kernel, tq=tq, nh=nh),
        grid=(b, HEADS // nh, s // tq),
        in_specs=[pl.BlockSpec((1, nh, tq, HEAD_DIM), lambda i, h, t: (i, QD // nh + h, t, 0)),
                  kv_spec(KDS), kv_spec(VDS),
                  pl.BlockSpec((1, 1, tq, HEAD_DIM), lambda i, h, t: (i, (h * nh) // GQA, t, 0)),
                  pl.BlockSpec((s, HEAD_DIM), lambda i, h, t: (0, 0)),
                  pl.BlockSpec((nh, s // tq, tq, tq), lambda i, h, t: (h, 0, 0, 0))],
        out_specs=pl.BlockSpec((1, nh, tq, HEAD_DIM), lambda i, h, t: (i, h, t, 0)),
        out_shape=jax.ShapeDtypeStruct((b, HEADS, s, HEAD_DIM), F32),
        scratch_shapes=[pltpu.VMEM((1, s, 2 * HEAD_DIM), BF16)] + _flash_state(nh, s // tq, tq),
        compiler_params=_cparams(3),
        name="nsa_selected",
    )(p, p, p, pen, _block_onehot_t(nblk, s), bias)


def _compress_kernel(t_ref, pe_ref, w1_ref, w2_ref, o_ref):
    t = t_ref[0, 0].astype(F32)
    half = t.shape[-1]
    top = (t + pe_ref[0, 0:1, :]).astype(BF16)
    bot = (t + pe_ref[0, 1:2, :]).astype(BF16)
    u = jnp.dot(top, w1_ref[0, :half, :], preferred_element_type=F32)
    vv = jnp.dot(bot, w1_ref[0, half:, :], preferred_element_type=F32)
    pre = u + pltpu.roll(vv, vv.shape[0] - 1, axis=0)
    hid = jax.nn.gelu(pre).astype(BF16)
    o_ref[0, 0] = jnp.dot(hid, w2_ref[0], preferred_element_type=F32).astype(BF16)


def _nsa_compress(p, pe, w1, w2):
    b, _, s, _ = p.shape
    n_half = s // NSA_CMP_STRIDE
    half = NSA_CMP_STRIDE * HEAD_DIM
    t = p[:, KDC:KDC + 2 * KV_HEADS].reshape(b, 2 * KV_HEADS, n_half, half)
    pe2 = pe.reshape(2, 2, half)
    return pl.pallas_call(
        _compress_kernel,
        grid=(b, 2 * KV_HEADS),
        in_specs=[pl.BlockSpec((1, 1, n_half, half), lambda i, j: (i, j, 0, 0)),
                  pl.BlockSpec((1, 2, half), lambda i, j: (j // KV_HEADS, 0, 0)),
                  pl.BlockSpec((1, 2 * half, NSA_CMP_HIDDEN), lambda i, j: (j // KV_HEADS, 0, 0)),
                  pl.BlockSpec((1, NSA_CMP_HIDDEN, HEAD_DIM), lambda i, j: (j // KV_HEADS, 0, 0))],
        out_specs=pl.BlockSpec((1, 1, n_half, HEAD_DIM), lambda i, j: (i, j, 0, 0)),
        out_shape=jax.ShapeDtypeStruct((b, 2 * KV_HEADS, n_half, HEAD_DIM), BF16),
        compiler_params=_cparams(2),
        name="nsa_compress",
    )(t, pe2, w1.astype(BF16), w2.astype(BF16))


def _cmp_attn_kernel(q_ref, kc_ref, vc_ref, ov_ref, o_ref, pen_ref, *, tq, n_slc):
    qi = pl.program_id(2)
    rows = GQA * tq
    q = q_ref[0].reshape(rows, HEAD_DIM)
    kc = kc_ref[0, 0]
    n_cmp = kc.shape[0]
    s = lax.dot_general(q, kc, (((1,), (1,)), ((), ())), preferred_element_type=F32) * SCALE
    row = lax.broadcasted_iota(jnp.int32, (rows, n_cmp), 0)
    pos = qi * tq + (row & (tq - 1))
    blk_end = lax.broadcasted_iota(jnp.int32, (rows, n_cmp), 1) * NSA_CMP_STRIDE + (NSA_CMP_BLOCK - 1)
    mask = blk_end <= pos
    s = jnp.where(mask, s, NEG_INF)
    m = jnp.max(s, axis=-1, keepdims=True)
    e = jnp.where(mask, jnp.exp(s - m), 0.0)
    den = jnp.maximum(jnp.sum(e, axis=-1, keepdims=True), TINY)
    pr = e / den
    o_ref[0] = jnp.dot(pr.astype(BF16), vc_ref[0, 0], preferred_element_type=F32).reshape(GQA, tq, HEAD_DIM)
    psum = pr[0:tq]
    for g in range(1, GQA):
        psum = psum + pr[g * tq:(g + 1) * tq]
    ov_t = ov_ref[...]
    hi = psum.astype(BF16)
    r1 = psum - hi.astype(F32)
    mid = r1.astype(BF16)
    lo = (r1 - mid.astype(F32)).astype(BF16)
    nt = (((1,), (1,)), ((), ()))
    imp_t = (lax.dot_general(ov_t, hi, nt, preferred_element_type=F32)
             + lax.dot_general(ov_t, mid, nt, preferred_element_type=F32)
             + lax.dot_general(ov_t, lo, nt, preferred_element_type=F32))
    jj = lax.broadcasted_iota(jnp.int32, (n_slc, tq), 0)
    cur = (qi * tq + lax.broadcasted_iota(jnp.int32, (n_slc, tq), 1)) // NSA_SLC_BLOCK
    forced = (jj == 0) | (jj == cur) | (jj == cur - 1)
    imp_t = jnp.where(forced, -NEG_INF, jnp.where(jj <= cur, imp_t, NEG_INF))
    picked = (_ranks_along_rows(imp_t) < NSA_SLC_TOPN) & (jj <= cur)
    pen_ref[0, 0] = _rows_to_lanes(jnp.where(picked, 0.0, NEG_INF), HEAD_DIM).astype(BF16)


def _nsa_compressed_attention(p, kvc):
    b, _, s, _ = p.shape
    tq = 128
    n_cmp = kvc.shape[2]
    n_slc = s // NSA_SLC_BLOCK
    assert n_slc == HEAD_DIM
    c_start = np.arange(n_cmp) * NSA_CMP_STRIDE
    j_start = np.arange(n_slc) * NSA_SLC_BLOCK
    overlap = ((c_start[:, None] < j_start[None, :] + NSA_SLC_BLOCK)
               & (c_start[:, None] + NSA_CMP_BLOCK > j_start[None, :])).astype(np.float32)
    overlap[n_cmp - 1] = 0.0
    return pl.pallas_call(
        functools.partial(_cmp_attn_kernel, tq=tq, n_slc=n_slc),
        grid=(b, KV_HEADS, s // tq),
        in_specs=[pl.BlockSpec((1, GQA, tq, HEAD_DIM), lambda i, h, t: (i, QD // GQA + h, t, 0)),
                  pl.BlockSpec((1, 1, n_cmp, HEAD_DIM), lambda i, h, t: (i, h, 0, 0)),
                  pl.BlockSpec((1, 1, n_cmp, HEAD_DIM), lambda i, h, t: (i, KV_HEADS + h, 0, 0)),
                  pl.BlockSpec((n_slc, n_cmp), lambda i, h, t: (0, 0))],
        out_specs=[pl.BlockSpec((1, GQA, tq, HEAD_DIM), lambda i, h, t: (i, h, t, 0)),
                   pl.BlockSpec((1, 1, tq, HEAD_DIM), lambda i, h, t: (i, h, t, 0))],
        out_shape=[jax.ShapeDtypeStruct((b, HEADS, s, HEAD_DIM), F32),
                   jax.ShapeDtypeStruct((b, KV_HEADS, s, HEAD_DIM), BF16)],
        compiler_params=_cparams(3),
        name="nsa_compressed",
    )(p, kvc, kvc, jnp.asarray(overlap.T, BF16))


def _heads_to_lanes(ref, width=HEAD_DIM, offset=0):
    return jnp.concatenate([ref[0, h][:, offset:offset + width] for h in range(HEADS)], axis=-1)


def _group_norm(o, g):
    ms = jnp.mean(o * o, axis=-1, keepdims=True)
    return (o * lax.rsqrt(ms + NORM_EPS)) * g


def _route(logits_t, bias_t):
    aff = jax.nn.sigmoid(logits_t)
    score = aff + bias_t
    rows = [score[e:e + 1] for e in range(N_EXPERTS)]
    affs = [aff[e:e + 1] for e in range(N_EXPERTS)]
    gscore = []
    for g in range(N_EXPERT_GROUPS):
        mem = rows[g * PER_GROUP:(g + 1) * PER_GROUP]
        best = None
        for a in range(PER_GROUP):
            for c in range(a + 1, PER_GROUP):
                pair = mem[a] + mem[c]
                best = pair if best is None else jnp.maximum(best, pair)
        gscore.append(best)
    gsel = jnp.zeros_like(gscore[0], dtype=jnp.int32)
    gbest = gscore[0]
    for g in range(1, N_EXPERT_GROUPS):
        better = gscore[g] > gbest
        gsel = jnp.where(better, g, gsel)
        gbest = jnp.where(better, gscore[g], gbest)
    masked = [jnp.where(gsel == e // PER_GROUP, rows[e], NEG_INF) for e in range(N_EXPERTS)]

    def argbest(vals):
        idx = jnp.zeros_like(gsel)
        best = vals[0]
        for e in range(1, N_EXPERTS):
            better = vals[e] > best
            idx = jnp.where(better, e, idx)
            best = jnp.where(better, vals[e], best)
        return idx

    e1 = argbest(masked)
    e2 = argbest([jnp.where(e1 == e, -jnp.inf, masked[e]) for e in range(N_EXPERTS)])
    w1 = sum(jnp.where(e1 == e, affs[e], 0.0) for e in range(N_EXPERTS))
    w2 = sum(jnp.where(e2 == e, affs[e], 0.0) for e in range(N_EXPERTS))
    tot = w1 + w2
    w1 = w1 / tot
    w2 = w2 / tot
    comb = jnp.concatenate([jnp.where(e1 == e, w1, 0.0) + jnp.where(e2 == e, w2, 0.0)
                            for e in range(N_EXPERTS)], axis=0)
    return comb, gsel


def _out_kernel(x_ref, a0_ref, a1_ref, a2_ref, ob_ref, oc_ref, dc_ref, ds_ref, dw_ref, gate_ref,
                mixg_ref, wout_ref, g1_ref, n2g_ref, sc2_ref, sh2_ref, rw_ref, rb_ref,
                xo_ref, h2_ref, gsel_ref):
    outs = [_heads_to_lanes(r) for r in (a0_ref, a1_ref, a2_ref)]
    lses = [_heads_to_lanes(r, offset=HEAD_DIM) for r in (a0_ref, a1_ref, a2_ref)]
    mx = jnp.maximum(jnp.maximum(lses[0], lses[1]), lses[2])
    ws = [jnp.exp(l - mx) for l in lses]
    tot = ws[0] + ws[1] + ws[2]
    o_a = (ws[0] / tot) * outs[0] + (ws[1] / tot) * outs[1] + (ws[2] / tot) * outs[2]
    o_b = _heads_to_lanes(ob_ref)
    o_c = _heads_to_lanes(oc_ref)
    gates = gate_ref[0]
    tm = gates.shape[0]

    def gate_cols(i):
        return jnp.concatenate([jnp.broadcast_to(gates[:, h * 3 + i:h * 3 + i + 1], (tm, HEAD_DIM))
                                for h in range(HEADS)], axis=-1)

    o_d = (gate_cols(0) * _heads_to_lanes(dc_ref) + gate_cols(1) * _heads_to_lanes(ds_ref)
           + gate_cols(2) * _heads_to_lanes(dw_ref))
    mixed = jnp.concatenate([_group_norm(o, mixg_ref[i:i + 1, :]).astype(BF16)
                             for i, o in enumerate((o_a, o_b, o_c, o_d))], axis=-1)
    y = jnp.dot(mixed, wout_ref[...], preferred_element_type=F32)
    x = x_ref[0] + g1_ref[0] * y
    xo_ref[0] = x
    ms = jnp.mean(x * x, axis=-1, keepdims=True)
    h2 = (x * lax.rsqrt(ms + NORM_EPS)) * n2g_ref[...] * (1.0 + sc2_ref[0]) + sh2_ref[0]
    logits_t = lax.dot_general(rw_ref[...], h2.astype(BF16), (((1,), (1,)), ((), ())),
                               preferred_element_type=F32)
    comb_t, gsel = _route(logits_t, rb_ref[...])
    h2_ref[0] = jnp.concatenate([h2, _rows_to_lanes(comb_t, LANES)], axis=-1)
    gsel_ref[0] = gsel


def _mixer_out(x, a_outs, o_b, o_c, o_dc, o_ds, o_dw, gates, mix_g, w_out, g1, n2g, sc2, sh2, rw_t, rb):
    b, s, d = x.shape
    tm = min(256, s)
    head_spec = lambda width: pl.BlockSpec((1, HEADS, tm, width), lambda i, m: (i, 0, m, 0))
    vec_b = pl.BlockSpec((1, 1, d), lambda i, m: (i, 0, 0))
    return pl.pallas_call(
        _out_kernel,
        grid=(b, s // tm),
        in_specs=[pl.BlockSpec((1, tm, d), lambda i, m: (i, m, 0)),
                  head_spec(2 * HEAD_DIM), head_spec(2 * HEAD_DIM), head_spec(2 * HEAD_DIM),
                  head_spec(HEAD_DIM), head_spec(HEAD_DIM), head_spec(HEAD_DIM), head_spec(HEAD_DIM),
                  head_spec(HEAD_DIM),
                  pl.BlockSpec((1, tm, LANES), lambda i, m: (i, m, 0)),
                  pl.BlockSpec((N_MIXERS, GROUP_WIDTH), lambda i, m: (0, 0)),
                  pl.BlockSpec((N_MIXERS * GROUP_WIDTH, d), lambda i, m: (0, 0)),
                  vec_b,
                  pl.BlockSpec((1, d), lambda i, m: (0, 0)),
                  vec_b, vec_b,
                  pl.BlockSpec((N_EXPERTS, d), lambda i, m: (0, 0)),
                  pl.BlockSpec((N_EXPERTS, 1), lambda i, m: (0, 0))],
        out_specs=[pl.BlockSpec((1, tm, d), lambda i, m: (i, m, 0)),
                   pl.BlockSpec((1, tm, d + LANES), lambda i, m: (i, m, 0)),
                   pl.BlockSpec((1, 1, tm), lambda i, m: (i, 0, m))],
        out_shape=[jax.ShapeDtypeStruct((b, s, d), F32),
                   jax.ShapeDtypeStruct((b, s, d + LANES), F32),
                   jax.ShapeDtypeStruct((b, 1, s), jnp.int32)],
        compiler_params=_cparams(2),
        name="mixer_out",
    )(x, *a_outs, o_b, o_c, o_dc, o_ds, o_dw, gates, mix_g, w_out, g1.reshape(b, 1, d), n2g.reshape(1, d),
      sc2.reshape(b, 1, d), sh2.reshape(b, 1, d), rw_t, rb.reshape(N_EXPERTS, 1))


EXPERT_TILE = 512


def _dispatch_plan(gsel, tm):
    gid = gsel.reshape(-1)
    n_tok = gid.shape[0]
    onehot = (gid[:, None] == jnp.arange(N_EXPERT_GROUPS, dtype=jnp.int32)[None, :]).astype(jnp.int32)
    csum = jnp.cumsum(onehot, axis=0)
    rank = jnp.sum((csum - onehot) * onehot, axis=1)
    padded = ((csum[-1] + tm - 1) // tm) * tm
    ends = jnp.cumsum(padded)
    dest = jnp.sum(onehot * (ends - padded)[None, :], axis=1) + rank
    n_rows = n_tok + N_EXPERT_GROUPS * tm
    src = jnp.zeros((n_rows,), jnp.int32).at[dest].set(jnp.arange(n_tok, dtype=jnp.int32), unique_indices=True)
    tile_start = jnp.arange(n_rows // tm, dtype=jnp.int32) * tm
    tile_gid = jnp.minimum(jnp.sum((tile_start[:, None] >= ends[None, :]).astype(jnp.int32), axis=1),
                           N_EXPERT_GROUPS - 1)
    return dest.astype(jnp.int32), src, tile_gid, (ends[-1] // tm).reshape(1).astype(jnp.int32)


def _start_row_gather(idx_ref, base, src_hbm, buf, sem):
    def issue(r, c):
        pltpu.make_async_copy(src_hbm.at[pl.ds(idx_ref[base + r], 1)], buf.at[pl.ds(r, 1)], sem).start()
        return c

    lax.fori_loop(0, buf.shape[0], issue, 0, unroll=8)


def _wait_row_gather(src_hbm, buf, sem):
    pltpu.make_async_copy(src_hbm.at[pl.ds(0, buf.shape[0])], buf, sem).wait()


def _expert_kernel(src_ref, gid_ref, nvalid_ref, h_hbm, wgu_ref, wd_ref, o_ref, rows_buf, sems, h_scr, acc_ref, *, tm):
    t = pl.program_id(0)
    e = pl.program_id(1)
    slot = t % 2
    n_valid = nvalid_ref[0]

    @pl.when(t < n_valid)
    def _():
        @pl.when(e == 0)
        def _():
            @pl.when(t == 0)
            def _():
                _start_row_gather(src_ref, 0, h_hbm, rows_buf.at[0], sems.at[0])

            _wait_row_gather(h_hbm, rows_buf.at[slot], sems.at[slot])

            @pl.when(t + 1 < n_valid)
            def _():
                _start_row_gather(src_ref, (t + 1) * tm, h_hbm, rows_buf.at[1 - slot], sems.at[1 - slot])

            h_scr[...] = rows_buf[slot, :, :D_MODEL].astype(BF16)
            acc_ref[...] = jnp.zeros_like(acc_ref)

        gu = jnp.dot(h_scr[...], wgu_ref[0], preferred_element_type=F32)
        gate = gu[:, :D_EXPERT]
        he = (gate * jax.nn.sigmoid(gate)) * gu[:, D_EXPERT:]
        cw = rows_buf[slot, :, D_MODEL:]
        col = lax.broadcasted_iota(jnp.int32, cw.shape, 1)
        w = jnp.sum(jnp.where(col == gid_ref[t] * PER_GROUP + e, cw, 0.0), axis=-1, keepdims=True)
        acc_ref[...] += w * jnp.dot(he.astype(BF16), wd_ref[0], preferred_element_type=F32)

        @pl.when(e == PER_GROUP - 1)
        def _():
            o_ref[...] = acc_ref[...]

    @pl.when((t >= n_valid) & (e == PER_GROUP - 1))
    def _():
        o_ref[...] = jnp.zeros_like(o_ref)


def _experts(h2ext, src, tile_gid, n_valid, w_gu, w_d, tm):
    n_rows = src.shape[0]
    d = D_MODEL
    expert = lambda t, e, src_ref, gid, nv: (gid[t] * PER_GROUP + e, 0, 0)
    return pl.pallas_call(
        functools.partial(_expert_kernel, tm=tm),
        grid_spec=pltpu.PrefetchScalarGridSpec(
            num_scalar_prefetch=3,
            grid=(n_rows // tm, PER_GROUP),
            in_specs=[pl.BlockSpec(memory_space=pl.ANY),
                      pl.BlockSpec((1, d, 2 * D_EXPERT), expert),
                      pl.BlockSpec((1, D_EXPERT, d), expert)],
            out_specs=pl.BlockSpec((tm, d), lambda t, e, src_ref, gid, nv: (t, 0)),
            scratch_shapes=[pltpu.VMEM((2, tm, d + LANES), F32), pltpu.SemaphoreType.DMA((2,)),
                            pltpu.VMEM((tm, d), BF16), pltpu.VMEM((tm, d), F32)]),
        out_shape=jax.ShapeDtypeStruct((n_rows, d), F32),
        compiler_params=_cparams(2),
        name="experts",
    )(src, tile_gid, n_valid, h2ext, w_gu, w_d)


COMBINE_TILE = 512


def _combine_kernel(dest_ref, ys_hbm, x_ref, g2_ref, fg_ref, o_ref, rows_buf, sems, *, tm, final_norm):
    i = pl.program_id(0)
    slot = i % 2

    @pl.when(i == 0)
    def _():
        _start_row_gather(dest_ref, 0, ys_hbm, rows_buf.at[0], sems.at[0])

    _wait_row_gather(ys_hbm, rows_buf.at[slot], sems.at[slot])

    @pl.when(i + 1 < pl.num_programs(0))
    def _():
        _start_row_gather(dest_ref, (i + 1) * tm, ys_hbm, rows_buf.at[1 - slot], sems.at[1 - slot])

    x = x_ref[0] + g2_ref[0] * rows_buf[slot]
    if final_norm:
        ms = jnp.mean(x * x, axis=-1, keepdims=True)
        x = (x * lax.rsqrt(ms + NORM_EPS)) * fg_ref[...]
    o_ref[0] = x


def _combine(x, ys, dest, g2, final_g, final_norm):
    b, s, d = x.shape
    tm = min(COMBINE_TILE, s)
    per_seq = s // tm
    return pl.pallas_call(
        functools.partial(_combine_kernel, tm=tm, final_norm=final_norm),
        grid_spec=pltpu.PrefetchScalarGridSpec(
            num_scalar_prefetch=1,
            grid=(b * per_seq,),
            in_specs=[pl.BlockSpec(memory_space=pl.ANY),
                      pl.BlockSpec((1, tm, d), lambda i, dest_ref: (i // per_seq, i % per_seq, 0)),
                      pl.BlockSpec((1, 1, d), lambda i, dest_ref: (i // per_seq, 0, 0)),
                      pl.BlockSpec((1, d), lambda i, dest_ref: (0, 0))],
            out_specs=pl.BlockSpec((1, tm, d), lambda i, dest_ref: (i // per_seq, i % per_seq, 0)),
            scratch_shapes=[pltpu.VMEM((2, tm, d), F32), pltpu.SemaphoreType.DMA((2,))]),
        out_shape=jax.ShapeDtypeStruct((b, s, d), F32),
        compiler_params=_cparams(1),
        name="moe_combine",
    )(dest, ys, x, g2.reshape(b, 1, d), final_g.reshape(1, d))


def _moe(x, h2ext, gsel, w_gu, w_d, g2, final_g, final_norm):
    b, s, _ = x.shape
    dest, src, tile_gid, n_valid = _dispatch_plan(gsel, EXPERT_TILE)
    ys = _experts(h2ext.reshape(b * s, -1), src, tile_gid, n_valid, w_gu, w_d, EXPERT_TILE)
    return _combine(x, ys, dest, g2, final_g, final_norm)


def _bias_tables(rel_bias, s):
    tabs = [rel_bias[:, i * HEADS:(i + 1) * HEADS].T for i in range(N_MIXERS)]
    t = {}
    for window, r in DILATED_CONFIGS:
        t["a%d" % r] = _band_bias_tiles(tabs[0], window // r, 128, r, 1)
    t["b"] = _band_bias_tiles(tabs[1], SWA_WINDOW - 1, 128, 1, GQA)
    t["c"] = _flash_bias_tiles(tabs[2], s)
    t["ds"] = _flash_bias_tiles(tabs[3], s)
    t["dw"] = _band_bias_tiles(tabs[3], NSA_WINDOW - 1, NSA_WINDOW, 1, GQA)
    return t


def _token_mixer_branches(p, bias, sink, pe, w1, w2):
    b, _, s, _ = p.shape
    a_units = 4
    a_outs = []
    for window, r in DILATED_CONFIGS:
        if r == 1:
            src, lead, (q0, k0, v0) = p, b, (QA, KA, VA)
        else:
            src = p[:, QA:QA + 3 * HEADS].reshape(b, 3 * HEADS, s // r, r, HEAD_DIM)
            src = src.transpose(0, 3, 1, 2, 4).reshape(b * r, 3 * HEADS, s // r, HEAD_DIM)
            lead, (q0, k0, v0) = b * r, (0, HEADS, 2 * HEADS)
        o = _banded_attention(src, src, src, bias["a%d" % r], grid01=(lead, HEADS // a_units),
                              q_map=lambda i, j, q0=q0: (i, q0 // a_units + j),
                              kv_maps=(lambda i, j, k0=k0: (i, k0 // a_units + j),
                                       lambda i, j, v0=v0: (i, v0 // a_units + j)),
                              bias_map=lambda i, j: j, r_heads=1, units=a_units, w=128, out_heads=HEADS,
                              emit_lse=True, name="dilated_r%d" % r)
        if r > 1:
            o = o.reshape(b, r, HEADS, s // r, 2 * HEAD_DIM).transpose(0, 2, 3, 1, 4)
            o = o.reshape(b, HEADS, s, 2 * HEAD_DIM)
        a_outs.append(o)
    o_b = _banded_attention(p, p, p, bias["b"], grid01=(b, 1),
                            q_map=lambda i, j: (i, QB // HEADS), kv_maps=(lambda i, j: (i, KB // KV_HEADS),
                                                                         lambda i, j: (i, VB // KV_HEADS)),
                            bias_map=lambda i, j: 0, r_heads=GQA, units=KV_HEADS, w=128, out_heads=HEADS,
                            sink=sink, name="swa")
    o_c = _moba_attention(p, bias["c"])
    kvc = _nsa_compress(p, pe, w1, w2)
    o_dc, pen = _nsa_compressed_attention(p, kvc)
    o_ds = _nsa_selected_attention(p, pen, bias["ds"])
    o_dw = _banded_attention(p, p, p, bias["dw"], grid01=(b, KV_HEADS),
                             q_map=lambda i, j: (i, QD // GQA + j), kv_maps=(lambda i, j: (i, KDW + j),
                                                                            lambda i, j: (i, VDW + j)),
                             bias_map=lambda i, j: j, r_heads=GQA, units=1, w=NSA_WINDOW, out_heads=HEADS,
                             name="nsa_window")
    return a_outs, o_b, o_c, o_dc, o_ds, o_dw


def _split_w_in(w_in_l):
    w_heads = w_in_l[:, :PROJ_COLS].astype(BF16)
    w_gate = jnp.pad(w_in_l[:, PROJ_COLS:], ((0, 0), (0, LANES - N_GATE_COLS))).astype(BF16)
    return w_heads, w_gate


def _sink_rows(sink_l):
    return jnp.repeat(sink_l.astype(F32).reshape(KV_HEADS, GQA), BAND_TILE, axis=1)[..., None]


def kernel(x, c, rel_bias, router_w, router_bias, norm1_g, norm2_g, ada_w, ada_b, w_in, nsa_pe_k, nsa_pe_v,
           nsa_cmp_w1_k, nsa_cmp_w2_k, nsa_cmp_w1_v, nsa_cmp_w2_v, sinks, mix_norm_g, w_out, exp_w_gate,
           exp_w_up, exp_w_down, final_g):
    b, s, d = x.shape
    mod = _modulation(c, ada_w, ada_b)
    bias = _bias_tables(rel_bias, s)
    rw_t = router_w.T.astype(BF16)
    for l in range(DEPTH):
        sh1, sc1, g1, sh2, sc2, g2 = jnp.split(mod[l], 6, axis=-1)
        w_heads, w_gate = _split_w_in(w_in[l])
        p, gates = _in_projection(x, norm1_g[l], sc1, sh1, w_heads, w_gate)
        pe = jnp.stack([nsa_pe_k[l], nsa_pe_v[l]])
        w1 = jnp.stack([nsa_cmp_w1_k[l], nsa_cmp_w1_v[l]])
        w2 = jnp.stack([nsa_cmp_w2_k[l], nsa_cmp_w2_v[l]])
        a_outs, o_b, o_c, o_dc, o_ds, o_dw = _token_mixer_branches(p, bias, _sink_rows(sinks[l]), pe, w1, w2)
        x, h2ext, gsel = _mixer_out(x, a_outs, o_b, o_c, o_dc, o_ds, o_dw, gates, mix_norm_g[l],
                                    w_out[l].astype(BF16), g1, norm2_g[l], sc2, sh2, rw_t, router_bias)
        w_gu = jnp.concatenate([exp_w_gate[l], exp_w_up[l]], axis=-1).astype(BF16)
        x = _moe(x, h2ext, gsel, w_gu, exp_w_down[l].astype(BF16), g2, final_g, l == DEPTH - 1)
    return x
```

```python
import functools
import math

import numpy as np
import jax
import jax.numpy as jnp
from jax import lax
from jax.experimental import pallas as pl
from jax.experimental.pallas import tpu as pltpu

F32 = jnp.float32
BF16 = jnp.bfloat16

D_MODEL = 2048
DEPTH = 4
HEAD_DIM = 64
N_MIXERS = 4
HEADS = 8
GROUP_WIDTH = HEADS * HEAD_DIM
SCALE = HEAD_DIM ** -0.5
REL_BUCKETS = 32
REL_MAX_DIST = 2048
DILATED_CONFIGS = ((128, 1), (512, 4), (2048, 16))
SWA_WINDOW = 128
KV_HEADS = 2
GQA = HEADS // KV_HEADS
MOBA_BLOCK = 256
MOBA_TOPK = 3
NSA_CMP_BLOCK = 32
NSA_CMP_STRIDE = 16
NSA_CMP_HIDDEN = 128
NSA_SLC_BLOCK = 64
NSA_SLC_TOPN = 16
NSA_WINDOW = 512
N_EXPERTS = 16
N_EXPERT_GROUPS = 4
PER_GROUP = N_EXPERTS // N_EXPERT_GROUPS
D_EXPERT = 512
NORM_EPS = 1e-6
NEG_INF = -1e30
TINY = float(np.finfo(np.float32).tiny)

N_PROJ_HEADS = 80
PROJ_COLS = N_PROJ_HEADS * HEAD_DIM
N_GATE_COLS = HEADS * 3
LANES = 128
SUBLANES = 8
QA, KA, VA = 0, 8, 16
QB, KB, VB = 24, 32, 34
QC, KC, VC = 36, 44, 52
QD, KDC, VDC, KDS, VDS, KDW, VDW = 60, 68, 70, 72, 74, 76, 78

VMEM_LIMIT = 56 * 1024 * 1024


def _cparams(n_axes):
    return pltpu.CompilerParams(dimension_semantics=("arbitrary",) * n_axes,
                                vmem_limit_bytes=VMEM_LIMIT)


def _mod_kernel(c_ref, w_ref, b_ref, o_ref):
    c = c_ref[...]
    ca = (c * jax.nn.sigmoid(c)).astype(BF16)
    o_ref[0] = jnp.dot(ca, w_ref[0].astype(BF16), preferred_element_type=F32) + b_ref[0]


def _modulation(c, ada_w, ada_b):
    depth, d, n = ada_w.shape
    b = c.shape[0]
    bp = 8
    tn = 1024
    cp = jnp.pad(c, ((0, bp - b), (0, 0)))
    out = pl.pallas_call(
        _mod_kernel,
        grid=(depth, n // tn),
        in_specs=[pl.BlockSpec((bp, d), lambda l, j: (0, 0)),
                  pl.BlockSpec((1, d, tn), lambda l, j: (l, 0, j)),
                  pl.BlockSpec((1, 1, tn), lambda l, j: (l, 0, j))],
        out_specs=pl.BlockSpec((1, bp, tn), lambda l, j: (l, 0, j)),
        out_shape=jax.ShapeDtypeStruct((depth, bp, n), F32),
        compiler_params=_cparams(2),
        name="adaln_mod",
    )(cp, ada_w, ada_b.reshape(depth, 1, n))
    return out[:, :b]


def _inproj_kernel(x_ref, g_ref, sc_ref, sh_ref, w_ref, wg_ref, p_ref, gate_ref, h_scr):
    n = pl.program_id(2)

    @pl.when(n == 0)
    def _():
        x = x_ref[0]
        ms = jnp.mean(x * x, axis=-1, keepdims=True)
        h = (x * lax.rsqrt(ms + NORM_EPS)) * g_ref[...] * (1.0 + sc_ref[0]) + sh_ref[0]
        hb = h.astype(BF16)
        h_scr[...] = hb
        gate_ref[0] = jax.nn.sigmoid(jnp.dot(hb, wg_ref[...], preferred_element_type=F32))

    res = jnp.dot(h_scr[...], w_ref[...], preferred_element_type=F32)
    for j in range(HEADS):
        p_ref[0, j] = res[:, j * HEAD_DIM:(j + 1) * HEAD_DIM].astype(BF16)


def _in_projection(x, g, sc, sh, w_heads, w_gate):
    b, s, d = x.shape
    tm = min(1024, s)
    tn = HEADS * HEAD_DIM
    return pl.pallas_call(
        _inproj_kernel,
        grid=(b, s // tm, PROJ_COLS // tn),
        in_specs=[pl.BlockSpec((1, tm, d), lambda i, m, n: (i, m, 0)),
                  pl.BlockSpec((1, d), lambda i, m, n: (0, 0)),
                  pl.BlockSpec((1, 1, d), lambda i, m, n: (i, 0, 0)),
                  pl.BlockSpec((1, 1, d), lambda i, m, n: (i, 0, 0)),
                  pl.BlockSpec((d, tn), lambda i, m, n: (0, n)),
                  pl.BlockSpec((d, LANES), lambda i, m, n: (0, 0))],
        out_specs=[pl.BlockSpec((1, HEADS, tm, HEAD_DIM), lambda i, m, n: (i, n, m, 0)),
                   pl.BlockSpec((1, tm, LANES), lambda i, m, n: (i, m, 0))],
        out_shape=[jax.ShapeDtypeStruct((b, N_PROJ_HEADS, s, HEAD_DIM), BF16),
                   jax.ShapeDtypeStruct((b, s, LANES), F32)],
        scratch_shapes=[pltpu.VMEM((tm, d), BF16)],
        compiler_params=_cparams(3),
        name="in_proj",
    )(x, g.reshape(1, d), sc.reshape(b, 1, d), sh.reshape(b, 1, d), w_heads, w_gate)


def _bucket_thresholds():
    d = np.arange(REL_MAX_DIST + 1)
    max_exact = REL_BUCKETS // 2
    large = max_exact + (np.log(np.maximum(d, 1).astype(np.float32) / np.float32(max_exact))
                         / np.float32(math.log(REL_MAX_DIST / max_exact))
                         * np.float32(REL_BUCKETS - max_exact)).astype(np.int32)
    bucket = np.where(d < max_exact, d, np.minimum(large, REL_BUCKETS - 1))
    return [int(np.argmax(bucket >= b)) for b in range(REL_BUCKETS)]


def _bias_kernel(off_ref, cmin_ref, tab_ref, o_ref, *, max_dist, dist_scale, thresholds):
    h = pl.program_id(0)
    t = pl.program_id(1)
    shape = o_ref.shape[-2:]
    rel = off_ref[t] + lax.broadcasted_iota(jnp.int32, shape, 0) - lax.broadcasted_iota(jnp.int32, shape, 1)
    acc = jnp.full(shape, tab_ref[h, 0], F32)
    for b in range(1, REL_BUCKETS):
        acc = jnp.where(rel >= -(-thresholds[b] // dist_scale), tab_ref[h, b], acc)
    col = lax.broadcasted_iota(jnp.int32, shape, 1)
    valid = (rel >= 0) & (rel <= max_dist) & (col >= cmin_ref[t])
    o_ref[...] = jnp.where(valid, acc, NEG_INF).reshape(o_ref.shape)


def _bias_tiles(tab, offsets, col_mins, rows, cols, max_dist, dist_scale, group):
    n_heads = tab.shape[0]
    n_tiles = len(offsets)
    smem = pl.BlockSpec(memory_space=pltpu.SMEM)
    out = pl.pallas_call(
        functools.partial(_bias_kernel, max_dist=max_dist, dist_scale=dist_scale,
                          thresholds=_bucket_thresholds()),
        grid=(n_heads, n_tiles),
        in_specs=[smem, smem, smem],
        out_specs=pl.BlockSpec((1, 1, 1, rows, cols), lambda h, t: (h // group, t, h % group, 0, 0)),
        out_shape=jax.ShapeDtypeStruct((n_heads // group, n_tiles, group, rows, cols), F32),
        compiler_params=_cparams(2),
        name="bias_tiles",
    )(jnp.asarray(offsets, jnp.int32), jnp.asarray(col_mins, jnp.int32), tab.astype(F32))
    return out.reshape(n_heads // group, n_tiles, group * rows, cols)


BAND_TILE = 128
BAND_STEP = 512


def _gate_column(gates, idx):
    col = lax.broadcasted_iota(jnp.int32, gates.shape, 1)
    return jnp.sum(jnp.where(col == idx, gates, 0.0), axis=-1, keepdims=True)


def _banded_kernel(*refs, units, r_heads, nsub, w, has_sink, emit_lse, gate_branch):
    q_ref, kp_ref, kc_ref, vp_ref, vc_ref, b_ref = refs[:6]
    pos = 6
    sink_ref = gate_ref = None
    if has_sink:
        sink_ref = refs[pos]
        pos += 1
    if gate_branch is not None:
        gate_ref = refs[pos]
        pos += 1
    o_ref, kw_scr, vw_scr = refs[pos:pos + 3]
    t = pl.program_id(2)
    tq = BAND_TILE
    rows = r_heads * tq
    nk = w + tq
    kw_scr[:, :w] = kp_ref[0]
    kw_scr[:, w:] = kc_ref[0]
    vw_scr[:, :w] = vp_ref[0]
    vw_scr[:, w:] = vc_ref[0]
    for u in range(units):
        heads = slice(u * r_heads, (u + 1) * r_heads)
        for j in range(nsub):
            q = q_ref[0, heads, j * tq:(j + 1) * tq, :].reshape(rows, HEAD_DIM) * SCALE
            k = kw_scr[u, j * tq:j * tq + nk, :]
            v = vw_scr[u, j * tq:j * tq + nk, :]
            tile = jnp.minimum(t * nsub + j, w // tq)
            s = lax.dot_general(q, k, (((1,), (1,)), ((), ())), preferred_element_type=F32) + b_ref[u, tile]
            m = jnp.max(s, axis=-1, keepdims=True)
            if has_sink:
                m = jnp.maximum(m, sink_ref[u])
            e = jnp.exp(s - m)
            den = jnp.sum(e, axis=-1, keepdims=True)
            if has_sink:
                den = den + jnp.exp(sink_ref[u] - m)
            den = jnp.maximum(den, TINY)
            o = jnp.dot(e.astype(BF16), v, preferred_element_type=F32) / den
            if emit_lse:
                o = jnp.concatenate([o, jnp.broadcast_to(m + jnp.log(den), (rows, HEAD_DIM))], axis=-1)
                o_ref[0, heads, j * tq:(j + 1) * tq, :] = o.reshape(r_heads, tq, o.shape[-1])
            else:
                parts = []
                for g in range(r_heads):
                    og = o[g * tq:(g + 1) * tq]
                    if gate_ref is not None:
                        head = (pl.program_id(1) * units + u) * r_heads + g
                        og = og * _gate_column(gate_ref[0, j * tq:(j + 1) * tq, :], head * 3 + gate_branch)
                    parts.append(og)
                lanes = slice(u * r_heads * HEAD_DIM, (u + 1) * r_heads * HEAD_DIM)
                o_ref[0, j * tq:(j + 1) * tq, lanes] = jnp.concatenate(parts, axis=-1)


def _band_bias_tiles(tab, max_dist, w, dist_scale, group):
    tq = BAND_TILE
    n_early = w // tq
    return _bias_tiles(tab, [w] * (n_early + 1), [w - e * tq for e in range(n_early)] + [0], tq, w + tq,
                       max_dist, dist_scale, group)


def _banded_attention(q_arr, k_arr, v_arr, bias, *, grid01, q_map, kv_maps, bias_map, r_heads, units, w,
                      out_heads, sink=None, emit_lse=False, gates=None, gate_branch=None, name):
    seq = q_arr.shape[2]
    tq = BAND_TILE
    step = min(BAND_STEP, seq)
    nsub = step // tq
    assert seq % step == 0 and w % tq == 0 and step % w == 0
    k_map, v_map = kv_maps
    prev = lambda t: jnp.maximum(t * (step // w) - 1, 0)
    in_specs = [pl.BlockSpec((1, units * r_heads, step, HEAD_DIM), lambda i, j, t: q_map(i, j) + (t, 0)),
                pl.BlockSpec((1, units, w, HEAD_DIM), lambda i, j, t: k_map(i, j) + (prev(t), 0)),
                pl.BlockSpec((1, units, step, HEAD_DIM), lambda i, j, t: k_map(i, j) + (t, 0)),
                pl.BlockSpec((1, units, w, HEAD_DIM), lambda i, j, t: v_map(i, j) + (prev(t), 0)),
                pl.BlockSpec((1, units, step, HEAD_DIM), lambda i, j, t: v_map(i, j) + (t, 0)),
                pl.BlockSpec((units, w // tq + 1, r_heads * tq, w + tq), lambda i, j, t: (bias_map(i, j), 0, 0, 0))]
    args = [q_arr, k_arr, k_arr, v_arr, v_arr, bias]
    if sink is not None:
        in_specs.append(pl.BlockSpec((units, r_heads * tq, 1), lambda i, j, t: (bias_map(i, j), 0, 0)))
        args.append(sink)
    if gates is not None:
        in_specs.append(pl.BlockSpec((1, step, LANES), lambda i, j, t: (i, t, 0)))
        args.append(gates)
    n0, n1 = grid01
    if emit_lse:
        out_spec = pl.BlockSpec((1, units * r_heads, step, 2 * HEAD_DIM), lambda i, j, t: (i, j, t, 0))
        out_shape = jax.ShapeDtypeStruct((n0, out_heads, seq, 2 * HEAD_DIM), F32)
    else:
        out_spec = pl.BlockSpec((1, step, units * r_heads * HEAD_DIM), lambda i, j, t: (i, t, j))
        out_shape = jax.ShapeDtypeStruct((n0, seq, out_heads * HEAD_DIM), F32)
    return pl.pallas_call(
        functools.partial(_banded_kernel, units=units, r_heads=r_heads, nsub=nsub, w=w,
                          has_sink=sink is not None, emit_lse=emit_lse,
                          gate_branch=gate_branch if gates is not None else None),
        grid=(n0, n1, seq // step),
        in_specs=in_specs,
        out_specs=out_spec,
        out_shape=out_shape,
        scratch_shapes=[pltpu.VMEM((units, w + step, HEAD_DIM), BF16),
                        pltpu.VMEM((units, w + step, HEAD_DIM), BF16)],
        compiler_params=_cparams(3),
        name=name,
    )(*args)


def _ranks_along_rows(vals):
    n, t = vals.shape
    groups = [vals[g * SUBLANES:(g + 1) * SUBLANES] for g in range(n // SUBLANES)]
    row = lax.broadcasted_iota(jnp.int32, (SUBLANES, t), 0)
    later = [jnp.where(row > r, 1, 0) for r in range(SUBLANES)]
    cnt = [jnp.zeros((SUBLANES, t), jnp.int32) for _ in groups]
    for i in range(n):
        vi = vals[i:i + 1]
        for g, vg in enumerate(groups):
            if g > i // SUBLANES:
                cnt[g] = cnt[g] + jnp.where(vi >= vg, 1, 0)
            elif g < i // SUBLANES:
                cnt[g] = cnt[g] + jnp.where(vi > vg, 1, 0)
            else:
                cnt[g] = cnt[g] + jnp.where(vi > vg, 1, 0) + jnp.where(vi == vg, later[i % SUBLANES], 0)
    return jnp.concatenate(cnt, axis=0)


def _rows_to_lanes(x_t, width):
    n, t = x_t.shape
    padded = jnp.concatenate([x_t, jnp.zeros((LANES - n, t), x_t.dtype)], axis=0)
    return padded.T[:, :width]


def _tile_groups(n, fn):
    def body(j, c):
        fn(j * 4, 4)
        return c

    lax.fori_loop(0, n // 4, body, 0)
    base = (n // 4) * 4

    @pl.when((n & 2) != 0)
    def _():
        fn(base, 2)

    @pl.when((n & 1) != 0)
    def _():
        fn(base + (n & 2), 1)


def _flash_selected(qa_scr, qi, kaug_scr, kv_of, v_ref, b_ref, s_scr, m_scr, l_scr, acc_scr, tq):
    nh = qa_scr.shape[0]
    half = tq // 2
    n = qi + 1
    nt = (((1,), (1,)), ((), ()))
    m_scr[...] = jnp.full(m_scr.shape, NEG_INF, F32)

    def logits(kj0, cnt):
        ks = pl.multiple_of(kj0 * tq, tq)
        for h in range(nh):
            s = lax.dot_general(qa_scr[h], kaug_scr[kv_of[h], pl.ds(ks, cnt * tq), :], nt,
                                preferred_element_type=F32)
            mloc = None
            for u in range(cnt):
                su = s[:, u * tq:(u + 1) * tq] + b_ref[h, jnp.minimum(qi - kj0 - u, b_ref.shape[1] - 1)]
                s_scr[h, kj0 + u] = su
                mu = jnp.maximum(su[:, :half], su[:, half:])
                mloc = mu if mloc is None else jnp.maximum(mloc, mu)
            m_scr[h] = jnp.maximum(m_scr[h], mloc)

    _tile_groups(n, logits)
    for h in range(nh):
        m_scr[h] = jnp.broadcast_to(jnp.max(m_scr[h], axis=-1, keepdims=True), (tq, half))
    l_scr[...] = jnp.zeros(l_scr.shape, F32)
    acc_scr[...] = jnp.zeros(acc_scr.shape, F32)

    def weigh(kj0, cnt):
        ks = pl.multiple_of(kj0 * tq, tq)
        for h in range(nh):
            mb = m_scr[h]
            parts = []
            lloc = None
            for u in range(cnt):
                su = s_scr[h, kj0 + u]
                e_lo = jnp.exp(su[:, :half] - mb)
                e_hi = jnp.exp(su[:, half:] - mb)
                lu = e_lo + e_hi
                lloc = lu if lloc is None else lloc + lu
                parts += [e_lo.astype(BF16), e_hi.astype(BF16)]
            acc_scr[h] += jnp.dot(jnp.concatenate(parts, axis=-1), v_ref[0, kv_of[h], pl.ds(ks, cnt * tq), :],
                                  preferred_element_type=F32)
            l_scr[h] += lloc

    _tile_groups(n, weigh)
    return [acc_scr[h] / jnp.maximum(jnp.sum(l_scr[h], axis=-1, keepdims=True), TINY) for h in range(nh)]


def _fill_kaug(kaug_scr, k_ref, et_ref):
    for kv in range(kaug_scr.shape[0]):
        kaug_scr[kv, :, :HEAD_DIM] = k_ref[0, kv]
        kaug_scr[kv, :, HEAD_DIM:] = et_ref[...]


def _augment(q, pen):
    return jnp.concatenate([q * SCALE, pen.astype(BF16)], axis=-1)


def _moba_kernel(q_ref, k_ref, v_ref, pool_ref, et_ref, b_ref, o_ref, kaug_scr, kmean_scr, qa_scr, s_scr, m_scr,
                 l_scr, acc_scr, *, tq, nblk, nh):
    qi = pl.program_id(2)

    @pl.when(qi == 0)
    def _():
        _fill_kaug(kaug_scr, k_ref, et_ref)
        for h in range(nh):
            kmean_scr[h] = jnp.dot(pool_ref[...], k_ref[0, h], preferred_element_type=F32).astype(BF16)

    for h in range(nh):
        q = q_ref[0, h]
        gate = lax.dot_general(kmean_scr[h], q, (((1,), (1,)), ((), ())), preferred_element_type=F32)
        blk = lax.broadcasted_iota(jnp.int32, (nblk, tq), 0)
        gate = jnp.where(blk < qi, gate, NEG_INF)
        picked = ((_ranks_along_rows(gate) < MOBA_TOPK) & (blk < qi)) | (blk == qi)
        pen_t = jnp.where(picked, 0.0, NEG_INF)
        qa_scr[h] = _augment(q, _rows_to_lanes(pen_t, HEAD_DIM))
    outs = _flash_selected(qa_scr, qi, kaug_scr, tuple(range(nh)), v_ref, b_ref, s_scr, m_scr, l_scr, acc_scr, tq)
    o_ref[0] = jnp.concatenate(outs, axis=-1)


def _slc_kernel(q_ref, k_ref, v_ref, pen_ref, gate_ref, et_ref, b_ref, o_ref, kaug_scr, qa_scr, s_scr, m_scr,
                l_scr, acc_scr, *, tq, nh):
    qi = pl.program_id(2)

    @pl.when(qi == 0)
    def _():
        _fill_kaug(kaug_scr, k_ref, et_ref)

    for h in range(nh):
        qa_scr[h] = _augment(q_ref[0, h], pen_ref[0, 0])
    outs = _flash_selected(qa_scr, qi, kaug_scr, (0,) * nh, v_ref, b_ref, s_scr, m_scr, l_scr, acc_scr, tq)
    gates = gate_ref[0]
    first = pl.program_id(1) * nh
    o_ref[0] = jnp.concatenate([outs[h] * _gate_column(gates, (first + h) * 3 + 1) for h in range(nh)], axis=-1)


def _flash_state(nh, n_tiles, tq):
    return [pltpu.VMEM((nh, tq, 2 * HEAD_DIM), BF16),
            pltpu.VMEM((nh, n_tiles, tq, tq), F32),
            pltpu.VMEM((nh, tq, tq // 2), F32),
            pltpu.VMEM((nh, tq, tq // 2), F32),
            pltpu.VMEM((nh, tq, HEAD_DIM), F32)]


FLASH_TILE = 256
MOBA_HEADS_PER_STEP = 2
SLC_HEADS_PER_STEP = GQA


def _flash_bias_tiles(tab, seq):
    tq = FLASH_TILE
    first_const = -(-(_bucket_thresholds()[-1] + tq - 1) // tq)
    n_off = min(seq // tq, first_const + 1)
    return _bias_tiles(tab, [o * tq for o in range(n_off)], [0] * n_off, tq, tq, seq, 1, 1)


def _block_onehot_t(nblk, seq):
    blk = seq // nblk
    return jnp.asarray((np.arange(seq)[:, None] // blk == np.arange(HEAD_DIM)[None, :]).astype(np.float32), BF16)


def _moba_attention(p, bias):
    b, _, s, _ = p.shape
    tq, nh = FLASH_TILE, MOBA_HEADS_PER_STEP
    assert tq == MOBA_BLOCK
    nblk = s // tq
    nrow = max(nblk, 16)
    n_bias = bias.shape[1]
    pool = jnp.asarray((np.arange(s)[None, :] // tq == np.arange(nrow)[:, None]).astype(np.float32) / tq, BF16)
    kv_spec = lambda base: pl.BlockSpec((1, nh, s, HEAD_DIM), lambda i, h, t: (i, base // nh + h, 0, 0))
    return pl.pallas_call(
        functools.partial(_moba_kernel, tq=tq, nblk=nrow, nh=nh),
        grid=(b, HEADS // nh, s // tq),
        in_specs=[pl.BlockSpec((1, nh, tq, HEAD_DIM), lambda i, h, t: (i, QC // nh + h, t, 0)),
                  kv_spec(KC), kv_spec(VC),
                  pl.BlockSpec((nrow, s), lambda i, h, t: (0, 0)),
                  pl.BlockSpec((s, HEAD_DIM), lambda i, h, t: (0, 0)),
                  pl.BlockSpec((nh, n_bias, tq, tq), lambda i, h, t: (h, 0, 0, 0))],
        out_specs=pl.BlockSpec((1, tq, nh * HEAD_DIM), lambda i, h, t: (i, t, h)),
        out_shape=jax.ShapeDtypeStruct((b, s, HEADS * HEAD_DIM), F32),
        scratch_shapes=[pltpu.VMEM((nh, s, 2 * HEAD_DIM), BF16),
                        pltpu.VMEM((nh, nrow, HEAD_DIM), BF16)] + _flash_state(nh, s // tq, tq),
        compiler_params=_cparams(3),
        name="moba",
    )(p, p, p, pool, _block_onehot_t(nblk, s), bias)


def _nsa_selected_attention(p, pen, gates, bias):
    b, _, s, _ = p.shape
    tq, nh = FLASH_TILE, SLC_HEADS_PER_STEP
    nblk = s // NSA_SLC_BLOCK
    n_bias = bias.shape[1]
    assert nblk <= HEAD_DIM and GQA % nh == 0
    kv_spec = lambda base: pl.BlockSpec((1, 1, s, HEAD_DIM), lambda i, h, t: (i, base + (h * nh) // GQA, 0, 0))
    return pl.pallas_call(
        functools.partial(_slc_kernel, tq=tq, nh=nh),
        grid=(b, HEADS // nh, s // tq),
        in_specs=[pl.BlockSpec((1, nh, tq, HEAD_DIM), lambda i, h, t: (i, QD // nh + h, t, 0)),
                  kv_spec(KDS), kv_spec(VDS),
                  pl.BlockSpec((1, 1, tq, HEAD_DIM), lambda i, h, t: (i, (h * nh) // GQA, t, 0)),
                  pl.BlockSpec((1, tq, LANES), lambda i, h, t: (i, t, 0)),
                  pl.BlockSpec((s, HEAD_DIM), lambda i, h, t: (0, 0)),
                  pl.BlockSpec((nh, n_bias, tq, tq), lambda i, h, t: (h, 0, 0, 0))],
        out_specs=pl.BlockSpec((1, tq, nh * HEAD_DIM), lambda i, h, t: (i, t, h)),
        out_shape=jax.ShapeDtypeStruct((b, s, HEADS * HEAD_DIM), F32),
        scratch_shapes=[pltpu.VMEM((1, s, 2 * HEAD_DIM), BF16)] + _flash_state(nh, s // tq, tq),
        compiler_params=_cparams(3),
        name="nsa_selected",
    )(p, p, p, pen, gates, _block_onehot_t(nblk, s), bias)


def _compress_kernel(t_ref, pe_ref, w1_ref, w2_ref, o_ref):
    t = t_ref[0, 0].astype(F32)
    half = t.shape[-1]
    top = (t + pe_ref[0, 0:1, :]).astype(BF16)
    bot = (t + pe_ref[0, 1:2, :]).astype(BF16)
    u = jnp.dot(top, w1_ref[0, :half, :], preferred_element_type=F32)
    vv = jnp.dot(bot, w1_ref[0, half:, :], preferred_element_type=F32)
    pre = u + pltpu.roll(vv, vv.shape[0] - 1, axis=0)
    hid = jax.nn.gelu(pre).astype(BF16)
    o_ref[0, 0] = jnp.dot(hid, w2_ref[0], preferred_element_type=F32).astype(BF16)


def _nsa_compress(p, pe, w1, w2):
    b, _, s, _ = p.shape
    n_half = s // NSA_CMP_STRIDE
    half = NSA_CMP_STRIDE * HEAD_DIM
    t = p[:, KDC:KDC + 2 * KV_HEADS].reshape(b, 2 * KV_HEADS, n_half, half)
    pe2 = pe.reshape(2, 2, half)
    return pl.pallas_call(
        _compress_kernel,
        grid=(b, 2 * KV_HEADS),
        in_specs=[pl.BlockSpec((1, 1, n_half, half), lambda i, j: (i, j, 0, 0)),
                  pl.BlockSpec((1, 2, half), lambda i, j: (j // KV_HEADS, 0, 0)),
                  pl.BlockSpec((1, 2 * half, NSA_CMP_HIDDEN), lambda i, j: (j // KV_HEADS, 0, 0)),
                  pl.BlockSpec((1, NSA_CMP_HIDDEN, HEAD_DIM), lambda i, j: (j // KV_HEADS, 0, 0))],
        out_specs=pl.BlockSpec((1, 1, n_half, HEAD_DIM), lambda i, j: (i, j, 0, 0)),
        out_shape=jax.ShapeDtypeStruct((b, 2 * KV_HEADS, n_half, HEAD_DIM), BF16),
        compiler_params=_cparams(2),
        name="nsa_compress",
    )(t, pe2, w1.astype(BF16), w2.astype(BF16))


def _cmp_attn_kernel(q_ref, kc_ref, vc_ref, ov_ref, gate_ref, o_ref, pen_ref, *, tq, n_slc):
    qi = pl.program_id(2)
    rows = GQA * tq
    q = q_ref[0].reshape(rows, HEAD_DIM)
    kc = kc_ref[0, 0]
    n_cmp = kc.shape[0]
    s = lax.dot_general(q, kc, (((1,), (1,)), ((), ())), preferred_element_type=F32) * SCALE
    row = lax.broadcasted_iota(jnp.int32, (rows, n_cmp), 0)
    pos = qi * tq + (row & (tq - 1))
    blk_end = lax.broadcasted_iota(jnp.int32, (rows, n_cmp), 1) * NSA_CMP_STRIDE + (NSA_CMP_BLOCK - 1)
    mask = blk_end <= pos
    s = jnp.where(mask, s, NEG_INF)
    m = jnp.max(s, axis=-1, keepdims=True)
    e = jnp.where(mask, jnp.exp(s - m), 0.0)
    den = jnp.maximum(jnp.sum(e, axis=-1, keepdims=True), TINY)
    pr = e / den
    o = jnp.dot(pr.astype(BF16), vc_ref[0, 0], preferred_element_type=F32)
    gates = gate_ref[0]
    first = pl.program_id(1) * GQA
    o_ref[0] = jnp.concatenate([o[g * tq:(g + 1) * tq] * _gate_column(gates, (first + g) * 3)
                                for g in range(GQA)], axis=-1)
    psum = pr[0:tq]
    for g in range(1, GQA):
        psum = psum + pr[g * tq:(g + 1) * tq]
    ov_t = ov_ref[...]
    hi = psum.astype(BF16)
    r1 = psum - hi.astype(F32)
    mid = r1.astype(BF16)
    lo = (r1 - mid.astype(F32)).astype(BF16)
    nt = (((1,), (1,)), ((), ()))
    imp_t = (lax.dot_general(ov_t, hi, nt, preferred_element_type=F32)
             + lax.dot_general(ov_t, mid, nt, preferred_element_type=F32)
             + lax.dot_general(ov_t, lo, nt, preferred_element_type=F32))
    jj = lax.broadcasted_iota(jnp.int32, (n_slc, tq), 0)
    cur = (qi * tq + lax.broadcasted_iota(jnp.int32, (n_slc, tq), 1)) // NSA_SLC_BLOCK
    forced = (jj == 0) | (jj == cur) | (jj == cur - 1)
    imp_t = jnp.where(forced, -NEG_INF, jnp.where(jj <= cur, imp_t, NEG_INF))
    picked = (_ranks_along_rows(imp_t) < NSA_SLC_TOPN) & (jj <= cur)
    pen_ref[0, 0] = _rows_to_lanes(jnp.where(picked, 0.0, NEG_INF), HEAD_DIM).astype(BF16)


def _nsa_compressed_attention(p, kvc, gates):
    b, _, s, _ = p.shape
    tq = 128
    n_cmp = kvc.shape[2]
    n_slc = s // NSA_SLC_BLOCK
    assert n_slc == HEAD_DIM
    c_start = np.arange(n_cmp) * NSA_CMP_STRIDE
    j_start = np.arange(n_slc) * NSA_SLC_BLOCK
    overlap = ((c_start[:, None] < j_start[None, :] + NSA_SLC_BLOCK)
               & (c_start[:, None] + NSA_CMP_BLOCK > j_start[None, :])).astype(np.float32)
    overlap[n_cmp - 1] = 0.0
    return pl.pallas_call(
        functools.partial(_cmp_attn_kernel, tq=tq, n_slc=n_slc),
        grid=(b, KV_HEADS, s // tq),
        in_specs=[pl.BlockSpec((1, GQA, tq, HEAD_DIM), lambda i, h, t: (i, QD // GQA + h, t, 0)),
                  pl.BlockSpec((1, 1, n_cmp, HEAD_DIM), lambda i, h, t: (i, h, 0, 0)),
                  pl.BlockSpec((1, 1, n_cmp, HEAD_DIM), lambda i, h, t: (i, KV_HEADS + h, 0, 0)),
                  pl.BlockSpec((n_slc, n_cmp), lambda i, h, t: (0, 0)),
                  pl.BlockSpec((1, tq, LANES), lambda i, h, t: (i, t, 0))],
        out_specs=[pl.BlockSpec((1, tq, GQA * HEAD_DIM), lambda i, h, t: (i, t, h)),
                   pl.BlockSpec((1, 1, tq, HEAD_DIM), lambda i, h, t: (i, h, t, 0))],
        out_shape=[jax.ShapeDtypeStruct((b, s, HEADS * HEAD_DIM), F32),
                   jax.ShapeDtypeStruct((b, KV_HEADS, s, HEAD_DIM), BF16)],
        compiler_params=_cparams(3),
        name="nsa_compressed",
    )(p, kvc, kvc, jnp.asarray(overlap.T, BF16), gates)


def _heads_to_lanes(ref, width=HEAD_DIM, offset=0):
    return jnp.concatenate([ref[0, h][:, offset:offset + width] for h in range(HEADS)], axis=-1)


def _group_norm(o, g):
    ms = jnp.mean(o * o, axis=-1, keepdims=True)
    return (o * lax.rsqrt(ms + NORM_EPS)) * g


def _route(logits_t, bias_t):
    aff = jax.nn.sigmoid(logits_t)
    score = aff + bias_t
    rows = [score[e:e + 1] for e in range(N_EXPERTS)]
    affs = [aff[e:e + 1] for e in range(N_EXPERTS)]
    gscore = []
    for g in range(N_EXPERT_GROUPS):
        mem = rows[g * PER_GROUP:(g + 1) * PER_GROUP]
        best = None
        for a in range(PER_GROUP):
            for c in range(a + 1, PER_GROUP):
                pair = mem[a] + mem[c]
                best = pair if best is None else jnp.maximum(best, pair)
        gscore.append(best)
    gsel = jnp.zeros_like(gscore[0], dtype=jnp.int32)
    gbest = gscore[0]
    for g in range(1, N_EXPERT_GROUPS):
        better = gscore[g] > gbest
        gsel = jnp.where(better, g, gsel)
        gbest = jnp.where(better, gscore[g], gbest)
    masked = [jnp.where(gsel == e // PER_GROUP, rows[e], NEG_INF) for e in range(N_EXPERTS)]

    def argbest(vals):
        idx = jnp.zeros_like(gsel)
        best = vals[0]
        for e in range(1, N_EXPERTS):
            better = vals[e] > best
            idx = jnp.where(better, e, idx)
            best = jnp.where(better, vals[e], best)
        return idx

    e1 = argbest(masked)
    e2 = argbest([jnp.where(e1 == e, -jnp.inf, masked[e]) for e in range(N_EXPERTS)])
    w1 = sum(jnp.where(e1 == e, affs[e], 0.0) for e in range(N_EXPERTS))
    w2 = sum(jnp.where(e2 == e, affs[e], 0.0) for e in range(N_EXPERTS))
    tot = w1 + w2
    w1 = w1 / tot
    w2 = w2 / tot
    comb = jnp.concatenate([jnp.where(e1 == e, w1, 0.0) + jnp.where(e2 == e, w2, 0.0)
                            for e in range(N_EXPERTS)], axis=0)
    return comb, gsel


def _out_kernel(x_ref, a0_ref, a1_ref, a2_ref, ob_ref, oc_ref, dc_ref, ds_ref, dw_ref,
                mixg_ref, wout_ref, g1_ref, n2g_ref, sc2_ref, sh2_ref, rw_ref, rb_ref,
                xo_ref, h2_ref, gsel_ref):
    outs = [_heads_to_lanes(r) for r in (a0_ref, a1_ref, a2_ref)]
    lses = [_heads_to_lanes(r, offset=HEAD_DIM) for r in (a0_ref, a1_ref, a2_ref)]
    mx = jnp.maximum(jnp.maximum(lses[0], lses[1]), lses[2])
    ws = [jnp.exp(l - mx) for l in lses]
    tot = ws[0] + ws[1] + ws[2]
    o_a = (ws[0] / tot) * outs[0] + (ws[1] / tot) * outs[1] + (ws[2] / tot) * outs[2]
    o_b = ob_ref[0]
    o_c = oc_ref[0]
    o_d = dc_ref[0] + ds_ref[0] + dw_ref[0]
    mixed = jnp.concatenate([_group_norm(o, mixg_ref[i:i + 1, :]).astype(BF16)
                             for i, o in enumerate((o_a, o_b, o_c, o_d))], axis=-1)
    y = jnp.dot(mixed, wout_ref[...], preferred_element_type=F32)
    x = x_ref[0] + g1_ref[0] * y
    xo_ref[0] = x
    ms = jnp.mean(x * x, axis=-1, keepdims=True)
    h2 = (x * lax.rsqrt(ms + NORM_EPS)) * n2g_ref[...] * (1.0 + sc2_ref[0]) + sh2_ref[0]
    logits_t = lax.dot_general(rw_ref[...], h2.astype(BF16), (((1,), (1,)), ((), ())),
                               preferred_element_type=F32)
    comb_t, gsel = _route(logits_t, rb_ref[...])
    h2_ref[0] = jnp.concatenate([h2, _rows_to_lanes(comb_t, LANES)], axis=-1)
    gsel_ref[0] = gsel


def _mixer_out(x, a_outs, o_b, o_c, o_dc, o_ds, o_dw, mix_g, w_out, g1, n2g, sc2, sh2, rw_t, rb):
    b, s, d = x.shape
    tm = min(256, s)
    head_spec = pl.BlockSpec((1, HEADS, tm, 2 * HEAD_DIM), lambda i, m: (i, 0, m, 0))
    mixer_spec = pl.BlockSpec((1, tm, GROUP_WIDTH), lambda i, m: (i, m, 0))
    vec_b = pl.BlockSpec((1, 1, d), lambda i, m: (i, 0, 0))
    return pl.pallas_call(
        _out_kernel,
        grid=(b, s // tm),
        in_specs=[pl.BlockSpec((1, tm, d), lambda i, m: (i, m, 0)),
                  head_spec, head_spec, head_spec,
                  mixer_spec, mixer_spec, mixer_spec, mixer_spec, mixer_spec,
                  pl.BlockSpec((N_MIXERS, GROUP_WIDTH), lambda i, m: (0, 0)),
                  pl.BlockSpec((N_MIXERS * GROUP_WIDTH, d), lambda i, m: (0, 0)),
                  vec_b,
                  pl.BlockSpec((1, d), lambda i, m: (0, 0)),
                  vec_b, vec_b,
                  pl.BlockSpec((N_EXPERTS, d), lambda i, m: (0, 0)),
                  pl.BlockSpec((N_EXPERTS, 1), lambda i, m: (0, 0))],
        out_specs=[pl.BlockSpec((1, tm, d), lambda i, m: (i, m, 0)),
                   pl.BlockSpec((1, tm, d + LANES), lambda i, m: (i, m, 0)),
                   pl.BlockSpec((1, 1, tm), lambda i, m: (i, 0, m))],
        out_shape=[jax.ShapeDtypeStruct((b, s, d), F32),
                   jax.ShapeDtypeStruct((b, s, d + LANES), F32),
                   jax.ShapeDtypeStruct((b, 1, s), jnp.int32)],
        compiler_params=_cparams(2),
        name="mixer_out",
    )(x, *a_outs, o_b, o_c, o_dc, o_ds, o_dw, mix_g, w_out, g1.reshape(b, 1, d), n2g.reshape(1, d),
      sc2.reshape(b, 1, d), sh2.reshape(b, 1, d), rw_t, rb.reshape(N_EXPERTS, 1))


EXPERT_TILE = 512


def _dispatch_plan(gsel, tm):
    gid = gsel.reshape(-1)
    n_tok = gid.shape[0]
    onehot = (gid[:, None] == jnp.arange(N_EXPERT_GROUPS, dtype=jnp.int32)[None, :]).astype(jnp.int32)
    csum = jnp.cumsum(onehot, axis=0)
    rank = jnp.sum((csum - onehot) * onehot, axis=1)
    padded = ((csum[-1] + tm - 1) // tm) * tm
    ends = jnp.cumsum(padded)
    dest = jnp.sum(onehot * (ends - padded)[None, :], axis=1) + rank
    n_rows = n_tok + N_EXPERT_GROUPS * tm
    src = jnp.zeros((n_rows,), jnp.int32).at[dest].set(jnp.arange(n_tok, dtype=jnp.int32), unique_indices=True)
    tile_start = jnp.arange(n_rows // tm, dtype=jnp.int32) * tm
    tile_gid = jnp.minimum(jnp.sum((tile_start[:, None] >= ends[None, :]).astype(jnp.int32), axis=1),
                           N_EXPERT_GROUPS - 1)
    return dest.astype(jnp.int32), src, tile_gid, (ends[-1] // tm).reshape(1).astype(jnp.int32)


def _start_row_gather(idx_ref, base, src_hbm, buf, sem):
    def issue(r, c):
        pltpu.make_async_copy(src_hbm.at[pl.ds(idx_ref[base + r], 1)], buf.at[pl.ds(r, 1)], sem).start()
        return c

    lax.fori_loop(0, buf.shape[0], issue, 0, unroll=8)


def _wait_row_gather(src_hbm, buf, sem):
    pltpu.make_async_copy(src_hbm.at[pl.ds(0, buf.shape[0])], buf, sem).wait()


def _expert_kernel(src_ref, gid_ref, nvalid_ref, h_hbm, wgu_ref, wd_ref, o_ref, rows_buf, sems, h_scr, acc_ref, *, tm):
    t = pl.program_id(0)
    e = pl.program_id(1)
    slot = t % 2
    n_valid = nvalid_ref[0]

    @pl.when(t < n_valid)
    def _():
        @pl.when(e == 0)
        def _():
            @pl.when(t == 0)
            def _():
                _start_row_gather(src_ref, 0, h_hbm, rows_buf.at[0], sems.at[0])

            _wait_row_gather(h_hbm, rows_buf.at[slot], sems.at[slot])

            @pl.when(t + 1 < n_valid)
            def _():
                _start_row_gather(src_ref, (t + 1) * tm, h_hbm, rows_buf.at[1 - slot], sems.at[1 - slot])

            h_scr[...] = rows_buf[slot, :, :D_MODEL].astype(BF16)
            acc_ref[...] = jnp.zeros_like(acc_ref)

        gu = jnp.dot(h_scr[...], wgu_ref[0], preferred_element_type=F32)
        gate = gu[:, :D_EXPERT]
        he = (gate * jax.nn.sigmoid(gate)) * gu[:, D_EXPERT:]
        cw = rows_buf[slot, :, D_MODEL:]
        col = lax.broadcasted_iota(jnp.int32, cw.shape, 1)
        w = jnp.sum(jnp.where(col == gid_ref[t] * PER_GROUP + e, cw, 0.0), axis=-1, keepdims=True)
        acc_ref[...] += w * jnp.dot(he.astype(BF16), wd_ref[0], preferred_element_type=F32)

        @pl.when(e == PER_GROUP - 1)
        def _():
            o_ref[...] = acc_ref[...]

    @pl.when((t >= n_valid) & (e == PER_GROUP - 1))
    def _():
        o_ref[...] = jnp.zeros_like(o_ref)


def _experts(h2ext, src, tile_gid, n_valid, w_gu, w_d, tm):
    n_rows = src.shape[0]
    d = D_MODEL
    expert = lambda t, e, src_ref, gid, nv: (gid[t] * PER_GROUP + e, 0, 0)
    return pl.pallas_call(
        functools.partial(_expert_kernel, tm=tm),
        grid_spec=pltpu.PrefetchScalarGridSpec(
            num_scalar_prefetch=3,
            grid=(n_rows // tm, PER_GROUP),
            in_specs=[pl.BlockSpec(memory_space=pl.ANY),
                      pl.BlockSpec((1, d, 2 * D_EXPERT), expert),
                      pl.BlockSpec((1, D_EXPERT, d), expert)],
            out_specs=pl.BlockSpec((tm, d), lambda t, e, src_ref, gid, nv: (t, 0)),
            scratch_shapes=[pltpu.VMEM((2, tm, d + LANES), F32), pltpu.SemaphoreType.DMA((2,)),
                            pltpu.VMEM((tm, d), BF16), pltpu.VMEM((tm, d), F32)]),
        out_shape=jax.ShapeDtypeStruct((n_rows, d), F32),
        compiler_params=_cparams(2),
        name="experts",
    )(src, tile_gid, n_valid, h2ext, w_gu, w_d)


COMBINE_TILE = 512


def _combine_kernel(dest_ref, ys_hbm, x_ref, g2_ref, fg_ref, o_ref, rows_buf, sems, *, tm, final_norm):
    i = pl.program_id(0)
    slot = i % 2

    @pl.when(i == 0)
    def _():
        _start_row_gather(dest_ref, 0, ys_hbm, rows_buf.at[0], sems.at[0])

    _wait_row_gather(ys_hbm, rows_buf.at[slot], sems.at[slot])

    @pl.when(i + 1 < pl.num_programs(0))
    def _():
        _start_row_gather(dest_ref, (i + 1) * tm, ys_hbm, rows_buf.at[1 - slot], sems.at[1 - slot])

    x = x_ref[0] + g2_ref[0] * rows_buf[slot]
    if final_norm:
        ms = jnp.mean(x * x, axis=-1, keepdims=True)
        x = (x * lax.rsqrt(ms + NORM_EPS)) * fg_ref[...]
    o_ref[0] = x


def _combine(x, ys, dest, g2, final_g, final_norm):
    b, s, d = x.shape
    tm = min(COMBINE_TILE, s)
    per_seq = s // tm
    return pl.pallas_call(
        functools.partial(_combine_kernel, tm=tm, final_norm=final_norm),
        grid_spec=pltpu.PrefetchScalarGridSpec(
            num_scalar_prefetch=1,
            grid=(b * per_seq,),
            in_specs=[pl.BlockSpec(memory_space=pl.ANY),
                      pl.BlockSpec((1, tm, d), lambda i, dest_ref: (i // per_seq, i % per_seq, 0)),
                      pl.BlockSpec((1, 1, d), lambda i, dest_ref: (i // per_seq, 0, 0)),
                      pl.BlockSpec((1, d), lambda i, dest_ref: (0, 0))],
            out_specs=pl.BlockSpec((1, tm, d), lambda i, dest_ref: (i // per_seq, i % per_seq, 0)),
            scratch_shapes=[pltpu.VMEM((2, tm, d), F32), pltpu.SemaphoreType.DMA((2,))]),
        out_shape=jax.ShapeDtypeStruct((b, s, d), F32),
        compiler_params=_cparams(1),
        name="moe_combine",
    )(dest, ys, x, g2.reshape(b, 1, d), final_g.reshape(1, d))


def _moe(x, h2ext, gsel, w_gu, w_d, g2, final_g, final_norm):
    b, s, _ = x.shape
    dest, src, tile_gid, n_valid = _dispatch_plan(gsel, EXPERT_TILE)
    ys = _experts(h2ext.reshape(b * s, -1), src, tile_gid, n_valid, w_gu, w_d, EXPERT_TILE)
    return _combine(x, ys, dest, g2, final_g, final_norm)


def _bias_tables(rel_bias, s):
    tabs = [rel_bias[:, i * HEADS:(i + 1) * HEADS].T for i in range(N_MIXERS)]
    t = {}
    for window, r in DILATED_CONFIGS:
        t["a%d" % r] = _band_bias_tiles(tabs[0], window // r, 128, r, 1)
    t["b"] = _band_bias_tiles(tabs[1], SWA_WINDOW - 1, 128, 1, GQA)
    t["c"] = _flash_bias_tiles(tabs[2], s)
    t["ds"] = _flash_bias_tiles(tabs[3], s)
    t["dw"] = _band_bias_tiles(tabs[3], NSA_WINDOW - 1, NSA_WINDOW, 1, GQA)
    return t


def _token_mixer_branches(p, gates, bias, sink, pe, w1, w2):
    b, _, s, _ = p.shape
    a_units = 4
    a_outs = []
    for window, r in DILATED_CONFIGS:
        if r == 1:
            src, lead, (q0, k0, v0) = p, b, (QA, KA, VA)
        else:
            src = p[:, QA:QA + 3 * HEADS].reshape(b, 3 * HEADS, s // r, r, HEAD_DIM)
            src = src.transpose(0, 3, 1, 2, 4).reshape(b * r, 3 * HEADS, s // r, HEAD_DIM)
            lead, (q0, k0, v0) = b * r, (0, HEADS, 2 * HEADS)
        o = _banded_attention(src, src, src, bias["a%d" % r], grid01=(lead, HEADS // a_units),
                              q_map=lambda i, j, q0=q0: (i, q0 // a_units + j),
                              kv_maps=(lambda i, j, k0=k0: (i, k0 // a_units + j),
                                       lambda i, j, v0=v0: (i, v0 // a_units + j)),
                              bias_map=lambda i, j: j, r_heads=1, units=a_units, w=128, out_heads=HEADS,
                              emit_lse=True, name="dilated_r%d" % r)
        if r > 1:
            o = o.reshape(b, r, HEADS, s // r, 2 * HEAD_DIM).transpose(0, 2, 3, 1, 4)
            o = o.reshape(b, HEADS, s, 2 * HEAD_DIM)
        a_outs.append(o)
    o_b = _banded_attention(p, p, p, bias["b"], grid01=(b, 1),
                            q_map=lambda i, j: (i, QB // HEADS), kv_maps=(lambda i, j: (i, KB // KV_HEADS),
                                                                         lambda i, j: (i, VB // KV_HEADS)),
                            bias_map=lambda i, j: 0, r_heads=GQA, units=KV_HEADS, w=128, out_heads=HEADS,
                            sink=sink, name="swa")
    o_c = _moba_attention(p, bias["c"])
    kvc = _nsa_compress(p, pe, w1, w2)
    o_dc, pen = _nsa_compressed_attention(p, kvc, gates)
    o_ds = _nsa_selected_attention(p, pen, gates, bias["ds"])
    o_dw = _banded_attention(p, p, p, bias["dw"], grid01=(b, KV_HEADS),
                             q_map=lambda i, j: (i, QD // GQA + j), kv_maps=(lambda i, j: (i, KDW + j),
                                                                            lambda i, j: (i, VDW + j)),
                             bias_map=lambda i, j: j, r_heads=GQA, units=1, w=NSA_WINDOW, out_heads=HEADS,
                             gates=gates, gate_branch=2, name="nsa_window")
    return a_outs, o_b, o_c, o_dc, o_ds, o_dw


def _split_w_in(w_in_l):
    w_heads = w_in_l[:, :PROJ_COLS].astype(BF16)
    w_gate = jnp.pad(w_in_l[:, PROJ_COLS:], ((0, 0), (0, LANES - N_GATE_COLS))).astype(BF16)
    return w_heads, w_gate


def _sink_rows(sink_l):
    return jnp.repeat(sink_l.astype(F32).reshape(KV_HEADS, GQA), BAND_TILE, axis=1)[..., None]


def kernel(x, c, rel_bias, router_w, router_bias, norm1_g, norm2_g, ada_w, ada_b, w_in, nsa_pe_k, nsa_pe_v,
           nsa_cmp_w1_k, nsa_cmp_w2_k, nsa_cmp_w1_v, nsa_cmp_w2_v, sinks, mix_norm_g, w_out, exp_w_gate,
           exp_w_up, exp_w_down, final_g):
    b, s, d = x.shape
    mod = _modulation(c, ada_w, ada_b)
    bias = _bias_tables(rel_bias, s)
    rw_t = router_w.T.astype(BF16)
    for l in range(DEPTH):
        sh1, sc1, g1, sh2, sc2, g2 = jnp.split(mod[l], 6, axis=-1)
        w_heads, w_gate = _split_w_in(w_in[l])
        p, gates = _in_projection(x, norm1_g[l], sc1, sh1, w_heads, w_gate)
        pe = jnp.stack([nsa_pe_k[l], nsa_pe_v[l]])
        w1 = jnp.stack([nsa_cmp_w1_k[l], nsa_cmp_w1_v[l]])
        w2 = jnp.stack([nsa_cmp_w2_k[l], nsa_cmp_w2_v[l]])
        a_outs, o_b, o_c, o_dc, o_ds, o_dw = _token_mixer_branches(p, gates, bias, _sink_rows(sinks[l]), pe, w1, w2)
        x, h2ext, gsel = _mixer_out(x, a_outs, o_b, o_c, o_dc, o_ds, o_dw, mix_norm_g[l],
                                    w_out[l].astype(BF16), g1, norm2_g[l], sc2, sh2, rw_t, router_bias)
        w_gu = jnp.concatenate([exp_w_gate[l], exp_w_up[l]], axis=-1).astype(BF16)
        x = _moe(x, h2ext, gsel, w_gu, exp_w_down[l].astype(BF16), g2, final_g, l == DEPTH - 1)
    return x
```

```python
import functools
import math

import numpy as np
import jax
import jax.numpy as jnp
from jax import lax
from jax.experimental import pallas as pl
from jax.experimental.pallas import tpu as pltpu

F32 = jnp.float32
BF16 = jnp.bfloat16

D_MODEL = 2048
DEPTH = 4
HEAD_DIM = 64
N_MIXERS = 4
HEADS = 8
GROUP_WIDTH = HEADS * HEAD_DIM
SCALE = HEAD_DIM ** -0.5
REL_BUCKETS = 32
REL_MAX_DIST = 2048
DILATED_CONFIGS = ((128, 1), (512, 4), (2048, 16))
SWA_WINDOW = 128
KV_HEADS = 2
GQA = HEADS // KV_HEADS
MOBA_BLOCK = 256
MOBA_TOPK = 3
NSA_CMP_BLOCK = 32
NSA_CMP_STRIDE = 16
NSA_CMP_HIDDEN = 128
NSA_SLC_BLOCK = 64
NSA_SLC_TOPN = 16
NSA_WINDOW = 512
N_EXPERTS = 16
N_EXPERT_GROUPS = 4
PER_GROUP = N_EXPERTS // N_EXPERT_GROUPS
D_EXPERT = 512
NORM_EPS = 1e-6
NEG_INF = -1e30
TINY = float(np.finfo(np.float32).tiny)

N_PROJ_HEADS = 80
PROJ_COLS = N_PROJ_HEADS * HEAD_DIM
N_GATE_COLS = HEADS * 3
LANES = 128
SUBLANES = 8
QA, KA, VA = 0, 8, 16
QB, KB, VB = 24, 32, 34
QC, KC, VC = 36, 44, 52
QD, KDC, VDC, KDS, VDS, KDW, VDW = 60, 68, 70, 72, 74, 76, 78

VMEM_LIMIT = 56 * 1024 * 1024


def _cparams(n_axes):
    return pltpu.CompilerParams(dimension_semantics=("arbitrary",) * n_axes,
                                vmem_limit_bytes=VMEM_LIMIT)


def _mod_kernel(c_ref, w_ref, b_ref, o_ref):
    c = c_ref[...]
    ca = (c * jax.nn.sigmoid(c)).astype(BF16)
    o_ref[0] = jnp.dot(ca, w_ref[0].astype(BF16), preferred_element_type=F32) + b_ref[0]


def _modulation(c, ada_w, ada_b):
    depth, d, n = ada_w.shape
    b = c.shape[0]
    bp = 8
    tn = 1024
    cp = jnp.pad(c, ((0, bp - b), (0, 0)))
    out = pl.pallas_call(
        _mod_kernel,
        grid=(depth, n // tn),
        in_specs=[pl.BlockSpec((bp, d), lambda l, j: (0, 0)),
                  pl.BlockSpec((1, d, tn), lambda l, j: (l, 0, j)),
                  pl.BlockSpec((1, 1, tn), lambda l, j: (l, 0, j))],
        out_specs=pl.BlockSpec((1, bp, tn), lambda l, j: (l, 0, j)),
        out_shape=jax.ShapeDtypeStruct((depth, bp, n), F32),
        compiler_params=_cparams(2),
        name="adaln_mod",
    )(cp, ada_w, ada_b.reshape(depth, 1, n))
    return out[:, :b]


def _inproj_kernel(x_ref, g_ref, sc_ref, sh_ref, w_ref, wg_ref, p_ref, gate_ref, h_scr):
    n = pl.program_id(2)

    @pl.when(n == 0)
    def _():
        x = x_ref[0]
        ms = jnp.mean(x * x, axis=-1, keepdims=True)
        h = (x * lax.rsqrt(ms + NORM_EPS)) * g_ref[...] * (1.0 + sc_ref[0]) + sh_ref[0]
        hb = h.astype(BF16)
        h_scr[...] = hb
        gate_ref[0] = jax.nn.sigmoid(jnp.dot(hb, wg_ref[...], preferred_element_type=F32))

    res = jnp.dot(h_scr[...], w_ref[...], preferred_element_type=F32)
    for j in range(HEADS):
        p_ref[0, j] = res[:, j * HEAD_DIM:(j + 1) * HEAD_DIM].astype(BF16)


def _in_projection(x, g, sc, sh, w_heads, w_gate):
    b, s, d = x.shape
    tm = min(1024, s)
    tn = HEADS * HEAD_DIM
    return pl.pallas_call(
        _inproj_kernel,
        grid=(b, s // tm, PROJ_COLS // tn),
        in_specs=[pl.BlockSpec((1, tm, d), lambda i, m, n: (i, m, 0)),
                  pl.BlockSpec((1, d), lambda i, m, n: (0, 0)),
                  pl.BlockSpec((1, 1, d), lambda i, m, n: (i, 0, 0)),
                  pl.BlockSpec((1, 1, d), lambda i, m, n: (i, 0, 0)),
                  pl.BlockSpec((d, tn), lambda i, m, n: (0, n)),
                  pl.BlockSpec((d, LANES), lambda i, m, n: (0, 0))],
        out_specs=[pl.BlockSpec((1, HEADS, tm, HEAD_DIM), lambda i, m, n: (i, n, m, 0)),
                   pl.BlockSpec((1, tm, LANES), lambda i, m, n: (i, m, 0))],
        out_shape=[jax.ShapeDtypeStruct((b, N_PROJ_HEADS, s, HEAD_DIM), BF16),
                   jax.ShapeDtypeStruct((b, s, LANES), F32)],
        scratch_shapes=[pltpu.VMEM((tm, d), BF16)],
        compiler_params=_cparams(3),
        name="in_proj",
    )(x, g.reshape(1, d), sc.reshape(b, 1, d), sh.reshape(b, 1, d), w_heads, w_gate)


def _bucket_thresholds():
    d = np.arange(REL_MAX_DIST + 1)
    max_exact = REL_BUCKETS // 2
    large = max_exact + (np.log(np.maximum(d, 1).astype(np.float32) / np.float32(max_exact))
                         / np.float32(math.log(REL_MAX_DIST / max_exact))
                         * np.float32(REL_BUCKETS - max_exact)).astype(np.int32)
    bucket = np.where(d < max_exact, d, np.minimum(large, REL_BUCKETS - 1))
    return [int(np.argmax(bucket >= b)) for b in range(REL_BUCKETS)]


def _bias_kernel(off_ref, cmin_ref, tab_ref, o_ref, *, max_dist, dist_scale, thresholds):
    h = pl.program_id(0)
    t = pl.program_id(1)
    shape = o_ref.shape[-2:]
    rel = off_ref[t] + lax.broadcasted_iota(jnp.int32, shape, 0) - lax.broadcasted_iota(jnp.int32, shape, 1)
    acc = jnp.full(shape, tab_ref[h, 0], F32)
    for b in range(1, REL_BUCKETS):
        acc = jnp.where(rel >= -(-thresholds[b] // dist_scale), tab_ref[h, b], acc)
    col = lax.broadcasted_iota(jnp.int32, shape, 1)
    valid = (rel >= 0) & (rel <= max_dist) & (col >= cmin_ref[t])
    o_ref[...] = jnp.where(valid, acc, NEG_INF).reshape(o_ref.shape)


def _bias_tiles(tab, offsets, col_mins, rows, cols, max_dist, dist_scale, group):
    n_heads = tab.shape[0]
    n_tiles = len(offsets)
    smem = pl.BlockSpec(memory_space=pltpu.SMEM)
    out = pl.pallas_call(
        functools.partial(_bias_kernel, max_dist=max_dist, dist_scale=dist_scale,
                          thresholds=_bucket_thresholds()),
        grid=(n_heads, n_tiles),
        in_specs=[smem, smem, smem],
        out_specs=pl.BlockSpec((1, 1, 1, rows, cols), lambda h, t: (h // group, t, h % group, 0, 0)),
        out_shape=jax.ShapeDtypeStruct((n_heads // group, n_tiles, group, rows, cols), F32),
        compiler_params=_cparams(2),
        name="bias_tiles",
    )(jnp.asarray(offsets, jnp.int32), jnp.asarray(col_mins, jnp.int32), tab.astype(F32))
    return out.reshape(n_heads // group, n_tiles, group * rows, cols)


BAND_TILE = 128
BAND_STEP = 512


def _gate_column(gates, idx):
    col = lax.broadcasted_iota(jnp.int32, gates.shape, 1)
    return jnp.sum(jnp.where(col == idx, gates, 0.0), axis=-1, keepdims=True)


def _banded_kernel(*refs, units, r_heads, nsub, w, has_sink, gate_branch):
    q_ref, kp_ref, kc_ref, vp_ref, vc_ref, b_ref = refs[:6]
    pos = 6
    sink_ref = gate_ref = None
    if has_sink:
        sink_ref = refs[pos]
        pos += 1
    if gate_branch is not None:
        gate_ref = refs[pos]
        pos += 1
    o_ref, kw_scr, vw_scr = refs[pos:pos + 3]
    t = pl.program_id(2)
    tq = BAND_TILE
    rows = r_heads * tq
    nk = w + tq
    kw_scr[:, :w] = kp_ref[0]
    kw_scr[:, w:] = kc_ref[0]
    vw_scr[:, :w] = vp_ref[0]
    vw_scr[:, w:] = vc_ref[0]
    for u in range(units):
        for j in range(nsub):
            q = q_ref[0, u * r_heads:(u + 1) * r_heads, j * tq:(j + 1) * tq, :].reshape(rows, HEAD_DIM) * SCALE
            k = kw_scr[u, j * tq:j * tq + nk, :]
            v = vw_scr[u, j * tq:j * tq + nk, :]
            tile = jnp.minimum(t * nsub + j, w // tq)
            s = lax.dot_general(q, k, (((1,), (1,)), ((), ())), preferred_element_type=F32) + b_ref[u, tile]
            m = jnp.max(s, axis=-1, keepdims=True)
            if has_sink:
                m = jnp.maximum(m, sink_ref[u])
            e = jnp.exp(s - m)
            den = jnp.sum(e, axis=-1, keepdims=True)
            if has_sink:
                den = den + jnp.exp(sink_ref[u] - m)
            den = jnp.maximum(den, TINY)
            o = jnp.dot(e.astype(BF16), v, preferred_element_type=F32) / den
            parts = []
            for g in range(r_heads):
                og = o[g * tq:(g + 1) * tq]
                if gate_ref is not None:
                    head = (pl.program_id(1) * units + u) * r_heads + g
                    og = og * _gate_column(gate_ref[0, j * tq:(j + 1) * tq, :], head * 3 + gate_branch)
                parts.append(og)
            lanes = slice(u * r_heads * HEAD_DIM, (u + 1) * r_heads * HEAD_DIM)
            o_ref[0, j * tq:(j + 1) * tq, lanes] = jnp.concatenate(parts, axis=-1)


def _band_bias_tiles(tab, max_dist, w, dist_scale, group):
    tq = BAND_TILE
    n_early = w // tq
    return _bias_tiles(tab, [w] * (n_early + 1), [w - e * tq for e in range(n_early)] + [0], tq, w + tq,
                       max_dist, dist_scale, group)


def _banded_attention(q_arr, k_arr, v_arr, bias, *, grid01, q_map, kv_maps, bias_map, r_heads, units, w,
                      out_heads, sink=None, gates=None, gate_branch=None, name):
    seq = q_arr.shape[2]
    tq = BAND_TILE
    step = min(BAND_STEP, seq)
    nsub = step // tq
    assert seq % step == 0 and w % tq == 0 and step % w == 0
    k_map, v_map = kv_maps
    prev = lambda t: jnp.maximum(t * (step // w) - 1, 0)
    in_specs = [pl.BlockSpec((1, units * r_heads, step, HEAD_DIM), lambda i, j, t: q_map(i, j) + (t, 0)),
                pl.BlockSpec((1, units, w, HEAD_DIM), lambda i, j, t: k_map(i, j) + (prev(t), 0)),
                pl.BlockSpec((1, units, step, HEAD_DIM), lambda i, j, t: k_map(i, j) + (t, 0)),
                pl.BlockSpec((1, units, w, HEAD_DIM), lambda i, j, t: v_map(i, j) + (prev(t), 0)),
                pl.BlockSpec((1, units, step, HEAD_DIM), lambda i, j, t: v_map(i, j) + (t, 0)),
                pl.BlockSpec((units, w // tq + 1, r_heads * tq, w + tq), lambda i, j, t: (bias_map(i, j), 0, 0, 0))]
    args = [q_arr, k_arr, k_arr, v_arr, v_arr, bias]
    if sink is not None:
        in_specs.append(pl.BlockSpec((units, r_heads * tq, 1), lambda i, j, t: (bias_map(i, j), 0, 0)))
        args.append(sink)
    if gates is not None:
        in_specs.append(pl.BlockSpec((1, step, LANES), lambda i, j, t: (i, t, 0)))
        args.append(gates)
    n0, n1 = grid01
    return pl.pallas_call(
        functools.partial(_banded_kernel, units=units, r_heads=r_heads, nsub=nsub, w=w,
                          has_sink=sink is not None, gate_branch=gate_branch if gates is not None else None),
        grid=(n0, n1, seq // step),
        in_specs=in_specs,
        out_specs=pl.BlockSpec((1, step, units * r_heads * HEAD_DIM), lambda i, j, t: (i, t, j)),
        out_shape=jax.ShapeDtypeStruct((n0, seq, out_heads * HEAD_DIM), F32),
        scratch_shapes=[pltpu.VMEM((units, w + step, HEAD_DIM), BF16),
                        pltpu.VMEM((units, w + step, HEAD_DIM), BF16)],
        compiler_params=_cparams(3),
        name=name,
    )(*args)


DILATED_WINDOW = 128


def _dilated_kernel(q_ref, k_ref, v_ref, b_ref, o_ref, q_scr, kv_scr, o_scr, l_scr, *, seq, pad):
    tq = BAND_TILE
    w = DILATED_WINDOW
    nt = (((1,), (1,)), ((), ()))
    zeros = jnp.zeros((seq, HEAD_DIM), F32)
    q_scr[...] = jnp.concatenate([q_ref[0, 0].astype(F32) * SCALE, zeros], axis=-1)
    kv_scr[:pad] = jnp.zeros((pad, 2 * HEAD_DIM), F32)
    kv_scr[pad:] = jnp.concatenate([k_ref[0, 0].astype(F32), v_ref[0, 0].astype(F32)], axis=-1)

    def rows(start, n, r):
        return pl.ds(start, n) if r == 1 else pl.ds(start, n, stride=r)

    def tile(cfg, r, j, t):
        first = j + r * (t * tq)
        q = q_scr[rows(first, tq, r), :].astype(BF16)
        kv = kv_scr[rows(pad + first - r * w, w + tq, r), :].astype(BF16)
        s = lax.dot_general(q, kv, nt, preferred_element_type=F32) + b_ref[0, cfg, jnp.minimum(t, 1)]
        m = jnp.max(s, axis=-1, keepdims=True)
        e = jnp.exp(s - m)
        den = jnp.maximum(jnp.sum(e, axis=-1, keepdims=True), TINY)
        o_scr[cfg, rows(first, tq, r), :] = jnp.dot(e.astype(BF16), kv, preferred_element_type=F32) / den
        l_scr[cfg, rows(first, tq, r), :] = jnp.broadcast_to(m + jnp.log(den), (tq, 2 * HEAD_DIM))

    group = 8
    for cfg, (_, r) in enumerate(DILATED_CONFIGS):
        nsub = seq // r // tq

        def body(g, c, cfg=cfg, r=r, nsub=nsub):
            for u in range(group):
                tid = g * group + u
                tile(cfg, r, tid // nsub, tid % nsub)
            return c

        lax.fori_loop(0, r * nsub // group, body, 0)

    chunk = 512
    for c in range(seq // chunk):
        sl = slice(c * chunk, (c + 1) * chunk)
        ls = [l_scr[i, sl] for i in range(len(DILATED_CONFIGS))]
        mx = functools.reduce(jnp.maximum, ls)
        ws = [jnp.exp(l - mx) for l in ls]
        mixed = sum(wi * o_scr[i, sl] for i, wi in enumerate(ws)) / sum(ws)
        o_ref[0, 0, sl, :] = mixed[:, HEAD_DIM:]


def _dilated_attention(p, bias):
    b, _, s, _ = p.shape
    pad = max(r for _, r in DILATED_CONFIGS) * DILATED_WINDOW
    assert all(win // r == DILATED_WINDOW and (s // r) % BAND_TILE == 0 for win, r in DILATED_CONFIGS)
    head_spec = lambda base: pl.BlockSpec((1, 1, s, HEAD_DIM), lambda i, h: (i, base + h, 0, 0))
    n_cfg = len(DILATED_CONFIGS)
    return pl.pallas_call(
        functools.partial(_dilated_kernel, seq=s, pad=pad),
        grid=(b, HEADS),
        in_specs=[head_spec(QA), head_spec(KA), head_spec(VA),
                  pl.BlockSpec((1, n_cfg, 2, BAND_TILE, DILATED_WINDOW + BAND_TILE), lambda i, h: (h, 0, 0, 0, 0))],
        out_specs=pl.BlockSpec((1, 1, s, HEAD_DIM), lambda i, h: (i, h, 0, 0)),
        out_shape=jax.ShapeDtypeStruct((b, HEADS, s, HEAD_DIM), F32),
        scratch_shapes=[pltpu.VMEM((s, 2 * HEAD_DIM), F32),
                        pltpu.VMEM((pad + s, 2 * HEAD_DIM), F32),
                        pltpu.VMEM((n_cfg, s, 2 * HEAD_DIM), F32),
                        pltpu.VMEM((n_cfg, s, 2 * HEAD_DIM), F32)],
        compiler_params=_cparams(2),
        name="dilated",
    )(p, p, p, bias)


def _ranks_along_rows(vals):
    n, t = vals.shape
    groups = [vals[g * SUBLANES:(g + 1) * SUBLANES] for g in range(n // SUBLANES)]
    row = lax.broadcasted_iota(jnp.int32, (SUBLANES, t), 0)
    later = [jnp.where(row > r, 1, 0) for r in range(SUBLANES)]
    cnt = [jnp.zeros((SUBLANES, t), jnp.int32) for _ in groups]
    for i in range(n):
        vi = vals[i:i + 1]
        for g, vg in enumerate(groups):
            if g > i // SUBLANES:
                cnt[g] = cnt[g] + jnp.where(vi >= vg, 1, 0)
            elif g < i // SUBLANES:
                cnt[g] = cnt[g] + jnp.where(vi > vg, 1, 0)
            else:
                cnt[g] = cnt[g] + jnp.where(vi > vg, 1, 0) + jnp.where(vi == vg, later[i % SUBLANES], 0)
    return jnp.concatenate(cnt, axis=0)


def _rows_to_lanes(x_t, width):
    n, t = x_t.shape
    padded = jnp.concatenate([x_t, jnp.zeros((LANES - n, t), x_t.dtype)], axis=0)
    return padded.T[:, :width]


def _tile_groups(n, fn):
    def body(j, c):
        fn(j * 4, 4)
        return c

    lax.fori_loop(0, n // 4, body, 0)
    base = (n // 4) * 4

    @pl.when((n & 2) != 0)
    def _():
        fn(base, 2)

    @pl.when((n & 1) != 0)
    def _():
        fn(base + (n & 2), 1)


def _flash_selected(qa_scr, qi, kaug_scr, kv_of, v_ref, b_ref, s_scr, m_scr, l_scr, acc_scr, tq):
    nh = qa_scr.shape[0]
    half = tq // 2
    n = qi + 1
    nt = (((1,), (1,)), ((), ()))
    m_scr[...] = jnp.full(m_scr.shape, NEG_INF, F32)

    def logits(kj0, cnt):
        ks = pl.multiple_of(kj0 * tq, tq)
        for h in range(nh):
            s = lax.dot_general(qa_scr[h], kaug_scr[kv_of[h], pl.ds(ks, cnt * tq), :], nt,
                                preferred_element_type=F32)
            mloc = None
            for u in range(cnt):
                su = s[:, u * tq:(u + 1) * tq] + b_ref[h, jnp.minimum(qi - kj0 - u, b_ref.shape[1] - 1)]
                s_scr[h, kj0 + u] = su
                mu = jnp.maximum(su[:, :half], su[:, half:])
                mloc = mu if mloc is None else jnp.maximum(mloc, mu)
            m_scr[h] = jnp.maximum(m_scr[h], mloc)

    _tile_groups(n, logits)
    for h in range(nh):
        m_scr[h] = jnp.broadcast_to(jnp.max(m_scr[h], axis=-1, keepdims=True), (tq, half))
    l_scr[...] = jnp.zeros(l_scr.shape, F32)
    acc_scr[...] = jnp.zeros(acc_scr.shape, F32)

    def weigh(kj0, cnt):
        ks = pl.multiple_of(kj0 * tq, tq)
        for h in range(nh):
            mb = m_scr[h]
            parts = []
            lloc = None
            for u in range(cnt):
                su = s_scr[h, kj0 + u]
                e_lo = jnp.exp(su[:, :half] - mb)
                e_hi = jnp.exp(su[:, half:] - mb)
                lu = e_lo + e_hi
                lloc = lu if lloc is None else lloc + lu
                parts += [e_lo.astype(BF16), e_hi.astype(BF16)]
            acc_scr[h] += jnp.dot(jnp.concatenate(parts, axis=-1), v_ref[0, kv_of[h], pl.ds(ks, cnt * tq), :],
                                  preferred_element_type=F32)
            l_scr[h] += lloc

    _tile_groups(n, weigh)
    return [acc_scr[h] / jnp.maximum(jnp.sum(l_scr[h], axis=-1, keepdims=True), TINY) for h in range(nh)]


def _fill_kaug(kaug_scr, k_ref, et_ref):
    for kv in range(kaug_scr.shape[0]):
        kaug_scr[kv, :, :HEAD_DIM] = k_ref[0, kv]
        kaug_scr[kv, :, HEAD_DIM:] = et_ref[...]


def _augment(q, pen):
    return jnp.concatenate([q * SCALE, pen.astype(BF16)], axis=-1)


def _moba_kernel(q_ref, k_ref, v_ref, pool_ref, et_ref, b_ref, o_ref, kaug_scr, kmean_scr, qa_scr, s_scr, m_scr,
                 l_scr, acc_scr, *, tq, nblk, nh):
    qi = pl.program_id(2)

    @pl.when(qi == 0)
    def _():
        _fill_kaug(kaug_scr, k_ref, et_ref)
        for h in range(nh):
            kmean_scr[h] = jnp.dot(pool_ref[...], k_ref[0, h], preferred_element_type=F32).astype(BF16)

    for h in range(nh):
        q = q_ref[0, h]
        gate = lax.dot_general(kmean_scr[h], q, (((1,), (1,)), ((), ())), preferred_element_type=F32)
        blk = lax.broadcasted_iota(jnp.int32, (nblk, tq), 0)
        gate = jnp.where(blk < qi, gate, NEG_INF)
        picked = ((_ranks_along_rows(gate) < MOBA_TOPK) & (blk < qi)) | (blk == qi)
        pen_t = jnp.where(picked, 0.0, NEG_INF)
        qa_scr[h] = _augment(q, _rows_to_lanes(pen_t, HEAD_DIM))
    outs = _flash_selected(qa_scr, qi, kaug_scr, tuple(range(nh)), v_ref, b_ref, s_scr, m_scr, l_scr, acc_scr, tq)
    o_ref[0] = jnp.concatenate(outs, axis=-1)


def _slc_kernel(q_ref, k_ref, v_ref, pen_ref, gate_ref, et_ref, b_ref, o_ref, kaug_scr, qa_scr, s_scr, m_scr,
                l_scr, acc_scr, *, tq, nh):
    qi = pl.program_id(2)

    @pl.when(qi == 0)
    def _():
        _fill_kaug(kaug_scr, k_ref, et_ref)

    for h in range(nh):
        qa_scr[h] = _augment(q_ref[0, h], pen_ref[0, 0])
    outs = _flash_selected(qa_scr, qi, kaug_scr, (0,) * nh, v_ref, b_ref, s_scr, m_scr, l_scr, acc_scr, tq)
    gates = gate_ref[0]
    first = pl.program_id(1) * nh
    o_ref[0] = jnp.concatenate([outs[h] * _gate_column(gates, (first + h) * 3 + 1) for h in range(nh)], axis=-1)


def _flash_state(nh, n_tiles, tq):
    return [pltpu.VMEM((nh, tq, 2 * HEAD_DIM), BF16),
            pltpu.VMEM((nh, n_tiles, tq, tq), F32),
            pltpu.VMEM((nh, tq, tq // 2), F32),
            pltpu.VMEM((nh, tq, tq // 2), F32),
            pltpu.VMEM((nh, tq, HEAD_DIM), F32)]


FLASH_TILE = 256
MOBA_HEADS_PER_STEP = 2
SLC_HEADS_PER_STEP = GQA


def _flash_bias_tiles(tab, seq):
    tq = FLASH_TILE
    first_const = -(-(_bucket_thresholds()[-1] + tq - 1) // tq)
    n_off = min(seq // tq, first_const + 1)
    return _bias_tiles(tab, [o * tq for o in range(n_off)], [0] * n_off, tq, tq, seq, 1, 1)


def _block_onehot_t(nblk, seq):
    blk = seq // nblk
    return jnp.asarray((np.arange(seq)[:, None] // blk == np.arange(HEAD_DIM)[None, :]).astype(np.float32), BF16)


def _moba_attention(p, bias):
    b, _, s, _ = p.shape
    tq, nh = FLASH_TILE, MOBA_HEADS_PER_STEP
    assert tq == MOBA_BLOCK
    nblk = s // tq
    nrow = max(nblk, 16)
    n_bias = bias.shape[1]
    pool = jnp.asarray((np.arange(s)[None, :] // tq == np.arange(nrow)[:, None]).astype(np.float32) / tq, BF16)
    kv_spec = lambda base: pl.BlockSpec((1, nh, s, HEAD_DIM), lambda i, h, t: (i, base // nh + h, 0, 0))
    return pl.pallas_call(
        functools.partial(_moba_kernel, tq=tq, nblk=nrow, nh=nh),
        grid=(b, HEADS // nh, s // tq),
        in_specs=[pl.BlockSpec((1, nh, tq, HEAD_DIM), lambda i, h, t: (i, QC // nh + h, t, 0)),
                  kv_spec(KC), kv_spec(VC),
                  pl.BlockSpec((nrow, s), lambda i, h, t: (0, 0)),
                  pl.BlockSpec((s, HEAD_DIM), lambda i, h, t: (0, 0)),
                  pl.BlockSpec((nh, n_bias, tq, tq), lambda i, h, t: (h, 0, 0, 0))],
        out_specs=pl.BlockSpec((1, tq, nh * HEAD_DIM), lambda i, h, t: (i, t, h)),
        out_shape=jax.ShapeDtypeStruct((b, s, HEADS * HEAD_DIM), F32),
        scratch_shapes=[pltpu.VMEM((nh, s, 2 * HEAD_DIM), BF16),
                        pltpu.VMEM((nh, nrow, HEAD_DIM), BF16)] + _flash_state(nh, s // tq, tq),
        compiler_params=_cparams(3),
        name="moba",
    )(p, p, p, pool, _block_onehot_t(nblk, s), bias)


def _nsa_selected_attention(p, pen, gates, bias):
    b, _, s, _ = p.shape
    tq, nh = FLASH_TILE, SLC_HEADS_PER_STEP
    nblk = s // NSA_SLC_BLOCK
    n_bias = bias.shape[1]
    assert nblk <= HEAD_DIM and GQA % nh == 0
    kv_spec = lambda base: pl.BlockSpec((1, 1, s, HEAD_DIM), lambda i, h, t: (i, base + (h * nh) // GQA, 0, 0))
    return pl.pallas_call(
        functools.partial(_slc_kernel, tq=tq, nh=nh),
        grid=(b, HEADS // nh, s // tq),
        in_specs=[pl.BlockSpec((1, nh, tq, HEAD_DIM), lambda i, h, t: (i, QD // nh + h, t, 0)),
                  kv_spec(KDS), kv_spec(VDS),
                  pl.BlockSpec((1, 1, tq, HEAD_DIM), lambda i, h, t: (i, (h * nh) // GQA, t, 0)),
                  pl.BlockSpec((1, tq, LANES), lambda i, h, t: (i, t, 0)),
                  pl.BlockSpec((s, HEAD_DIM), lambda i, h, t: (0, 0)),
                  pl.BlockSpec((nh, n_bias, tq, tq), lambda i, h, t: (h, 0, 0, 0))],
        out_specs=pl.BlockSpec((1, tq, nh * HEAD_DIM), lambda i, h, t: (i, t, h)),
        out_shape=jax.ShapeDtypeStruct((b, s, HEADS * HEAD_DIM), F32),
        scratch_shapes=[pltpu.VMEM((1, s, 2 * HEAD_DIM), BF16)] + _flash_state(nh, s // tq, tq),
        compiler_params=_cparams(3),
        name="nsa_selected",
    )(p, p, p, pen, gates, _block_onehot_t(nblk, s), bias)


def _compress_kernel(t_ref, pe_ref, w1_ref, w2_ref, o_ref):
    t = t_ref[0, 0].astype(F32)
    half = t.shape[-1]
    top = (t + pe_ref[0, 0:1, :]).astype(BF16)
    bot = (t + pe_ref[0, 1:2, :]).astype(BF16)
    u = jnp.dot(top, w1_ref[0, :half, :], preferred_element_type=F32)
    vv = jnp.dot(bot, w1_ref[0, half:, :], preferred_element_type=F32)
    pre = u + pltpu.roll(vv, vv.shape[0] - 1, axis=0)
    hid = jax.nn.gelu(pre).astype(BF16)
    o_ref[0, 0] = jnp.dot(hid, w2_ref[0], preferred_element_type=F32).astype(BF16)


def _nsa_compress(p, pe, w1, w2):
    b, _, s, _ = p.shape
    n_half = s // NSA_CMP_STRIDE
    half = NSA_CMP_STRIDE * HEAD_DIM
    t = p[:, KDC:KDC + 2 * KV_HEADS].reshape(b, 2 * KV_HEADS, n_half, half)
    pe2 = pe.reshape(2, 2, half)
    return pl.pallas_call(
        _compress_kernel,
        grid=(b, 2 * KV_HEADS),
        in_specs=[pl.BlockSpec((1, 1, n_half, half), lambda i, j: (i, j, 0, 0)),
                  pl.BlockSpec((1, 2, half), lambda i, j: (j // KV_HEADS, 0, 0)),
                  pl.BlockSpec((1, 2 * half, NSA_CMP_HIDDEN), lambda i, j: (j // KV_HEADS, 0, 0)),
                  pl.BlockSpec((1, NSA_CMP_HIDDEN, HEAD_DIM), lambda i, j: (j // KV_HEADS, 0, 0))],
        out_specs=pl.BlockSpec((1, 1, n_half, HEAD_DIM), lambda i, j: (i, j, 0, 0)),
        out_shape=jax.ShapeDtypeStruct((b, 2 * KV_HEADS, n_half, HEAD_DIM), BF16),
        compiler_params=_cparams(2),
        name="nsa_compress",
    )(t, pe2, w1.astype(BF16), w2.astype(BF16))


def _cmp_attn_kernel(q_ref, kc_ref, vc_ref, ov_ref, gate_ref, o_ref, pen_ref, *, tq, n_slc):
    qi = pl.program_id(2)
    rows = GQA * tq
    q = q_ref[0].reshape(rows, HEAD_DIM)
    kc = kc_ref[0, 0]
    n_cmp = kc.shape[0]
    s = lax.dot_general(q, kc, (((1,), (1,)), ((), ())), preferred_element_type=F32) * SCALE
    row = lax.broadcasted_iota(jnp.int32, (rows, n_cmp), 0)
    pos = qi * tq + (row & (tq - 1))
    blk_end = lax.broadcasted_iota(jnp.int32, (rows, n_cmp), 1) * NSA_CMP_STRIDE + (NSA_CMP_BLOCK - 1)
    mask = blk_end <= pos
    s = jnp.where(mask, s, NEG_INF)
    m = jnp.max(s, axis=-1, keepdims=True)
    e = jnp.where(mask, jnp.exp(s - m), 0.0)
    den = jnp.maximum(jnp.sum(e, axis=-1, keepdims=True), TINY)
    pr = e / den
    o = jnp.dot(pr.astype(BF16), vc_ref[0, 0], preferred_element_type=F32)
    gates = gate_ref[0]
    first = pl.program_id(1) * GQA
    o_ref[0] = jnp.concatenate([o[g * tq:(g + 1) * tq] * _gate_column(gates, (first + g) * 3)
                                for g in range(GQA)], axis=-1)
    psum = pr[0:tq]
    for g in range(1, GQA):
        psum = psum + pr[g * tq:(g + 1) * tq]
    ov_t = ov_ref[...]
    hi = psum.astype(BF16)
    r1 = psum - hi.astype(F32)
    mid = r1.astype(BF16)
    lo = (r1 - mid.astype(F32)).astype(BF16)
    nt = (((1,), (1,)), ((), ()))
    imp_t = (lax.dot_general(ov_t, hi, nt, preferred_element_type=F32)
             + lax.dot_general(ov_t, mid, nt, preferred_element_type=F32)
             + lax.dot_general(ov_t, lo, nt, preferred_element_type=F32))
    jj = lax.broadcasted_iota(jnp.int32, (n_slc, tq), 0)
    cur = (qi * tq + lax.broadcasted_iota(jnp.int32, (n_slc, tq), 1)) // NSA_SLC_BLOCK
    forced = (jj == 0) | (jj == cur) | (jj == cur - 1)
    imp_t = jnp.where(forced, -NEG_INF, jnp.where(jj <= cur, imp_t, NEG_INF))
    picked = (_ranks_along_rows(imp_t) < NSA_SLC_TOPN) & (jj <= cur)
    pen_ref[0, 0] = _rows_to_lanes(jnp.where(picked, 0.0, NEG_INF), HEAD_DIM).astype(BF16)


def _nsa_compressed_attention(p, kvc, gates):
    b, _, s, _ = p.shape
    tq = 128
    n_cmp = kvc.shape[2]
    n_slc = s // NSA_SLC_BLOCK
    assert n_slc == HEAD_DIM
    c_start = np.arange(n_cmp) * NSA_CMP_STRIDE
    j_start = np.arange(n_slc) * NSA_SLC_BLOCK
    overlap = ((c_start[:, None] < j_start[None, :] + NSA_SLC_BLOCK)
               & (c_start[:, None] + NSA_CMP_BLOCK > j_start[None, :])).astype(np.float32)
    overlap[n_cmp - 1] = 0.0
    return pl.pallas_call(
        functools.partial(_cmp_attn_kernel, tq=tq, n_slc=n_slc),
        grid=(b, KV_HEADS, s // tq),
        in_specs=[pl.BlockSpec((1, GQA, tq, HEAD_DIM), lambda i, h, t: (i, QD // GQA + h, t, 0)),
                  pl.BlockSpec((1, 1, n_cmp, HEAD_DIM), lambda i, h, t: (i, h, 0, 0)),
                  pl.BlockSpec((1, 1, n_cmp, HEAD_DIM), lambda i, h, t: (i, KV_HEADS + h, 0, 0)),
                  pl.BlockSpec((n_slc, n_cmp), lambda i, h, t: (0, 0)),
                  pl.BlockSpec((1, tq, LANES), lambda i, h, t: (i, t, 0))],
        out_specs=[pl.BlockSpec((1, tq, GQA * HEAD_DIM), lambda i, h, t: (i, t, h)),
                   pl.BlockSpec((1, 1, tq, HEAD_DIM), lambda i, h, t: (i, h, t, 0))],
        out_shape=[jax.ShapeDtypeStruct((b, s, HEADS * HEAD_DIM), F32),
                   jax.ShapeDtypeStruct((b, KV_HEADS, s, HEAD_DIM), BF16)],
        compiler_params=_cparams(3),
        name="nsa_compressed",
    )(p, kvc, kvc, jnp.asarray(overlap.T, BF16), gates)


def _group_norm(o, g):
    ms = jnp.mean(o * o, axis=-1, keepdims=True)
    return (o * lax.rsqrt(ms + NORM_EPS)) * g


def _route(logits_t, bias_t):
    aff = jax.nn.sigmoid(logits_t)
    score = aff + bias_t
    rows = [score[e:e + 1] for e in range(N_EXPERTS)]
    affs = [aff[e:e + 1] for e in range(N_EXPERTS)]
    gscore = []
    for g in range(N_EXPERT_GROUPS):
        mem = rows[g * PER_GROUP:(g + 1) * PER_GROUP]
        best = None
        for a in range(PER_GROUP):
            for c in range(a + 1, PER_GROUP):
                pair = mem[a] + mem[c]
                best = pair if best is None else jnp.maximum(best, pair)
        gscore.append(best)
    gsel = jnp.zeros_like(gscore[0], dtype=jnp.int32)
    gbest = gscore[0]
    for g in range(1, N_EXPERT_GROUPS):
        better = gscore[g] > gbest
        gsel = jnp.where(better, g, gsel)
        gbest = jnp.where(better, gscore[g], gbest)
    masked = [jnp.where(gsel == e // PER_GROUP, rows[e], NEG_INF) for e in range(N_EXPERTS)]

    def argbest(vals):
        idx = jnp.zeros_like(gsel)
        best = vals[0]
        for e in range(1, N_EXPERTS):
            better = vals[e] > best
            idx = jnp.where(better, e, idx)
            best = jnp.where(better, vals[e], best)
        return idx

    e1 = argbest(masked)
    e2 = argbest([jnp.where(e1 == e, -jnp.inf, masked[e]) for e in range(N_EXPERTS)])
    w1 = sum(jnp.where(e1 == e, affs[e], 0.0) for e in range(N_EXPERTS))
    w2 = sum(jnp.where(e2 == e, affs[e], 0.0) for e in range(N_EXPERTS))
    tot = w1 + w2
    w1 = w1 / tot
    w2 = w2 / tot
    comb = jnp.concatenate([jnp.where(e1 == e, w1, 0.0) + jnp.where(e2 == e, w2, 0.0)
                            for e in range(N_EXPERTS)], axis=0)
    return comb, gsel


def _out_kernel(x_ref, oa_ref, ob_ref, oc_ref, dc_ref, ds_ref, dw_ref,
                mixg_ref, wout_ref, g1_ref, n2g_ref, sc2_ref, sh2_ref, rw_ref, rb_ref,
                xo_ref, h2_ref, gsel_ref):
    o_a = jnp.concatenate([oa_ref[0, h] for h in range(HEADS)], axis=-1)
    o_b = ob_ref[0]
    o_c = oc_ref[0]
    o_d = dc_ref[0] + ds_ref[0] + dw_ref[0]
    mixed = jnp.concatenate([_group_norm(o, mixg_ref[i:i + 1, :]).astype(BF16)
                             for i, o in enumerate((o_a, o_b, o_c, o_d))], axis=-1)
    y = jnp.dot(mixed, wout_ref[...], preferred_element_type=F32)
    x = x_ref[0] + g1_ref[0] * y
    xo_ref[0] = x
    ms = jnp.mean(x * x, axis=-1, keepdims=True)
    h2 = (x * lax.rsqrt(ms + NORM_EPS)) * n2g_ref[...] * (1.0 + sc2_ref[0]) + sh2_ref[0]
    logits_t = lax.dot_general(rw_ref[...], h2.astype(BF16), (((1,), (1,)), ((), ())),
                               preferred_element_type=F32)
    comb_t, gsel = _route(logits_t, rb_ref[...])
    h2_ref[0] = jnp.concatenate([h2, _rows_to_lanes(comb_t, LANES)], axis=-1)
    gsel_ref[0] = gsel


def _mixer_out(x, o_a, o_b, o_c, o_dc, o_ds, o_dw, mix_g, w_out, g1, n2g, sc2, sh2, rw_t, rb):
    b, s, d = x.shape
    tm = min(256, s)
    head_spec = pl.BlockSpec((1, HEADS, tm, HEAD_DIM), lambda i, m: (i, 0, m, 0))
    mixer_spec = pl.BlockSpec((1, tm, GROUP_WIDTH), lambda i, m: (i, m, 0))
    vec_b = pl.BlockSpec((1, 1, d), lambda i, m: (i, 0, 0))
    return pl.pallas_call(
        _out_kernel,
        grid=(b, s // tm),
        in_specs=[pl.BlockSpec((1, tm, d), lambda i, m: (i, m, 0)),
                  head_spec, mixer_spec, mixer_spec, mixer_spec, mixer_spec, mixer_spec,
                  pl.BlockSpec((N_MIXERS, GROUP_WIDTH), lambda i, m: (0, 0)),
                  pl.BlockSpec((N_MIXERS * GROUP_WIDTH, d), lambda i, m: (0, 0)),
                  vec_b,
                  pl.BlockSpec((1, d), lambda i, m: (0, 0)),
                  vec_b, vec_b,
                  pl.BlockSpec((N_EXPERTS, d), lambda i, m: (0, 0)),
                  pl.BlockSpec((N_EXPERTS, 1), lambda i, m: (0, 0))],
        out_specs=[pl.BlockSpec((1, tm, d), lambda i, m: (i, m, 0)),
                   pl.BlockSpec((1, tm, d + LANES), lambda i, m: (i, m, 0)),
                   pl.BlockSpec((1, 1, tm), lambda i, m: (i, 0, m))],
        out_shape=[jax.ShapeDtypeStruct((b, s, d), F32),
                   jax.ShapeDtypeStruct((b, s, d + LANES), F32),
                   jax.ShapeDtypeStruct((b, 1, s), jnp.int32)],
        compiler_params=_cparams(2),
        name="mixer_out",
    )(x, o_a, o_b, o_c, o_dc, o_ds, o_dw, mix_g, w_out, g1.reshape(b, 1, d), n2g.reshape(1, d),
      sc2.reshape(b, 1, d), sh2.reshape(b, 1, d), rw_t, rb.reshape(N_EXPERTS, 1))


EXPERT_TILE = 512


def _dispatch_plan(gsel, tm):
    gid = gsel.reshape(-1)
    n_tok = gid.shape[0]
    onehot = (gid[:, None] == jnp.arange(N_EXPERT_GROUPS, dtype=jnp.int32)[None, :]).astype(jnp.int32)
    csum = jnp.cumsum(onehot, axis=0)
    rank = jnp.sum((csum - onehot) * onehot, axis=1)
    padded = ((csum[-1] + tm - 1) // tm) * tm
    ends = jnp.cumsum(padded)
    dest = jnp.sum(onehot * (ends - padded)[None, :], axis=1) + rank
    n_rows = n_tok + N_EXPERT_GROUPS * tm
    src = jnp.zeros((n_rows,), jnp.int32).at[dest].set(jnp.arange(n_tok, dtype=jnp.int32), unique_indices=True)
    tile_start = jnp.arange(n_rows // tm, dtype=jnp.int32) * tm
    tile_gid = jnp.minimum(jnp.sum((tile_start[:, None] >= ends[None, :]).astype(jnp.int32), axis=1),
                           N_EXPERT_GROUPS - 1)
    return dest.astype(jnp.int32), src, tile_gid, (ends[-1] // tm).reshape(1).astype(jnp.int32)


def _start_row_gather(idx_ref, base, src_hbm, buf, sem):
    def issue(r, c):
        pltpu.make_async_copy(src_hbm.at[pl.ds(idx_ref[base + r], 1)], buf.at[pl.ds(r, 1)], sem).start()
        return c

    lax.fori_loop(0, buf.shape[0], issue, 0, unroll=8)


def _wait_row_gather(src_hbm, buf, sem):
    pltpu.make_async_copy(src_hbm.at[pl.ds(0, buf.shape[0])], buf, sem).wait()


def _expert_kernel(src_ref, gid_ref, nvalid_ref, h_hbm, wgu_ref, wd_ref, o_ref, rows_buf, sems, h_scr, acc_ref, *, tm):
    t = pl.program_id(0)
    e = pl.program_id(1)
    slot = t % 2
    n_valid = nvalid_ref[0]

    @pl.when(t < n_valid)
    def _():
        @pl.when(e == 0)
        def _():
            @pl.when(t == 0)
            def _():
                _start_row_gather(src_ref, 0, h_hbm, rows_buf.at[0], sems.at[0])

            _wait_row_gather(h_hbm, rows_buf.at[slot], sems.at[slot])

            @pl.when(t + 1 < n_valid)
            def _():
                _start_row_gather(src_ref, (t + 1) * tm, h_hbm, rows_buf.at[1 - slot], sems.at[1 - slot])

            h_scr[...] = rows_buf[slot, :, :D_MODEL].astype(BF16)
            acc_ref[...] = jnp.zeros_like(acc_ref)

        gu = jnp.dot(h_scr[...], wgu_ref[0], preferred_element_type=F32)
        gate = gu[:, :D_EXPERT]
        he = (gate * jax.nn.sigmoid(gate)) * gu[:, D_EXPERT:]
        cw = rows_buf[slot, :, D_MODEL:]
        col = lax.broadcasted_iota(jnp.int32, cw.shape, 1)
        w = jnp.sum(jnp.where(col == gid_ref[t] * PER_GROUP + e, cw, 0.0), axis=-1, keepdims=True)
        acc_ref[...] += w * jnp.dot(he.astype(BF16), wd_ref[0], preferred_element_type=F32)

        @pl.when(e == PER_GROUP - 1)
        def _():
            o_ref[...] = acc_ref[...]

    @pl.when((t >= n_valid) & (e == PER_GROUP - 1))
    def _():
        o_ref[...] = jnp.zeros_like(o_ref)


def _experts(h2ext, src, tile_gid, n_valid, w_gu, w_d, tm):
    n_rows = src.shape[0]
    d = D_MODEL
    expert = lambda t, e, src_ref, gid, nv: (gid[t] * PER_GROUP + e, 0, 0)
    return pl.pallas_call(
        functools.partial(_expert_kernel, tm=tm),
        grid_spec=pltpu.PrefetchScalarGridSpec(
            num_scalar_prefetch=3,
            grid=(n_rows // tm, PER_GROUP),
            in_specs=[pl.BlockSpec(memory_space=pl.ANY),
                      pl.BlockSpec((1, d, 2 * D_EXPERT), expert),
                      pl.BlockSpec((1, D_EXPERT, d), expert)],
            out_specs=pl.BlockSpec((tm, d), lambda t, e, src_ref, gid, nv: (t, 0)),
            scratch_shapes=[pltpu.VMEM((2, tm, d + LANES), F32), pltpu.SemaphoreType.DMA((2,)),
                            pltpu.VMEM((tm, d), BF16), pltpu.VMEM((tm, d), F32)]),
        out_shape=jax.ShapeDtypeStruct((n_rows, d), F32),
        compiler_params=_cparams(2),
        name="experts",
    )(src, tile_gid, n_valid, h2ext, w_gu, w_d)


COMBINE_TILE = 512


def _combine_kernel(dest_ref, ys_hbm, x_ref, g2_ref, fg_ref, o_ref, rows_buf, sems, *, tm, final_norm):
    i = pl.program_id(0)
    slot = i % 2

    @pl.when(i == 0)
    def _():
        _start_row_gather(dest_ref, 0, ys_hbm, rows_buf.at[0], sems.at[0])

    _wait_row_gather(ys_hbm, rows_buf.at[slot], sems.at[slot])

    @pl.when(i + 1 < pl.num_programs(0))
    def _():
        _start_row_gather(dest_ref, (i + 1) * tm, ys_hbm, rows_buf.at[1 - slot], sems.at[1 - slot])

    x = x_ref[0] + g2_ref[0] * rows_buf[slot]
    if final_norm:
        ms = jnp.mean(x * x, axis=-1, keepdims=True)
        x = (x * lax.rsqrt(ms + NORM_EPS)) * fg_ref[...]
    o_ref[0] = x


def _combine(x, ys, dest, g2, final_g, final_norm):
    b, s, d = x.shape
    tm = min(COMBINE_TILE, s)
    per_seq = s // tm
    return pl.pallas_call(
        functools.partial(_combine_kernel, tm=tm, final_norm=final_norm),
        grid_spec=pltpu.PrefetchScalarGridSpec(
            num_scalar_prefetch=1,
            grid=(b * per_seq,),
            in_specs=[pl.BlockSpec(memory_space=pl.ANY),
                      pl.BlockSpec((1, tm, d), lambda i, dest_ref: (i // per_seq, i % per_seq, 0)),
                      pl.BlockSpec((1, 1, d), lambda i, dest_ref: (i // per_seq, 0, 0)),
                      pl.BlockSpec((1, d), lambda i, dest_ref: (0, 0))],
            out_specs=pl.BlockSpec((1, tm, d), lambda i, dest_ref: (i // per_seq, i % per_seq, 0)),
            scratch_shapes=[pltpu.VMEM((2, tm, d), F32), pltpu.SemaphoreType.DMA((2,))]),
        out_shape=jax.ShapeDtypeStruct((b, s, d), F32),
        compiler_params=_cparams(1),
        name="moe_combine",
    )(dest, ys, x, g2.reshape(b, 1, d), final_g.reshape(1, d))


def _moe(x, h2ext, gsel, w_gu, w_d, g2, final_g, final_norm):
    b, s, _ = x.shape
    dest, src, tile_gid, n_valid = _dispatch_plan(gsel, EXPERT_TILE)
    ys = _experts(h2ext.reshape(b * s, -1), src, tile_gid, n_valid, w_gu, w_d, EXPERT_TILE)
    return _combine(x, ys, dest, g2, final_g, final_norm)


def _bias_tables(rel_bias, s):
    tabs = [rel_bias[:, i * HEADS:(i + 1) * HEADS].T for i in range(N_MIXERS)]
    t = {}
    t["a"] = jnp.stack([_band_bias_tiles(tabs[0], window // r, DILATED_WINDOW, r, 1)
                        for window, r in DILATED_CONFIGS], axis=1)
    t["b"] = _band_bias_tiles(tabs[1], SWA_WINDOW - 1, 128, 1, GQA)
    t["c"] = _flash_bias_tiles(tabs[2], s)
    t["ds"] = _flash_bias_tiles(tabs[3], s)
    t["dw"] = _band_bias_tiles(tabs[3], NSA_WINDOW - 1, NSA_WINDOW, 1, GQA)
    return t


def _token_mixer_branches(p, gates, bias, sink, pe, w1, w2):
    b, _, s, _ = p.shape
    o_a = _dilated_attention(p, bias["a"])
    o_b = _banded_attention(p, p, p, bias["b"], grid01=(b, 1),
                            q_map=lambda i, j: (i, QB // HEADS), kv_maps=(lambda i, j: (i, KB // KV_HEADS),
                                                                         lambda i, j: (i, VB // KV_HEADS)),
                            bias_map=lambda i, j: 0, r_heads=GQA, units=KV_HEADS, w=128, out_heads=HEADS,
                            sink=sink, name="swa")
    o_c = _moba_attention(p, bias["c"])
    kvc = _nsa_compress(p, pe, w1, w2)
    o_dc, pen = _nsa_compressed_attention(p, kvc, gates)
    o_ds = _nsa_selected_attention(p, pen, gates, bias["ds"])
    o_dw = _banded_attention(p, p, p, bias["dw"], grid01=(b, KV_HEADS),
                             q_map=lambda i, j: (i, QD // GQA + j), kv_maps=(lambda i, j: (i, KDW + j),
                                                                            lambda i, j: (i, VDW + j)),
                             bias_map=lambda i, j: j, r_heads=GQA, units=1, w=NSA_WINDOW, out_heads=HEADS,
                             gates=gates, gate_branch=2, name="nsa_window")
    return o_a, o_b, o_c, o_dc, o_ds, o_dw


def _split_w_in(w_in_l):
    w_heads = w_in_l[:, :PROJ_COLS].astype(BF16)
    w_gate = jnp.pad(w_in_l[:, PROJ_COLS:], ((0, 0), (0, LANES - N_GATE_COLS))).astype(BF16)
    return w_heads, w_gate


def _sink_rows(sink_l):
    return jnp.repeat(sink_l.astype(F32).reshape(KV_HEADS, GQA), BAND_TILE, axis=1)[..., None]


def kernel(x, c, rel_bias, router_w, router_bias, norm1_g, norm2_g, ada_w, ada_b, w_in, nsa_pe_k, nsa_pe_v,
           nsa_cmp_w1_k, nsa_cmp_w2_k, nsa_cmp_w1_v, nsa_cmp_w2_v, sinks, mix_norm_g, w_out, exp_w_gate,
           exp_w_up, exp_w_down, final_g):
    b, s, d = x.shape
    mod = _modulation(c, ada_w, ada_b)
    bias = _bias_tables(rel_bias, s)
    rw_t = router_w.T.astype(BF16)
    for l in range(DEPTH):
        sh1, sc1, g1, sh2, sc2, g2 = jnp.split(mod[l], 6, axis=-1)
        w_heads, w_gate = _split_w_in(w_in[l])
        p, gates = _in_projection(x, norm1_g[l], sc1, sh1, w_heads, w_gate)
        pe = jnp.stack([nsa_pe_k[l], nsa_pe_v[l]])
        w1 = jnp.stack([nsa_cmp_w1_k[l], nsa_cmp_w1_v[l]])
        w2 = jnp.stack([nsa_cmp_w2_k[l], nsa_cmp_w2_v[l]])
        o_a, o_b, o_c, o_dc, o_ds, o_dw = _token_mixer_branches(p, gates, bias, _sink_rows(sinks[l]), pe, w1, w2)
        x, h2ext, gsel = _mixer_out(x, o_a, o_b, o_c, o_dc, o_ds, o_dw, mix_norm_g[l],
                                    w_out[l].astype(BF16), g1, norm2_g[l], sc2, sh2, rw_t, router_bias)
        w_gu = jnp.concatenate([exp_w_gate[l], exp_w_up[l]], axis=-1).astype(BF16)
        x = _moe(x, h2ext, gsel, w_gu, exp_w_down[l].astype(BF16), g2, final_g, l == DEPTH - 1)
    return x
```

```python
import functools
import math

import numpy as np
import jax
import jax.numpy as jnp
from jax import lax
from jax.experimental import pallas as pl
from jax.experimental.pallas import tpu as pltpu

F32 = jnp.float32
BF16 = jnp.bfloat16

D_MODEL = 2048
DEPTH = 4
HEAD_DIM = 64
N_MIXERS = 4
HEADS = 8
GROUP_WIDTH = HEADS * HEAD_DIM
SCALE = HEAD_DIM ** -0.5
REL_BUCKETS = 32
REL_MAX_DIST = 2048
DILATED_CONFIGS = ((128, 1), (512, 4), (2048, 16))
SWA_WINDOW = 128
KV_HEADS = 2
GQA = HEADS // KV_HEADS
MOBA_BLOCK = 256
MOBA_TOPK = 3
NSA_CMP_BLOCK = 32
NSA_CMP_STRIDE = 16
NSA_CMP_HIDDEN = 128
NSA_SLC_BLOCK = 64
NSA_SLC_TOPN = 16
NSA_WINDOW = 512
N_EXPERTS = 16
N_EXPERT_GROUPS = 4
PER_GROUP = N_EXPERTS // N_EXPERT_GROUPS
D_EXPERT = 512
NORM_EPS = 1e-6
NEG_INF = -1e30
TINY = float(np.finfo(np.float32).tiny)

N_PROJ_HEADS = 80
PROJ_COLS = N_PROJ_HEADS * HEAD_DIM
N_GATE_COLS = HEADS * 3
LANES = 128
SUBLANES = 8
QA, KA, VA = 0, 8, 16
QB, KB, VB = 24, 32, 34
QC, KC, VC = 36, 44, 52
QD, KDC, VDC, KDS, VDS, KDW, VDW = 60, 68, 70, 72, 74, 76, 78

VMEM_LIMIT = 56 * 1024 * 1024


def _cparams(n_axes):
    return pltpu.CompilerParams(dimension_semantics=("arbitrary",) * n_axes,
                                vmem_limit_bytes=VMEM_LIMIT)


def _mod_kernel(c_ref, w_ref, b_ref, o_ref):
    c = c_ref[...]
    ca = (c * jax.nn.sigmoid(c)).astype(BF16)
    o_ref[0] = jnp.dot(ca, w_ref[0].astype(BF16), preferred_element_type=F32) + b_ref[0]


def _modulation(c, ada_w, ada_b):
    depth, d, n = ada_w.shape
    b = c.shape[0]
    bp = 8
    tn = 1024
    cp = jnp.pad(c, ((0, bp - b), (0, 0)))
    out = pl.pallas_call(
        _mod_kernel,
        grid=(depth, n // tn),
        in_specs=[pl.BlockSpec((bp, d), lambda l, j: (0, 0)),
                  pl.BlockSpec((1, d, tn), lambda l, j: (l, 0, j)),
                  pl.BlockSpec((1, 1, tn), lambda l, j: (l, 0, j))],
        out_specs=pl.BlockSpec((1, bp, tn), lambda l, j: (l, 0, j)),
        out_shape=jax.ShapeDtypeStruct((depth, bp, n), F32),
        compiler_params=_cparams(2),
        name="adaln_mod",
    )(cp, ada_w, ada_b.reshape(depth, 1, n))
    return out[:, :b]


def _inproj_kernel(x_ref, g_ref, sc_ref, sh_ref, w_ref, wg_ref, p_ref, gate_ref, h_scr):
    n = pl.program_id(2)

    @pl.when(n == 0)
    def _():
        x = x_ref[0]
        ms = jnp.mean(x * x, axis=-1, keepdims=True)
        h = (x * lax.rsqrt(ms + NORM_EPS)) * g_ref[...] * (1.0 + sc_ref[0]) + sh_ref[0]
        hb = h.astype(BF16)
        h_scr[...] = hb
        gate_ref[0] = jax.nn.sigmoid(jnp.dot(hb, wg_ref[...], preferred_element_type=F32))

    res = jnp.dot(h_scr[...], w_ref[...], preferred_element_type=F32)
    for j in range(HEADS):
        p_ref[0, j] = res[:, j * HEAD_DIM:(j + 1) * HEAD_DIM].astype(BF16)


def _in_projection(x, g, sc, sh, w_heads, w_gate):
    b, s, d = x.shape
    tm = min(1024, s)
    tn = HEADS * HEAD_DIM
    return pl.pallas_call(
        _inproj_kernel,
        grid=(b, s // tm, PROJ_COLS // tn),
        in_specs=[pl.BlockSpec((1, tm, d), lambda i, m, n: (i, m, 0)),
                  pl.BlockSpec((1, d), lambda i, m, n: (0, 0)),
                  pl.BlockSpec((1, 1, d), lambda i, m, n: (i, 0, 0)),
                  pl.BlockSpec((1, 1, d), lambda i, m, n: (i, 0, 0)),
                  pl.BlockSpec((d, tn), lambda i, m, n: (0, n)),
                  pl.BlockSpec((d, LANES), lambda i, m, n: (0, 0))],
        out_specs=[pl.BlockSpec((1, HEADS, tm, HEAD_DIM), lambda i, m, n: (i, n, m, 0)),
                   pl.BlockSpec((1, tm, LANES), lambda i, m, n: (i, m, 0))],
        out_shape=[jax.ShapeDtypeStruct((b, N_PROJ_HEADS, s, HEAD_DIM), BF16),
                   jax.ShapeDtypeStruct((b, s, LANES), F32)],
        scratch_shapes=[pltpu.VMEM((tm, d), BF16)],
        compiler_params=_cparams(3),
        name="in_proj",
    )(x, g.reshape(1, d), sc.reshape(b, 1, d), sh.reshape(b, 1, d), w_heads, w_gate)


def _bucket_thresholds():
    d = np.arange(REL_MAX_DIST + 1)
    max_exact = REL_BUCKETS // 2
    large = max_exact + (np.log(np.maximum(d, 1).astype(np.float32) / np.float32(max_exact))
                         / np.float32(math.log(REL_MAX_DIST / max_exact))
                         * np.float32(REL_BUCKETS - max_exact)).astype(np.int32)
    bucket = np.where(d < max_exact, d, np.minimum(large, REL_BUCKETS - 1))
    return [int(np.argmax(bucket >= b)) for b in range(REL_BUCKETS)]


def _bias_kernel(off_ref, cmin_ref, tab_ref, o_ref, *, max_dist, dist_scale, thresholds):
    h = pl.program_id(0)
    t = pl.program_id(1)
    shape = o_ref.shape[-2:]
    rel = off_ref[t] + lax.broadcasted_iota(jnp.int32, shape, 0) - lax.broadcasted_iota(jnp.int32, shape, 1)
    acc = jnp.full(shape, tab_ref[h, 0], F32)
    for b in range(1, REL_BUCKETS):
        acc = jnp.where(rel >= -(-thresholds[b] // dist_scale), tab_ref[h, b], acc)
    col = lax.broadcasted_iota(jnp.int32, shape, 1)
    valid = (rel >= 0) & (rel <= max_dist) & (col >= cmin_ref[t])
    o_ref[...] = jnp.where(valid, acc, NEG_INF).reshape(o_ref.shape)


def _bias_tiles(tab, offsets, col_mins, rows, cols, max_dist, dist_scale, group):
    n_heads = tab.shape[0]
    n_tiles = len(offsets)
    smem = pl.BlockSpec(memory_space=pltpu.SMEM)
    out = pl.pallas_call(
        functools.partial(_bias_kernel, max_dist=max_dist, dist_scale=dist_scale,
                          thresholds=_bucket_thresholds()),
        grid=(n_heads, n_tiles),
        in_specs=[smem, smem, smem],
        out_specs=pl.BlockSpec((1, 1, 1, rows, cols), lambda h, t: (h // group, t, h % group, 0, 0)),
        out_shape=jax.ShapeDtypeStruct((n_heads // group, n_tiles, group, rows, cols), F32),
        compiler_params=_cparams(2),
        name="bias_tiles",
    )(jnp.asarray(offsets, jnp.int32), jnp.asarray(col_mins, jnp.int32), tab.astype(F32))
    return out.reshape(n_heads // group, n_tiles, group * rows, cols)


BAND_TILE = 128
BAND_STEP = 512


def _gate_column(gates, idx):
    col = lax.broadcasted_iota(jnp.int32, gates.shape, 1)
    return jnp.sum(jnp.where(col == idx, gates, 0.0), axis=-1, keepdims=True)


def _banded_kernel(*refs, units, r_heads, nsub, w, has_sink, gate_branch):
    q_ref, kp_ref, kc_ref, vp_ref, vc_ref, b_ref = refs[:6]
    pos = 6
    sink_ref = gate_ref = None
    if has_sink:
        sink_ref = refs[pos]
        pos += 1
    if gate_branch is not None:
        gate_ref = refs[pos]
        pos += 1
    o_ref, kw_scr, vw_scr = refs[pos:pos + 3]
    t = pl.program_id(2)
    tq = BAND_TILE
    rows = r_heads * tq
    nk = w + tq
    kw_scr[:, :w] = kp_ref[0]
    kw_scr[:, w:] = kc_ref[0]
    vw_scr[:, :w] = vp_ref[0]
    vw_scr[:, w:] = vc_ref[0]
    for u in range(units):
        for j in range(nsub):
            q = q_ref[0, u * r_heads:(u + 1) * r_heads, j * tq:(j + 1) * tq, :].reshape(rows, HEAD_DIM) * SCALE
            k = kw_scr[u, j * tq:j * tq + nk, :]
            v = vw_scr[u, j * tq:j * tq + nk, :]
            tile = jnp.minimum(t * nsub + j, w // tq)
            s = lax.dot_general(q, k, (((1,), (1,)), ((), ())), preferred_element_type=F32) + b_ref[u, tile]
            m = jnp.max(s, axis=-1, keepdims=True)
            if has_sink:
                m = jnp.maximum(m, sink_ref[u])
            e = jnp.exp(s - m)
            den = jnp.sum(e, axis=-1, keepdims=True)
            if has_sink:
                den = den + jnp.exp(sink_ref[u] - m)
            den = jnp.maximum(den, TINY)
            o = jnp.dot(e.astype(BF16), v, preferred_element_type=F32) / den
            parts = []
            for g in range(r_heads):
                og = o[g * tq:(g + 1) * tq]
                if gate_ref is not None:
                    head = (pl.program_id(1) * units + u) * r_heads + g
                    og = og * _gate_column(gate_ref[0, j * tq:(j + 1) * tq, :], head * 3 + gate_branch)
                parts.append(og)
            lanes = slice(u * r_heads * HEAD_DIM, (u + 1) * r_heads * HEAD_DIM)
            o_ref[0, j * tq:(j + 1) * tq, lanes] = jnp.concatenate(parts, axis=-1)


def _band_bias_tiles(tab, max_dist, w, dist_scale, group):
    tq = BAND_TILE
    n_early = w // tq
    return _bias_tiles(tab, [w] * (n_early + 1), [w - e * tq for e in range(n_early)] + [0], tq, w + tq,
                       max_dist, dist_scale, group)


def _banded_attention(q_arr, k_arr, v_arr, bias, *, grid01, q_map, kv_maps, bias_map, r_heads, units, w,
                      out_heads, sink=None, gates=None, gate_branch=None, name):
    seq = q_arr.shape[2]
    tq = BAND_TILE
    step = min(BAND_STEP, seq)
    nsub = step // tq
    assert seq % step == 0 and w % tq == 0 and step % w == 0
    k_map, v_map = kv_maps
    prev = lambda t: jnp.maximum(t * (step // w) - 1, 0)
    in_specs = [pl.BlockSpec((1, units * r_heads, step, HEAD_DIM), lambda i, j, t: q_map(i, j) + (t, 0)),
                pl.BlockSpec((1, units, w, HEAD_DIM), lambda i, j, t: k_map(i, j) + (prev(t), 0)),
                pl.BlockSpec((1, units, step, HEAD_DIM), lambda i, j, t: k_map(i, j) + (t, 0)),
                pl.BlockSpec((1, units, w, HEAD_DIM), lambda i, j, t: v_map(i, j) + (prev(t), 0)),
                pl.BlockSpec((1, units, step, HEAD_DIM), lambda i, j, t: v_map(i, j) + (t, 0)),
                pl.BlockSpec((units, w // tq + 1, r_heads * tq, w + tq), lambda i, j, t: (bias_map(i, j), 0, 0, 0))]
    args = [q_arr, k_arr, k_arr, v_arr, v_arr, bias]
    if sink is not None:
        in_specs.append(pl.BlockSpec((units, r_heads * tq, 1), lambda i, j, t: (bias_map(i, j), 0, 0)))
        args.append(sink)
    if gates is not None:
        in_specs.append(pl.BlockSpec((1, step, LANES), lambda i, j, t: (i, t, 0)))
        args.append(gates)
    n0, n1 = grid01
    return pl.pallas_call(
        functools.partial(_banded_kernel, units=units, r_heads=r_heads, nsub=nsub, w=w,
                          has_sink=sink is not None, gate_branch=gate_branch if gates is not None else None),
        grid=(n0, n1, seq // step),
        in_specs=in_specs,
        out_specs=pl.BlockSpec((1, step, units * r_heads * HEAD_DIM), lambda i, j, t: (i, t, j)),
        out_shape=jax.ShapeDtypeStruct((n0, seq, out_heads * HEAD_DIM), F32),
        scratch_shapes=[pltpu.VMEM((units, w + step, HEAD_DIM), BF16),
                        pltpu.VMEM((units, w + step, HEAD_DIM), BF16)],
        compiler_params=_cparams(3),
        name=name,
    )(*args)


DILATED_WINDOW = 128


def _dilated_kernel(q_ref, k_ref, v_ref, b_ref, o_ref, q_scr, kv_scr, o_scr, l_scr, *, seq, pad):
    tq = BAND_TILE
    w = DILATED_WINDOW
    nt = (((1,), (1,)), ((), ()))
    zeros = jnp.zeros((seq, HEAD_DIM), F32)
    q_scr[...] = jnp.concatenate([q_ref[0, 0].astype(F32) * SCALE, zeros], axis=-1)
    kv_scr[:pad] = jnp.zeros((pad, 2 * HEAD_DIM), F32)
    kv_scr[pad:] = jnp.concatenate([k_ref[0, 0].astype(F32), v_ref[0, 0].astype(F32)], axis=-1)

    def rows(start, n, r):
        return pl.ds(start, n) if r == 1 else pl.ds(start, n, stride=r)

    def tile(cfg, r, j, t):
        first = j + r * (t * tq)
        q = q_scr[rows(first, tq, r), :].astype(BF16)
        kv = kv_scr[rows(pad + first - r * w, w + tq, r), :].astype(BF16)
        s = lax.dot_general(q, kv, nt, preferred_element_type=F32) + b_ref[0, cfg, jnp.minimum(t, 1)]
        m = jnp.max(s, axis=-1, keepdims=True)
        e = jnp.exp(s - m)
        den = jnp.maximum(jnp.sum(e, axis=-1, keepdims=True), TINY)
        o_scr[cfg, rows(first, tq, r), :] = jnp.dot(e.astype(BF16), kv, preferred_element_type=F32) / den
        l_scr[cfg, rows(first, tq, r), :] = jnp.broadcast_to(m + jnp.log(den), (tq, 2 * HEAD_DIM))

    group = 8
    for cfg, (_, r) in enumerate(DILATED_CONFIGS):
        nsub = seq // r // tq

        def body(g, c, cfg=cfg, r=r, nsub=nsub):
            for u in range(group):
                tid = g * group + u
                tile(cfg, r, tid // nsub, tid % nsub)
            return c

        lax.fori_loop(0, r * nsub // group, body, 0)

    chunk = 512
    for c in range(seq // chunk):
        sl = slice(c * chunk, (c + 1) * chunk)
        ls = [l_scr[i, sl] for i in range(len(DILATED_CONFIGS))]
        mx = functools.reduce(jnp.maximum, ls)
        ws = [jnp.exp(l - mx) for l in ls]
        mixed = sum(wi * o_scr[i, sl] for i, wi in enumerate(ws)) / sum(ws)
        o_ref[0, 0, sl, :] = mixed[:, HEAD_DIM:]


def _dilated_attention(p, bias):
    b, _, s, _ = p.shape
    pad = max(r for _, r in DILATED_CONFIGS) * DILATED_WINDOW
    assert all(win // r == DILATED_WINDOW and (s // r) % BAND_TILE == 0 for win, r in DILATED_CONFIGS)
    head_spec = lambda base: pl.BlockSpec((1, 1, s, HEAD_DIM), lambda i, h: (i, base + h, 0, 0))
    n_cfg = len(DILATED_CONFIGS)
    return pl.pallas_call(
        functools.partial(_dilated_kernel, seq=s, pad=pad),
        grid=(b, HEADS),
        in_specs=[head_spec(QA), head_spec(KA), head_spec(VA),
                  pl.BlockSpec((1, n_cfg, 2, BAND_TILE, DILATED_WINDOW + BAND_TILE), lambda i, h: (h, 0, 0, 0, 0))],
        out_specs=pl.BlockSpec((1, 1, s, HEAD_DIM), lambda i, h: (i, h, 0, 0)),
        out_shape=jax.ShapeDtypeStruct((b, HEADS, s, HEAD_DIM), F32),
        scratch_shapes=[pltpu.VMEM((s, 2 * HEAD_DIM), F32),
                        pltpu.VMEM((pad + s, 2 * HEAD_DIM), F32),
                        pltpu.VMEM((n_cfg, s, 2 * HEAD_DIM), F32),
                        pltpu.VMEM((n_cfg, s, 2 * HEAD_DIM), F32)],
        compiler_params=_cparams(2),
        name="dilated",
    )(p, p, p, bias)


def _ranks_along_rows(vals):
    n, t = vals.shape
    groups = [vals[g * SUBLANES:(g + 1) * SUBLANES] for g in range(n // SUBLANES)]
    row = lax.broadcasted_iota(jnp.int32, (SUBLANES, t), 0)
    later = [jnp.where(row > r, 1, 0) for r in range(SUBLANES)]
    cnt = [jnp.zeros((SUBLANES, t), jnp.int32) for _ in groups]
    for i in range(n):
        vi = vals[i:i + 1]
        for g, vg in enumerate(groups):
            if g > i // SUBLANES:
                cnt[g] = cnt[g] + jnp.where(vi >= vg, 1, 0)
            elif g < i // SUBLANES:
                cnt[g] = cnt[g] + jnp.where(vi > vg, 1, 0)
            else:
                cnt[g] = cnt[g] + jnp.where(vi > vg, 1, 0) + jnp.where(vi == vg, later[i % SUBLANES], 0)
    return jnp.concatenate(cnt, axis=0)


def _rows_to_lanes(x_t, width):
    n, t = x_t.shape
    padded = jnp.concatenate([x_t, jnp.zeros((LANES - n, t), x_t.dtype)], axis=0)
    return padded.T[:, :width]


def _tile_groups(n, fn):
    def body(j, c):
        fn(j * 4, 4)
        return c

    lax.fori_loop(0, n // 4, body, 0)
    base = (n // 4) * 4

    @pl.when((n & 2) != 0)
    def _():
        fn(base, 2)

    @pl.when((n & 1) != 0)
    def _():
        fn(base + (n & 2), 1)


def _flash_selected(qa_scr, qi, kaug_scr, kv_of, v_ref, b_ref, s_scr, m_scr, l_scr, acc_scr, tq):
    nh = qa_scr.shape[0]
    half = tq // 2
    n = qi + 1
    nt = (((1,), (1,)), ((), ()))
    m_scr[...] = jnp.full(m_scr.shape, NEG_INF, F32)

    def logits(kj0, cnt):
        ks = pl.multiple_of(kj0 * tq, tq)
        for h in range(nh):
            s = lax.dot_general(qa_scr[h], kaug_scr[kv_of[h], pl.ds(ks, cnt * tq), :], nt,
                                preferred_element_type=F32)
            mloc = None
            for u in range(cnt):
                su = s[:, u * tq:(u + 1) * tq] + b_ref[h, jnp.minimum(qi - kj0 - u, b_ref.shape[1] - 1)]
                s_scr[h, kj0 + u] = su
                mu = jnp.maximum(su[:, :half], su[:, half:])
                mloc = mu if mloc is None else jnp.maximum(mloc, mu)
            m_scr[h] = jnp.maximum(m_scr[h], mloc)

    _tile_groups(n, logits)
    for h in range(nh):
        m_scr[h] = jnp.broadcast_to(jnp.max(m_scr[h], axis=-1, keepdims=True), (tq, half))
    l_scr[...] = jnp.zeros(l_scr.shape, F32)
    acc_scr[...] = jnp.zeros(acc_scr.shape, F32)

    def weigh(kj0, cnt):
        ks = pl.multiple_of(kj0 * tq, tq)
        for h in range(nh):
            mb = m_scr[h]
            parts = []
            lloc = None
            for u in range(cnt):
                su = s_scr[h, kj0 + u]
                e_lo = jnp.exp(su[:, :half] - mb)
                e_hi = jnp.exp(su[:, half:] - mb)
                lu = e_lo + e_hi
                lloc = lu if lloc is None else lloc + lu
                parts += [e_lo.astype(BF16), e_hi.astype(BF16)]
            acc_scr[h] += jnp.dot(jnp.concatenate(parts, axis=-1), v_ref[0, kv_of[h], pl.ds(ks, cnt * tq), :],
                                  preferred_element_type=F32)
            l_scr[h] += lloc

    _tile_groups(n, weigh)
    return [acc_scr[h] / jnp.maximum(jnp.sum(l_scr[h], axis=-1, keepdims=True), TINY) for h in range(nh)]


def _fill_kaug(kaug_scr, k_ref, et_ref):
    for kv in range(kaug_scr.shape[0]):
        kaug_scr[kv, :, :HEAD_DIM] = k_ref[0, kv]
        kaug_scr[kv, :, HEAD_DIM:] = et_ref[...]


def _augment(q, pen):
    return jnp.concatenate([q * SCALE, pen.astype(BF16)], axis=-1)


def _moba_kernel(q_ref, k_ref, v_ref, pool_ref, et_ref, b_ref, o_ref, kaug_scr, kmean_scr, qa_scr, s_scr, m_scr,
                 l_scr, acc_scr, *, tq, nblk, nh):
    qi = pl.program_id(2)

    @pl.when(qi == 0)
    def _():
        _fill_kaug(kaug_scr, k_ref, et_ref)
        for h in range(nh):
            kmean_scr[h] = jnp.dot(pool_ref[...], k_ref[0, h], preferred_element_type=F32).astype(BF16)

    for h in range(nh):
        q = q_ref[0, h]
        gate = lax.dot_general(kmean_scr[h], q, (((1,), (1,)), ((), ())), preferred_element_type=F32)
        blk = lax.broadcasted_iota(jnp.int32, (nblk, tq), 0)
        gate = jnp.where(blk < qi, gate, NEG_INF)
        picked = ((_ranks_along_rows(gate) < MOBA_TOPK) & (blk < qi)) | (blk == qi)
        pen_t = jnp.where(picked, 0.0, NEG_INF)
        qa_scr[h] = _augment(q, _rows_to_lanes(pen_t, HEAD_DIM))
    outs = _flash_selected(qa_scr, qi, kaug_scr, tuple(range(nh)), v_ref, b_ref, s_scr, m_scr, l_scr, acc_scr, tq)
    o_ref[0] = jnp.concatenate(outs, axis=-1)


def _slc_kernel(q_ref, k_ref, v_ref, pen_ref, gate_ref, et_ref, b_ref, o_ref, kaug_scr, qa_scr, s_scr, m_scr,
                l_scr, acc_scr, *, tq, nh):
    qi = pl.program_id(2)

    @pl.when(qi == 0)
    def _():
        _fill_kaug(kaug_scr, k_ref, et_ref)

    for h in range(nh):
        qa_scr[h] = _augment(q_ref[0, h], pen_ref[0, 0])
    outs = _flash_selected(qa_scr, qi, kaug_scr, (0,) * nh, v_ref, b_ref, s_scr, m_scr, l_scr, acc_scr, tq)
    gates = gate_ref[0]
    first = pl.program_id(1) * nh
    o_ref[0] = jnp.concatenate([outs[h] * _gate_column(gates, (first + h) * 3 + 1) for h in range(nh)], axis=-1)


def _flash_state(nh, n_tiles, tq):
    return [pltpu.VMEM((nh, tq, 2 * HEAD_DIM), BF16),
            pltpu.VMEM((nh, n_tiles, tq, tq), F32),
            pltpu.VMEM((nh, tq, tq // 2), F32),
            pltpu.VMEM((nh, tq, tq // 2), F32),
            pltpu.VMEM((nh, tq, HEAD_DIM), F32)]


FLASH_TILE = 256
MOBA_HEADS_PER_STEP = 4
SLC_HEADS_PER_STEP = GQA


def _flash_bias_tiles(tab, seq):
    tq = FLASH_TILE
    first_const = -(-(_bucket_thresholds()[-1] + tq - 1) // tq)
    n_off = min(seq // tq, first_const + 1)
    return _bias_tiles(tab, [o * tq for o in range(n_off)], [0] * n_off, tq, tq, seq, 1, 1)


def _block_onehot_t(nblk, seq):
    blk = seq // nblk
    return jnp.asarray((np.arange(seq)[:, None] // blk == np.arange(HEAD_DIM)[None, :]).astype(np.float32), BF16)


def _moba_attention(p, bias):
    b, _, s, _ = p.shape
    tq, nh = FLASH_TILE, MOBA_HEADS_PER_STEP
    assert tq == MOBA_BLOCK
    nblk = s // tq
    nrow = max(nblk, 16)
    n_bias = bias.shape[1]
    pool = jnp.asarray((np.arange(s)[None, :] // tq == np.arange(nrow)[:, None]).astype(np.float32) / tq, BF16)
    once = pl.Buffered(1)
    kv_spec = lambda base: pl.BlockSpec((1, nh, s, HEAD_DIM), lambda i, h, t: (i, base // nh + h, 0, 0),
                                        pipeline_mode=once)
    return pl.pallas_call(
        functools.partial(_moba_kernel, tq=tq, nblk=nrow, nh=nh),
        grid=(b, HEADS // nh, s // tq),
        in_specs=[pl.BlockSpec((1, nh, tq, HEAD_DIM), lambda i, h, t: (i, QC // nh + h, t, 0)),
                  kv_spec(KC), kv_spec(VC),
                  pl.BlockSpec((nrow, s), lambda i, h, t: (0, 0), pipeline_mode=once),
                  pl.BlockSpec((s, HEAD_DIM), lambda i, h, t: (0, 0), pipeline_mode=once),
                  pl.BlockSpec((nh, n_bias, tq, tq), lambda i, h, t: (h, 0, 0, 0), pipeline_mode=once)],
        out_specs=pl.BlockSpec((1, tq, nh * HEAD_DIM), lambda i, h, t: (i, t, h)),
        out_shape=jax.ShapeDtypeStruct((b, s, HEADS * HEAD_DIM), F32),
        scratch_shapes=[pltpu.VMEM((nh, s, 2 * HEAD_DIM), BF16),
                        pltpu.VMEM((nh, nrow, HEAD_DIM), BF16)] + _flash_state(nh, s // tq, tq),
        compiler_params=_cparams(3),
        name="moba",
    )(p, p, p, pool, _block_onehot_t(nblk, s), bias)


def _nsa_selected_attention(p, pen, gates, bias):
    b, _, s, _ = p.shape
    tq, nh = FLASH_TILE, SLC_HEADS_PER_STEP
    nblk = s // NSA_SLC_BLOCK
    n_bias = bias.shape[1]
    assert nblk <= HEAD_DIM and GQA % nh == 0
    kv_spec = lambda base: pl.BlockSpec((1, 1, s, HEAD_DIM), lambda i, h, t: (i, base + (h * nh) // GQA, 0, 0))
    return pl.pallas_call(
        functools.partial(_slc_kernel, tq=tq, nh=nh),
        grid=(b, HEADS // nh, s // tq),
        in_specs=[pl.BlockSpec((1, nh, tq, HEAD_DIM), lambda i, h, t: (i, QD // nh + h, t, 0)),
                  kv_spec(KDS), kv_spec(VDS),
                  pl.BlockSpec((1, 1, tq, HEAD_DIM), lambda i, h, t: (i, (h * nh) // GQA, t, 0)),
                  pl.BlockSpec((1, tq, LANES), lambda i, h, t: (i, t, 0)),
                  pl.BlockSpec((s, HEAD_DIM), lambda i, h, t: (0, 0)),
                  pl.BlockSpec((nh, n_bias, tq, tq), lambda i, h, t: (h, 0, 0, 0))],
        out_specs=pl.BlockSpec((1, tq, nh * HEAD_DIM), lambda i, h, t: (i, t, h)),
        out_shape=jax.ShapeDtypeStruct((b, s, HEADS * HEAD_DIM), F32),
        scratch_shapes=[pltpu.VMEM((1, s, 2 * HEAD_DIM), BF16)] + _flash_state(nh, s // tq, tq),
        compiler_params=_cparams(3),
        name="nsa_selected",
    )(p, p, p, pen, gates, _block_onehot_t(nblk, s), bias)


def _compress_kernel(t_ref, pe_ref, w1_ref, w2_ref, o_ref):
    t = t_ref[0, 0].astype(F32)
    half = t.shape[-1]
    top = (t + pe_ref[0, 0:1, :]).astype(BF16)
    bot = (t + pe_ref[0, 1:2, :]).astype(BF16)
    u = jnp.dot(top, w1_ref[0, :half, :], preferred_element_type=F32)
    vv = jnp.dot(bot, w1_ref[0, half:, :], preferred_element_type=F32)
    pre = u + pltpu.roll(vv, vv.shape[0] - 1, axis=0)
    hid = jax.nn.gelu(pre).astype(BF16)
    o_ref[0, 0] = jnp.dot(hid, w2_ref[0], preferred_element_type=F32).astype(BF16)


def _nsa_compress(p, pe, w1, w2):
    b, _, s, _ = p.shape
    n_half = s // NSA_CMP_STRIDE
    half = NSA_CMP_STRIDE * HEAD_DIM
    t = p[:, KDC:KDC + 2 * KV_HEADS].reshape(b, 2 * KV_HEADS, n_half, half)
    pe2 = pe.reshape(2, 2, half)
    return pl.pallas_call(
        _compress_kernel,
        grid=(b, 2 * KV_HEADS),
        in_specs=[pl.BlockSpec((1, 1, n_half, half), lambda i, j: (i, j, 0, 0)),
                  pl.BlockSpec((1, 2, half), lambda i, j: (j // KV_HEADS, 0, 0)),
                  pl.BlockSpec((1, 2 * half, NSA_CMP_HIDDEN), lambda i, j: (j // KV_HEADS, 0, 0)),
                  pl.BlockSpec((1, NSA_CMP_HIDDEN, HEAD_DIM), lambda i, j: (j // KV_HEADS, 0, 0))],
        out_specs=pl.BlockSpec((1, 1, n_half, HEAD_DIM), lambda i, j: (i, j, 0, 0)),
        out_shape=jax.ShapeDtypeStruct((b, 2 * KV_HEADS, n_half, HEAD_DIM), BF16),
        compiler_params=_cparams(2),
        name="nsa_compress",
    )(t, pe2, w1.astype(BF16), w2.astype(BF16))


def _cmp_attn_kernel(q_ref, kc_ref, vc_ref, ov_ref, gate_ref, o_ref, pen_ref, *, tq, n_slc):
    qi = pl.program_id(2)
    rows = GQA * tq
    q = q_ref[0].reshape(rows, HEAD_DIM)
    kc = kc_ref[0, 0]
    n_cmp = kc.shape[0]
    s = lax.dot_general(q, kc, (((1,), (1,)), ((), ())), preferred_element_type=F32) * SCALE
    row = lax.broadcasted_iota(jnp.int32, (rows, n_cmp), 0)
    pos = qi * tq + (row & (tq - 1))
    blk_end = lax.broadcasted_iota(jnp.int32, (rows, n_cmp), 1) * NSA_CMP_STRIDE + (NSA_CMP_BLOCK - 1)
    mask = blk_end <= pos
    s = jnp.where(mask, s, NEG_INF)
    m = jnp.max(s, axis=-1, keepdims=True)
    e = jnp.where(mask, jnp.exp(s - m), 0.0)
    den = jnp.maximum(jnp.sum(e, axis=-1, keepdims=True), TINY)
    pr = e / den
    o = jnp.dot(pr.astype(BF16), vc_ref[0, 0], preferred_element_type=F32)
    gates = gate_ref[0]
    first = pl.program_id(1) * GQA
    o_ref[0] = jnp.concatenate([o[g * tq:(g + 1) * tq] * _gate_column(gates, (first + g) * 3)
                                for g in range(GQA)], axis=-1)
    psum = pr[0:tq]
    for g in range(1, GQA):
        psum = psum + pr[g * tq:(g + 1) * tq]
    ov_t = ov_ref[...]
    hi = psum.astype(BF16)
    r1 = psum - hi.astype(F32)
    mid = r1.astype(BF16)
    lo = (r1 - mid.astype(F32)).astype(BF16)
    nt = (((1,), (1,)), ((), ()))
    imp_t = (lax.dot_general(ov_t, hi, nt, preferred_element_type=F32)
             + lax.dot_general(ov_t, mid, nt, preferred_element_type=F32)
             + lax.dot_general(ov_t, lo, nt, preferred_element_type=F32))
    jj = lax.broadcasted_iota(jnp.int32, (n_slc, tq), 0)
    cur = (qi * tq + lax.broadcasted_iota(jnp.int32, (n_slc, tq), 1)) // NSA_SLC_BLOCK
    forced = (jj == 0) | (jj == cur) | (jj == cur - 1)
    imp_t = jnp.where(forced, -NEG_INF, jnp.where(jj <= cur, imp_t, NEG_INF))
    picked = (_ranks_along_rows(imp_t) < NSA_SLC_TOPN) & (jj <= cur)
    pen_ref[0, 0] = _rows_to_lanes(jnp.where(picked, 0.0, NEG_INF), HEAD_DIM).astype(BF16)


def _nsa_compressed_attention(p, kvc, gates):
    b, _, s, _ = p.shape
    tq = 256
    n_cmp = kvc.shape[2]
    n_slc = s // NSA_SLC_BLOCK
    assert n_slc == HEAD_DIM
    c_start = np.arange(n_cmp) * NSA_CMP_STRIDE
    j_start = np.arange(n_slc) * NSA_SLC_BLOCK
    overlap = ((c_start[:, None] < j_start[None, :] + NSA_SLC_BLOCK)
               & (c_start[:, None] + NSA_CMP_BLOCK > j_start[None, :])).astype(np.float32)
    overlap[n_cmp - 1] = 0.0
    return pl.pallas_call(
        functools.partial(_cmp_attn_kernel, tq=tq, n_slc=n_slc),
        grid=(b, KV_HEADS, s // tq),
        in_specs=[pl.BlockSpec((1, GQA, tq, HEAD_DIM), lambda i, h, t: (i, QD // GQA + h, t, 0)),
                  pl.BlockSpec((1, 1, n_cmp, HEAD_DIM), lambda i, h, t: (i, h, 0, 0)),
                  pl.BlockSpec((1, 1, n_cmp, HEAD_DIM), lambda i, h, t: (i, KV_HEADS + h, 0, 0)),
                  pl.BlockSpec((n_slc, n_cmp), lambda i, h, t: (0, 0)),
                  pl.BlockSpec((1, tq, LANES), lambda i, h, t: (i, t, 0))],
        out_specs=[pl.BlockSpec((1, tq, GQA * HEAD_DIM), lambda i, h, t: (i, t, h)),
                   pl.BlockSpec((1, 1, tq, HEAD_DIM), lambda i, h, t: (i, h, t, 0))],
        out_shape=[jax.ShapeDtypeStruct((b, s, HEADS * HEAD_DIM), F32),
                   jax.ShapeDtypeStruct((b, KV_HEADS, s, HEAD_DIM), BF16)],
        compiler_params=_cparams(3),
        name="nsa_compressed",
    )(p, kvc, kvc, jnp.asarray(overlap.T, BF16), gates)


def _group_norm(o, g):
    ms = jnp.mean(o * o, axis=-1, keepdims=True)
    return (o * lax.rsqrt(ms + NORM_EPS)) * g


def _route(logits_t, bias_t):
    aff = jax.nn.sigmoid(logits_t)
    score = aff + bias_t
    rows = [score[e:e + 1] for e in range(N_EXPERTS)]
    affs = [aff[e:e + 1] for e in range(N_EXPERTS)]
    gscore = []
    for g in range(N_EXPERT_GROUPS):
        mem = rows[g * PER_GROUP:(g + 1) * PER_GROUP]
        best = None
        for a in range(PER_GROUP):
            for c in range(a + 1, PER_GROUP):
                pair = mem[a] + mem[c]
                best = pair if best is None else jnp.maximum(best, pair)
        gscore.append(best)
    gsel = jnp.zeros_like(gscore[0], dtype=jnp.int32)
    gbest = gscore[0]
    for g in range(1, N_EXPERT_GROUPS):
        better = gscore[g] > gbest
        gsel = jnp.where(better, g, gsel)
        gbest = jnp.where(better, gscore[g], gbest)
    masked = [jnp.where(gsel == e // PER_GROUP, rows[e], NEG_INF) for e in range(N_EXPERTS)]

    def argbest(vals):
        idx = jnp.zeros_like(gsel)
        best = vals[0]
        for e in range(1, N_EXPERTS):
            better = vals[e] > best
            idx = jnp.where(better, e, idx)
            best = jnp.where(better, vals[e], best)
        return idx

    e1 = argbest(masked)
    e2 = argbest([jnp.where(e1 == e, -jnp.inf, masked[e]) for e in range(N_EXPERTS)])
    w1 = sum(jnp.where(e1 == e, affs[e], 0.0) for e in range(N_EXPERTS))
    w2 = sum(jnp.where(e2 == e, affs[e], 0.0) for e in range(N_EXPERTS))
    tot = w1 + w2
    w1 = w1 / tot
    w2 = w2 / tot
    comb = jnp.concatenate([jnp.where(e1 == e, w1, 0.0) + jnp.where(e2 == e, w2, 0.0)
                            for e in range(N_EXPERTS)], axis=0)
    return comb, gsel


def _out_kernel(x_ref, oa_ref, ob_ref, oc_ref, dc_ref, ds_ref, dw_ref,
                mixg_ref, wout_ref, g1_ref, n2g_ref, sc2_ref, sh2_ref, rw_ref, rb_ref,
                xo_ref, h2_ref, gsel_ref):
    o_a = jnp.concatenate([oa_ref[0, h] for h in range(HEADS)], axis=-1)
    o_b = ob_ref[0]
    o_c = oc_ref[0]
    o_d = dc_ref[0] + ds_ref[0] + dw_ref[0]
    mixed = jnp.concatenate([_group_norm(o, mixg_ref[i:i + 1, :]).astype(BF16)
                             for i, o in enumerate((o_a, o_b, o_c, o_d))], axis=-1)
    y = jnp.dot(mixed, wout_ref[...], preferred_element_type=F32)
    x = x_ref[0] + g1_ref[0] * y
    xo_ref[0] = x
    ms = jnp.mean(x * x, axis=-1, keepdims=True)
    h2 = (x * lax.rsqrt(ms + NORM_EPS)) * n2g_ref[...] * (1.0 + sc2_ref[0]) + sh2_ref[0]
    logits_t = lax.dot_general(rw_ref[...], h2.astype(BF16), (((1,), (1,)), ((), ())),
                               preferred_element_type=F32)
    comb_t, gsel = _route(logits_t, rb_ref[...])
    h2_ref[0] = jnp.concatenate([h2, _rows_to_lanes(comb_t, LANES)], axis=-1)
    gsel_ref[0] = gsel


def _mixer_out(x, o_a, o_b, o_c, o_dc, o_ds, o_dw, mix_g, w_out, g1, n2g, sc2, sh2, rw_t, rb):
    b, s, d = x.shape
    tm = min(256, s)
    head_spec = pl.BlockSpec((1, HEADS, tm, HEAD_DIM), lambda i, m: (i, 0, m, 0))
    mixer_spec = pl.BlockSpec((1, tm, GROUP_WIDTH), lambda i, m: (i, m, 0))
    vec_b = pl.BlockSpec((1, 1, d), lambda i, m: (i, 0, 0))
    return pl.pallas_call(
        _out_kernel,
        grid=(b, s // tm),
        in_specs=[pl.BlockSpec((1, tm, d), lambda i, m: (i, m, 0)),
                  head_spec, mixer_spec, mixer_spec, mixer_spec, mixer_spec, mixer_spec,
                  pl.BlockSpec((N_MIXERS, GROUP_WIDTH), lambda i, m: (0, 0)),
                  pl.BlockSpec((N_MIXERS * GROUP_WIDTH, d), lambda i, m: (0, 0)),
                  vec_b,
                  pl.BlockSpec((1, d), lambda i, m: (0, 0)),
                  vec_b, vec_b,
                  pl.BlockSpec((N_EXPERTS, d), lambda i, m: (0, 0)),
                  pl.BlockSpec((N_EXPERTS, 1), lambda i, m: (0, 0))],
        out_specs=[pl.BlockSpec((1, tm, d), lambda i, m: (i, m, 0)),
                   pl.BlockSpec((1, tm, d + LANES), lambda i, m: (i, m, 0)),
                   pl.BlockSpec((1, 1, tm), lambda i, m: (i, 0, m))],
        out_shape=[jax.ShapeDtypeStruct((b, s, d), F32),
                   jax.ShapeDtypeStruct((b, s, d + LANES), F32),
                   jax.ShapeDtypeStruct((b, 1, s), jnp.int32)],
        compiler_params=_cparams(2),
        name="mixer_out",
    )(x, o_a, o_b, o_c, o_dc, o_ds, o_dw, mix_g, w_out, g1.reshape(b, 1, d), n2g.reshape(1, d),
      sc2.reshape(b, 1, d), sh2.reshape(b, 1, d), rw_t, rb.reshape(N_EXPERTS, 1))


EXPERT_TILE = 512


def _dispatch_plan(gsel, tm):
    gid = gsel.reshape(-1)
    n_tok = gid.shape[0]
    onehot = (gid[:, None] == jnp.arange(N_EXPERT_GROUPS, dtype=jnp.int32)[None, :]).astype(jnp.int32)
    csum = jnp.cumsum(onehot, axis=0)
    rank = jnp.sum((csum - onehot) * onehot, axis=1)
    padded = ((csum[-1] + tm - 1) // tm) * tm
    ends = jnp.cumsum(padded)
    dest = jnp.sum(onehot * (ends - padded)[None, :], axis=1) + rank
    n_rows = n_tok + N_EXPERT_GROUPS * tm
    src = jnp.zeros((n_rows,), jnp.int32).at[dest].set(jnp.arange(n_tok, dtype=jnp.int32), unique_indices=True)
    tile_start = jnp.arange(n_rows // tm, dtype=jnp.int32) * tm
    tile_gid = jnp.minimum(jnp.sum((tile_start[:, None] >= ends[None, :]).astype(jnp.int32), axis=1),
                           N_EXPERT_GROUPS - 1)
    return dest.astype(jnp.int32), src, tile_gid, (ends[-1] // tm).reshape(1).astype(jnp.int32)


def _start_row_gather(idx_ref, base, src_hbm, buf, sem):
    def issue(r, c):
        pltpu.make_async_copy(src_hbm.at[pl.ds(idx_ref[base + r], 1)], buf.at[pl.ds(r, 1)], sem).start()
        return c

    lax.fori_loop(0, buf.shape[0], issue, 0, unroll=8)


def _wait_row_gather(src_hbm, buf, sem):
    pltpu.make_async_copy(src_hbm.at[pl.ds(0, buf.shape[0])], buf, sem).wait()


def _expert_kernel(src_ref, gid_ref, nvalid_ref, h_hbm, wgu_ref, wd_ref, o_ref, rows_buf, sems, h_scr, acc_ref, *, tm):
    t = pl.program_id(0)
    e = pl.program_id(1)
    slot = t % 2
    n_valid = nvalid_ref[0]

    @pl.when(t < n_valid)
    def _():
        @pl.when(e == 0)
        def _():
            @pl.when(t == 0)
            def _():
                _start_row_gather(src_ref, 0, h_hbm, rows_buf.at[0], sems.at[0])

            _wait_row_gather(h_hbm, rows_buf.at[slot], sems.at[slot])

            @pl.when(t + 1 < n_valid)
            def _():
                _start_row_gather(src_ref, (t + 1) * tm, h_hbm, rows_buf.at[1 - slot], sems.at[1 - slot])

            h_scr[...] = rows_buf[slot, :, :D_MODEL].astype(BF16)
            acc_ref[...] = jnp.zeros_like(acc_ref)

        gu = jnp.dot(h_scr[...], wgu_ref[0], preferred_element_type=F32)
        gate = gu[:, :D_EXPERT]
        he = (gate * jax.nn.sigmoid(gate)) * gu[:, D_EXPERT:]
        cw = rows_buf[slot, :, D_MODEL:]
        col = lax.broadcasted_iota(jnp.int32, cw.shape, 1)
        w = jnp.sum(jnp.where(col == gid_ref[t] * PER_GROUP + e, cw, 0.0), axis=-1, keepdims=True)
        acc_ref[...] += w * jnp.dot(he.astype(BF16), wd_ref[0], preferred_element_type=F32)

        @pl.when(e == PER_GROUP - 1)
        def _():
            o_ref[...] = acc_ref[...]

    @pl.when((t >= n_valid) & (e == PER_GROUP - 1))
    def _():
        o_ref[...] = jnp.zeros_like(o_ref)


def _experts(h2ext, src, tile_gid, n_valid, w_gu, w_d, tm):
    n_rows = src.shape[0]
    d = D_MODEL
    expert = lambda t, e, src_ref, gid, nv: (gid[t] * PER_GROUP + e, 0, 0)
    return pl.pallas_call(
        functools.partial(_expert_kernel, tm=tm),
        grid_spec=pltpu.PrefetchScalarGridSpec(
            num_scalar_prefetch=3,
            grid=(n_rows // tm, PER_GROUP),
            in_specs=[pl.BlockSpec(memory_space=pl.ANY),
                      pl.BlockSpec((1, d, 2 * D_EXPERT), expert),
                      pl.BlockSpec((1, D_EXPERT, d), expert)],
            out_specs=pl.BlockSpec((tm, d), lambda t, e, src_ref, gid, nv: (t, 0)),
            scratch_shapes=[pltpu.VMEM((2, tm, d + LANES), F32), pltpu.SemaphoreType.DMA((2,)),
                            pltpu.VMEM((tm, d), BF16), pltpu.VMEM((tm, d), F32)]),
        out_shape=jax.ShapeDtypeStruct((n_rows, d), F32),
        compiler_params=_cparams(2),
        name="experts",
    )(src, tile_gid, n_valid, h2ext, w_gu, w_d)


COMBINE_TILE = 512


def _combine_kernel(dest_ref, ys_hbm, x_ref, g2_ref, fg_ref, o_ref, rows_buf, sems, *, tm, final_norm):
    i = pl.program_id(0)
    slot = i % 2

    @pl.when(i == 0)
    def _():
        _start_row_gather(dest_ref, 0, ys_hbm, rows_buf.at[0], sems.at[0])

    _wait_row_gather(ys_hbm, rows_buf.at[slot], sems.at[slot])

    @pl.when(i + 1 < pl.num_programs(0))
    def _():
        _start_row_gather(dest_ref, (i + 1) * tm, ys_hbm, rows_buf.at[1 - slot], sems.at[1 - slot])

    x = x_ref[0] + g2_ref[0] * rows_buf[slot]
    if final_norm:
        ms = jnp.mean(x * x, axis=-1, keepdims=True)
        x = (x * lax.rsqrt(ms + NORM_EPS)) * fg_ref[...]
    o_ref[0] = x


def _combine(x, ys, dest, g2, final_g, final_norm):
    b, s, d = x.shape
    tm = min(COMBINE_TILE, s)
    per_seq = s // tm
    return pl.pallas_call(
        functools.partial(_combine_kernel, tm=tm, final_norm=final_norm),
        grid_spec=pltpu.PrefetchScalarGridSpec(
            num_scalar_prefetch=1,
            grid=(b * per_seq,),
            in_specs=[pl.BlockSpec(memory_space=pl.ANY),
                      pl.BlockSpec((1, tm, d), lambda i, dest_ref: (i // per_seq, i % per_seq, 0)),
                      pl.BlockSpec((1, 1, d), lambda i, dest_ref: (i // per_seq, 0, 0)),
                      pl.BlockSpec((1, d), lambda i, dest_ref: (0, 0))],
            out_specs=pl.BlockSpec((1, tm, d), lambda i, dest_ref: (i // per_seq, i % per_seq, 0)),
            scratch_shapes=[pltpu.VMEM((2, tm, d), F32), pltpu.SemaphoreType.DMA((2,))]),
        out_shape=jax.ShapeDtypeStruct((b, s, d), F32),
        compiler_params=_cparams(1),
        name="moe_combine",
    )(dest, ys, x, g2.reshape(b, 1, d), final_g.reshape(1, d))


def _moe(x, h2ext, gsel, w_gu, w_d, g2, final_g, final_norm):
    b, s, _ = x.shape
    dest, src, tile_gid, n_valid = _dispatch_plan(gsel, EXPERT_TILE)
    ys = _experts(h2ext.reshape(b * s, -1), src, tile_gid, n_valid, w_gu, w_d, EXPERT_TILE)
    return _combine(x, ys, dest, g2, final_g, final_norm)


def _bias_tables(rel_bias, s):
    tabs = [rel_bias[:, i * HEADS:(i + 1) * HEADS].T for i in range(N_MIXERS)]
    t = {}
    t["a"] = jnp.stack([_band_bias_tiles(tabs[0], window // r, DILATED_WINDOW, r, 1)
                        for window, r in DILATED_CONFIGS], axis=1)
    t["b"] = _band_bias_tiles(tabs[1], SWA_WINDOW - 1, 128, 1, GQA)
    t["c"] = _flash_bias_tiles(tabs[2], s)
    t["ds"] = _flash_bias_tiles(tabs[3], s)
    t["dw"] = _band_bias_tiles(tabs[3], NSA_WINDOW - 1, NSA_WINDOW, 1, GQA)
    return t


def _token_mixer_branches(p, gates, bias, sink, pe, w1, w2):
    b, _, s, _ = p.shape
    o_a = _dilated_attention(p, bias["a"])
    o_b = _banded_attention(p, p, p, bias["b"], grid01=(b, 1),
                            q_map=lambda i, j: (i, QB // HEADS), kv_maps=(lambda i, j: (i, KB // KV_HEADS),
                                                                         lambda i, j: (i, VB // KV_HEADS)),
                            bias_map=lambda i, j: 0, r_heads=GQA, units=KV_HEADS, w=128, out_heads=HEADS,
                            sink=sink, name="swa")
    o_c = _moba_attention(p, bias["c"])
    kvc = _nsa_compress(p, pe, w1, w2)
    o_dc, pen = _nsa_compressed_attention(p, kvc, gates)
    o_ds = _nsa_selected_attention(p, pen, gates, bias["ds"])
    o_dw = _banded_attention(p, p, p, bias["dw"], grid01=(b, KV_HEADS),
                             q_map=lambda i, j: (i, QD // GQA + j), kv_maps=(lambda i, j: (i, KDW + j),
                                                                            lambda i, j: (i, VDW + j)),
                             bias_map=lambda i, j: j, r_heads=GQA, units=1, w=NSA_WINDOW, out_heads=HEADS,
                             gates=gates, gate_branch=2, name="nsa_window")
    return o_a, o_b, o_c, o_dc, o_ds, o_dw


def _split_w_in(w_in_l):
    w_heads = w_in_l[:, :PROJ_COLS].astype(BF16)
    w_gate = jnp.pad(w_in_l[:, PROJ_COLS:], ((0, 0), (0, LANES - N_GATE_COLS))).astype(BF16)
    return w_heads, w_gate


def _sink_rows(sink_l):
    return jnp.repeat(sink_l.astype(F32).reshape(KV_HEADS, GQA), BAND_TILE, axis=1)[..., None]


def kernel(x, c, rel_bias, router_w, router_bias, norm1_g, norm2_g, ada_w, ada_b, w_in, nsa_pe_k, nsa_pe_v,
           nsa_cmp_w1_k, nsa_cmp_w2_k, nsa_cmp_w1_v, nsa_cmp_w2_v, sinks, mix_norm_g, w_out, exp_w_gate,
           exp_w_up, exp_w_down, final_g):
    b, s, d = x.shape
    mod = _modulation(c, ada_w, ada_b)
    bias = _bias_tables(rel_bias, s)
    rw_t = router_w.T.astype(BF16)
    for l in range(DEPTH):
        sh1, sc1, g1, sh2, sc2, g2 = jnp.split(mod[l], 6, axis=-1)
        w_heads, w_gate = _split_w_in(w_in[l])
        p, gates = _in_projection(x, norm1_g[l], sc1, sh1, w_heads, w_gate)
        pe = jnp.stack([nsa_pe_k[l], nsa_pe_v[l]])
        w1 = jnp.stack([nsa_cmp_w1_k[l], nsa_cmp_w1_v[l]])
        w2 = jnp.stack([nsa_cmp_w2_k[l], nsa_cmp_w2_v[l]])
        o_a, o_b, o_c, o_dc, o_ds, o_dw = _token_mixer_branches(p, gates, bias, _sink_rows(sinks[l]), pe, w1, w2)
        x, h2ext, gsel = _mixer_out(x, o_a, o_b, o_c, o_dc, o_ds, o_dw, mix_norm_g[l],
                                    w_out[l].astype(BF16), g1, norm2_g[l], sc2, sh2, rw_t, router_bias)
        w_gu = jnp.concatenate([exp_w_gate[l], exp_w_up[l]], axis=-1).astype(BF16)
        x = _moe(x, h2ext, gsel, w_gu, exp_w_down[l].astype(BF16), g2, final_g, l == DEPTH - 1)
    return x
```

```python
import functools
import math

import numpy as np
import jax
import jax.numpy as jnp
from jax import lax
from jax.experimental import pallas as pl
from jax.experimental.pallas import tpu as pltpu

F32 = jnp.float32
BF16 = jnp.bfloat16

D_MODEL = 2048
DEPTH = 4
HEAD_DIM = 64
N_MIXERS = 4
HEADS = 8
GROUP_WIDTH = HEADS * HEAD_DIM
SCALE = HEAD_DIM ** -0.5
REL_BUCKETS = 32
REL_MAX_DIST = 2048
DILATED_CONFIGS = ((128, 1), (512, 4), (2048, 16))
SWA_WINDOW = 128
KV_HEADS = 2
GQA = HEADS // KV_HEADS
MOBA_BLOCK = 256
MOBA_TOPK = 3
NSA_CMP_BLOCK = 32
NSA_CMP_STRIDE = 16
NSA_CMP_HIDDEN = 128
NSA_SLC_BLOCK = 64
NSA_SLC_TOPN = 16
NSA_WINDOW = 512
N_EXPERTS = 16
N_EXPERT_GROUPS = 4
PER_GROUP = N_EXPERTS // N_EXPERT_GROUPS
D_EXPERT = 512
NORM_EPS = 1e-6
NEG_INF = -1e30
TINY = float(np.finfo(np.float32).tiny)

N_PROJ_HEADS = 80
PROJ_COLS = N_PROJ_HEADS * HEAD_DIM
N_GATE_COLS = HEADS * 3
LANES = 128
SUBLANES = 8
QA, KA, VA = 0, 8, 16
QB, KB, VB = 24, 32, 34
QC, KC, VC = 36, 44, 52
QD, KDC, VDC, KDS, VDS, KDW, VDW = 60, 68, 70, 72, 74, 76, 78

VMEM_LIMIT = 56 * 1024 * 1024


def _cparams(n_axes):
    return pltpu.CompilerParams(dimension_semantics=("arbitrary",) * n_axes,
                                vmem_limit_bytes=VMEM_LIMIT)


def _mod_kernel(c_ref, w_ref, b_ref, o_ref):
    c = c_ref[...]
    ca = (c * jax.nn.sigmoid(c)).astype(BF16)
    o_ref[0] = jnp.dot(ca, w_ref[0].astype(BF16), preferred_element_type=F32) + b_ref[0]


def _modulation(c, ada_w, ada_b):
    depth, d, n = ada_w.shape
    b = c.shape[0]
    bp = 8
    tn = 1024
    cp = jnp.pad(c, ((0, bp - b), (0, 0)))
    out = pl.pallas_call(
        _mod_kernel,
        grid=(depth, n // tn),
        in_specs=[pl.BlockSpec((bp, d), lambda l, j: (0, 0)),
                  pl.BlockSpec((1, d, tn), lambda l, j: (l, 0, j)),
                  pl.BlockSpec((1, 1, tn), lambda l, j: (l, 0, j))],
        out_specs=pl.BlockSpec((1, bp, tn), lambda l, j: (l, 0, j)),
        out_shape=jax.ShapeDtypeStruct((depth, bp, n), F32),
        compiler_params=_cparams(2),
        name="adaln_mod",
    )(cp, ada_w, ada_b.reshape(depth, 1, n))
    return out[:, :b]


def _inproj_kernel(x_ref, g_ref, sc_ref, sh_ref, w_ref, wg_ref, p_ref, gate_ref, h_scr):
    n = pl.program_id(2)

    @pl.when(n == 0)
    def _():
        x = x_ref[0]
        ms = jnp.mean(x * x, axis=-1, keepdims=True)
        h = (x * lax.rsqrt(ms + NORM_EPS)) * g_ref[...] * (1.0 + sc_ref[0]) + sh_ref[0]
        hb = h.astype(BF16)
        h_scr[...] = hb
        gate_ref[0] = jax.nn.sigmoid(jnp.dot(hb, wg_ref[...], preferred_element_type=F32))

    res = jnp.dot(h_scr[...], w_ref[...], preferred_element_type=F32)
    for j in range(HEADS):
        p_ref[0, j] = res[:, j * HEAD_DIM:(j + 1) * HEAD_DIM].astype(BF16)


def _in_projection(x, g, sc, sh, w_heads, w_gate):
    b, s, d = x.shape
    tm = min(1024, s)
    tn = HEADS * HEAD_DIM
    return pl.pallas_call(
        _inproj_kernel,
        grid=(b, s // tm, PROJ_COLS // tn),
        in_specs=[pl.BlockSpec((1, tm, d), lambda i, m, n: (i, m, 0)),
                  pl.BlockSpec((1, d), lambda i, m, n: (0, 0)),
                  pl.BlockSpec((1, 1, d), lambda i, m, n: (i, 0, 0)),
                  pl.BlockSpec((1, 1, d), lambda i, m, n: (i, 0, 0)),
                  pl.BlockSpec((d, tn), lambda i, m, n: (0, n)),
                  pl.BlockSpec((d, LANES), lambda i, m, n: (0, 0))],
        out_specs=[pl.BlockSpec((1, HEADS, tm, HEAD_DIM), lambda i, m, n: (i, n, m, 0)),
                   pl.BlockSpec((1, tm, LANES), lambda i, m, n: (i, m, 0))],
        out_shape=[jax.ShapeDtypeStruct((b, N_PROJ_HEADS, s, HEAD_DIM), BF16),
                   jax.ShapeDtypeStruct((b, s, LANES), F32)],
        scratch_shapes=[pltpu.VMEM((tm, d), BF16)],
        compiler_params=_cparams(3),
        name="in_proj",
    )(x, g.reshape(1, d), sc.reshape(b, 1, d), sh.reshape(b, 1, d), w_heads, w_gate)


def _bucket_thresholds():
    d = np.arange(REL_MAX_DIST + 1)
    max_exact = REL_BUCKETS // 2
    large = max_exact + (np.log(np.maximum(d, 1).astype(np.float32) / np.float32(max_exact))
                         / np.float32(math.log(REL_MAX_DIST / max_exact))
                         * np.float32(REL_BUCKETS - max_exact)).astype(np.int32)
    bucket = np.where(d < max_exact, d, np.minimum(large, REL_BUCKETS - 1))
    return [int(np.argmax(bucket >= b)) for b in range(REL_BUCKETS)]


def _bias_kernel(off_ref, cmin_ref, tab_ref, o_ref, *, max_dist, dist_scale, thresholds):
    h = pl.program_id(0)
    t = pl.program_id(1)
    shape = o_ref.shape[-2:]
    rel = off_ref[t] + lax.broadcasted_iota(jnp.int32, shape, 0) - lax.broadcasted_iota(jnp.int32, shape, 1)
    acc = jnp.full(shape, tab_ref[h, 0], F32)
    for b in range(1, REL_BUCKETS):
        acc = jnp.where(rel >= -(-thresholds[b] // dist_scale), tab_ref[h, b], acc)
    col = lax.broadcasted_iota(jnp.int32, shape, 1)
    valid = (rel >= 0) & (rel <= max_dist) & (col >= cmin_ref[t])
    o_ref[...] = jnp.where(valid, acc, NEG_INF).reshape(o_ref.shape)


def _bias_tiles(tab, offsets, col_mins, rows, cols, max_dist, dist_scale, group):
    n_heads = tab.shape[0]
    n_tiles = len(offsets)
    smem = pl.BlockSpec(memory_space=pltpu.SMEM)
    out = pl.pallas_call(
        functools.partial(_bias_kernel, max_dist=max_dist, dist_scale=dist_scale,
                          thresholds=_bucket_thresholds()),
        grid=(n_heads, n_tiles),
        in_specs=[smem, smem, smem],
        out_specs=pl.BlockSpec((1, 1, 1, rows, cols), lambda h, t: (h // group, t, h % group, 0, 0)),
        out_shape=jax.ShapeDtypeStruct((n_heads // group, n_tiles, group, rows, cols), F32),
        compiler_params=_cparams(2),
        name="bias_tiles",
    )(jnp.asarray(offsets, jnp.int32), jnp.asarray(col_mins, jnp.int32), tab.astype(F32))
    return out.reshape(n_heads // group, n_tiles, group * rows, cols)


BAND_TILE = 128
BAND_STEP = 512


def _gate_column(gates, idx):
    col = lax.broadcasted_iota(jnp.int32, gates.shape, 1)
    return jnp.sum(jnp.where(col == idx, gates, 0.0), axis=-1, keepdims=True)


def _banded_kernel(*refs, units, r_heads, nsub, w, has_sink, gate_branch):
    q_ref, kp_ref, kc_ref, vp_ref, vc_ref, b_ref = refs[:6]
    pos = 6
    sink_ref = gate_ref = None
    if has_sink:
        sink_ref = refs[pos]
        pos += 1
    if gate_branch is not None:
        gate_ref = refs[pos]
        pos += 1
    o_ref, kw_scr, vw_scr = refs[pos:pos + 3]
    t = pl.program_id(2)
    tq = BAND_TILE
    rows = r_heads * tq
    nk = w + tq
    kw_scr[:, :w] = kp_ref[0]
    kw_scr[:, w:] = kc_ref[0]
    vw_scr[:, :w, :HEAD_DIM] = vp_ref[0]
    vw_scr[:, w:, :HEAD_DIM] = vc_ref[0]
    vw_scr[:, :, HEAD_DIM:] = jnp.ones(vw_scr.shape[:2] + (HEAD_DIM,), BF16)
    for u in range(units):
        for j in range(nsub):
            q = q_ref[0, u * r_heads:(u + 1) * r_heads, j * tq:(j + 1) * tq, :].reshape(rows, HEAD_DIM) * SCALE
            k = kw_scr[u, j * tq:j * tq + nk, :]
            v = vw_scr[u, j * tq:j * tq + nk, :]
            tile = jnp.minimum(t * nsub + j, w // tq)
            s = lax.dot_general(q, k, (((1,), (1,)), ((), ())), preferred_element_type=F32) + b_ref[u, tile]
            m = jnp.max(s, axis=-1, keepdims=True)
            if has_sink:
                m = jnp.maximum(m, sink_ref[u])
            e = jnp.exp(s - m)
            num_den = jnp.dot(e.astype(BF16), v, preferred_element_type=F32)
            den = num_den[:, HEAD_DIM:]
            if has_sink:
                den = den + jnp.exp(sink_ref[u] - m)
            o = num_den[:, :HEAD_DIM] / jnp.maximum(den, TINY)
            parts = []
            for g in range(r_heads):
                og = o[g * tq:(g + 1) * tq]
                if gate_ref is not None:
                    head = (pl.program_id(1) * units + u) * r_heads + g
                    og = og * _gate_column(gate_ref[0, j * tq:(j + 1) * tq, :], head * 3 + gate_branch)
                parts.append(og)
            lanes = slice(u * r_heads * HEAD_DIM, (u + 1) * r_heads * HEAD_DIM)
            o_ref[0, j * tq:(j + 1) * tq, lanes] = jnp.concatenate(parts, axis=-1)


def _band_bias_tiles(tab, max_dist, w, dist_scale, group):
    tq = BAND_TILE
    n_early = w // tq
    return _bias_tiles(tab, [w] * (n_early + 1), [w - e * tq for e in range(n_early)] + [0], tq, w + tq,
                       max_dist, dist_scale, group)


def _banded_attention(q_arr, k_arr, v_arr, bias, *, grid01, q_map, kv_maps, bias_map, r_heads, units, w,
                      out_heads, sink=None, gates=None, gate_branch=None, name):
    seq = q_arr.shape[2]
    tq = BAND_TILE
    step = min(BAND_STEP, seq)
    nsub = step // tq
    assert seq % step == 0 and w % tq == 0 and step % w == 0
    k_map, v_map = kv_maps
    prev = lambda t: jnp.maximum(t * (step // w) - 1, 0)
    in_specs = [pl.BlockSpec((1, units * r_heads, step, HEAD_DIM), lambda i, j, t: q_map(i, j) + (t, 0)),
                pl.BlockSpec((1, units, w, HEAD_DIM), lambda i, j, t: k_map(i, j) + (prev(t), 0)),
                pl.BlockSpec((1, units, step, HEAD_DIM), lambda i, j, t: k_map(i, j) + (t, 0)),
                pl.BlockSpec((1, units, w, HEAD_DIM), lambda i, j, t: v_map(i, j) + (prev(t), 0)),
                pl.BlockSpec((1, units, step, HEAD_DIM), lambda i, j, t: v_map(i, j) + (t, 0)),
                pl.BlockSpec((units, w // tq + 1, r_heads * tq, w + tq), lambda i, j, t: (bias_map(i, j), 0, 0, 0))]
    args = [q_arr, k_arr, k_arr, v_arr, v_arr, bias]
    if sink is not None:
        in_specs.append(pl.BlockSpec((units, r_heads * tq, 1), lambda i, j, t: (bias_map(i, j), 0, 0)))
        args.append(sink)
    if gates is not None:
        in_specs.append(pl.BlockSpec((1, step, LANES), lambda i, j, t: (i, t, 0)))
        args.append(gates)
    n0, n1 = grid01
    return pl.pallas_call(
        functools.partial(_banded_kernel, units=units, r_heads=r_heads, nsub=nsub, w=w,
                          has_sink=sink is not None, gate_branch=gate_branch if gates is not None else None),
        grid=(n0, n1, seq // step),
        in_specs=in_specs,
        out_specs=pl.BlockSpec((1, step, units * r_heads * HEAD_DIM), lambda i, j, t: (i, t, j)),
        out_shape=jax.ShapeDtypeStruct((n0, seq, out_heads * HEAD_DIM), F32),
        scratch_shapes=[pltpu.VMEM((units, w + step, HEAD_DIM), BF16),
                        pltpu.VMEM((units, w + step, 2 * HEAD_DIM), BF16)],
        compiler_params=_cparams(3),
        name=name,
    )(*args)


DILATED_WINDOW = 128


def _dilated_kernel(q_ref, k_ref, v_ref, b_ref, o_ref, q_scr, kv_scr, o_scr, l_scr, *, seq, pad):
    tq = BAND_TILE
    w = DILATED_WINDOW
    nt = (((1,), (1,)), ((), ()))
    zeros = jnp.zeros((seq, HEAD_DIM), F32)
    q_scr[...] = jnp.concatenate([q_ref[0, 0].astype(F32) * SCALE, zeros], axis=-1)
    kv_scr[:pad] = jnp.zeros((pad, 2 * HEAD_DIM), F32)
    kv_scr[pad:] = jnp.concatenate([k_ref[0, 0].astype(F32), v_ref[0, 0].astype(F32)], axis=-1)

    def rows(start, n, r):
        return pl.ds(start, n) if r == 1 else pl.ds(start, n, stride=r)

    def tile(cfg, r, j, t):
        first = j + r * (t * tq)
        q = q_scr[rows(first, tq, r), :].astype(BF16)
        kv = kv_scr[rows(pad + first - r * w, w + tq, r), :].astype(BF16)
        s = lax.dot_general(q, kv, nt, preferred_element_type=F32) + b_ref[0, cfg, jnp.minimum(t, 1)]
        m = jnp.max(s, axis=-1, keepdims=True)
        e = jnp.exp(s - m)
        den = jnp.maximum(jnp.sum(e, axis=-1, keepdims=True), TINY)
        o_scr[cfg, rows(first, tq, r), :] = jnp.dot(e.astype(BF16), kv, preferred_element_type=F32) / den
        l_scr[cfg, rows(first, tq, r), :] = jnp.broadcast_to(m + jnp.log(den), (tq, 2 * HEAD_DIM))

    group = 8
    for cfg, (_, r) in enumerate(DILATED_CONFIGS):
        nsub = seq // r // tq

        def body(g, c, cfg=cfg, r=r, nsub=nsub):
            for u in range(group):
                tid = g * group + u
                tile(cfg, r, tid // nsub, tid % nsub)
            return c

        lax.fori_loop(0, r * nsub // group, body, 0)

    chunk = 512
    for c in range(seq // chunk):
        sl = slice(c * chunk, (c + 1) * chunk)
        ls = [l_scr[i, sl] for i in range(len(DILATED_CONFIGS))]
        mx = functools.reduce(jnp.maximum, ls)
        ws = [jnp.exp(l - mx) for l in ls]
        mixed = sum(wi * o_scr[i, sl] for i, wi in enumerate(ws)) / sum(ws)
        o_ref[0, 0, sl, :] = mixed[:, HEAD_DIM:]


def _dilated_attention(p, bias):
    b, _, s, _ = p.shape
    pad = max(r for _, r in DILATED_CONFIGS) * DILATED_WINDOW
    assert all(win // r == DILATED_WINDOW and (s // r) % BAND_TILE == 0 for win, r in DILATED_CONFIGS)
    head_spec = lambda base: pl.BlockSpec((1, 1, s, HEAD_DIM), lambda i, h: (i, base + h, 0, 0))
    n_cfg = len(DILATED_CONFIGS)
    return pl.pallas_call(
        functools.partial(_dilated_kernel, seq=s, pad=pad),
        grid=(b, HEADS),
        in_specs=[head_spec(QA), head_spec(KA), head_spec(VA),
                  pl.BlockSpec((1, n_cfg, 2, BAND_TILE, DILATED_WINDOW + BAND_TILE), lambda i, h: (h, 0, 0, 0, 0))],
        out_specs=pl.BlockSpec((1, 1, s, HEAD_DIM), lambda i, h: (i, h, 0, 0)),
        out_shape=jax.ShapeDtypeStruct((b, HEADS, s, HEAD_DIM), F32),
        scratch_shapes=[pltpu.VMEM((s, 2 * HEAD_DIM), F32),
                        pltpu.VMEM((pad + s, 2 * HEAD_DIM), F32),
                        pltpu.VMEM((n_cfg, s, 2 * HEAD_DIM), F32),
                        pltpu.VMEM((n_cfg, s, 2 * HEAD_DIM), F32)],
        compiler_params=_cparams(2),
        name="dilated",
    )(p, p, p, bias)


def _ranks_along_rows(vals):
    n, t = vals.shape
    groups = [vals[g * SUBLANES:(g + 1) * SUBLANES] for g in range(n // SUBLANES)]
    row = lax.broadcasted_iota(jnp.int32, (SUBLANES, t), 0)
    later = [jnp.where(row > r, 1, 0) for r in range(SUBLANES)]
    cnt = [jnp.zeros((SUBLANES, t), jnp.int32) for _ in groups]
    for i in range(n):
        vi = vals[i:i + 1]
        for g, vg in enumerate(groups):
            if g > i // SUBLANES:
                cnt[g] = cnt[g] + jnp.where(vi >= vg, 1, 0)
            elif g < i // SUBLANES:
                cnt[g] = cnt[g] + jnp.where(vi > vg, 1, 0)
            else:
                cnt[g] = cnt[g] + jnp.where(vi > vg, 1, 0) + jnp.where(vi == vg, later[i % SUBLANES], 0)
    return jnp.concatenate(cnt, axis=0)


def _rows_to_lanes(x_t, width):
    n, t = x_t.shape
    padded = jnp.concatenate([x_t, jnp.zeros((LANES - n, t), x_t.dtype)], axis=0)
    return padded.T[:, :width]


def _tile_groups(n, fn):
    def body(j, c):
        fn(j * 4, 4)
        return c

    lax.fori_loop(0, n // 4, body, 0)
    base = (n // 4) * 4

    @pl.when((n & 2) != 0)
    def _():
        fn(base, 2)

    @pl.when((n & 1) != 0)
    def _():
        fn(base + (n & 2), 1)


def _flash_selected(qa_scr, qi, kaug_scr, vaug_scr, kv_of, b_ref, s_scr, m_scr, acc_scr, tq):
    nh = qa_scr.shape[0]
    half = tq // 2
    n = qi + 1
    nt = (((1,), (1,)), ((), ()))
    m_scr[...] = jnp.full(m_scr.shape, NEG_INF, F32)

    def logits(kj0, cnt):
        ks = pl.multiple_of(kj0 * tq, tq)
        for h in range(nh):
            s = lax.dot_general(qa_scr[h], kaug_scr[kv_of[h], pl.ds(ks, cnt * tq), :], nt,
                                preferred_element_type=F32)
            mloc = None
            for u in range(cnt):
                su = s[:, u * tq:(u + 1) * tq] + b_ref[h, jnp.minimum(qi - kj0 - u, b_ref.shape[1] - 1)]
                s_scr[h, kj0 + u] = su
                mu = jnp.maximum(su[:, :half], su[:, half:])
                mloc = mu if mloc is None else jnp.maximum(mloc, mu)
            m_scr[h] = jnp.maximum(m_scr[h], mloc)

    _tile_groups(n, logits)
    for h in range(nh):
        m_scr[h] = jnp.broadcast_to(jnp.max(m_scr[h], axis=-1, keepdims=True), (tq, half))
    acc_scr[...] = jnp.zeros(acc_scr.shape, F32)

    def weigh(kj0, cnt):
        ks = pl.multiple_of(kj0 * tq, tq)
        for h in range(nh):
            mb = m_scr[h]
            parts = []
            for u in range(cnt):
                su = s_scr[h, kj0 + u]
                parts += [jnp.exp(su[:, :half] - mb).astype(BF16), jnp.exp(su[:, half:] - mb).astype(BF16)]
            acc_scr[h] += jnp.dot(jnp.concatenate(parts, axis=-1), vaug_scr[kv_of[h], pl.ds(ks, cnt * tq), :],
                                  preferred_element_type=F32)

    _tile_groups(n, weigh)
    outs = []
    for h in range(nh):
        acc = acc_scr[h]
        outs.append(acc[:, :HEAD_DIM] / jnp.maximum(acc[:, HEAD_DIM:], TINY))
    return outs


def _fill_kv_aug(kaug_scr, vaug_scr, k_ref, v_ref, et_ref):
    for kv in range(kaug_scr.shape[0]):
        kaug_scr[kv, :, :HEAD_DIM] = k_ref[0, kv]
        kaug_scr[kv, :, HEAD_DIM:] = et_ref[...]
        vaug_scr[kv, :, :HEAD_DIM] = v_ref[0, kv]
        vaug_scr[kv, :, HEAD_DIM:] = jnp.ones((vaug_scr.shape[1], HEAD_DIM), BF16)


def _augment(q, pen):
    return jnp.concatenate([q * SCALE, pen.astype(BF16)], axis=-1)


def _moba_kernel(q_ref, k_ref, v_ref, pool_ref, et_ref, b_ref, o_ref, kmean_scr, kaug_scr, vaug_scr, qa_scr, s_scr,
                 m_scr, acc_scr, *, tq, nblk, nh):
    qi = pl.program_id(2)

    @pl.when(qi == 0)
    def _():
        _fill_kv_aug(kaug_scr, vaug_scr, k_ref, v_ref, et_ref)
        for h in range(nh):
            kmean_scr[h] = jnp.dot(pool_ref[...], k_ref[0, h], preferred_element_type=F32).astype(BF16)

    for h in range(nh):
        q = q_ref[0, h]
        gate = lax.dot_general(kmean_scr[h], q, (((1,), (1,)), ((), ())), preferred_element_type=F32)
        blk = lax.broadcasted_iota(jnp.int32, (nblk, tq), 0)
        gate = jnp.where(blk < qi, gate, NEG_INF)
        picked = ((_ranks_along_rows(gate) < MOBA_TOPK) & (blk < qi)) | (blk == qi)
        pen_t = jnp.where(picked, 0.0, NEG_INF)
        qa_scr[h] = _augment(q, _rows_to_lanes(pen_t, HEAD_DIM))
    outs = _flash_selected(qa_scr, qi, kaug_scr, vaug_scr, tuple(range(nh)), b_ref, s_scr, m_scr, acc_scr, tq)
    o_ref[0] = jnp.concatenate(outs, axis=-1)


def _slc_kernel(q_ref, k_ref, v_ref, pen_ref, gate_ref, et_ref, b_ref, o_ref, kaug_scr, vaug_scr, qa_scr, s_scr,
                m_scr, acc_scr, *, tq, nh):
    qi = pl.program_id(2)

    @pl.when(qi == 0)
    def _():
        _fill_kv_aug(kaug_scr, vaug_scr, k_ref, v_ref, et_ref)

    for h in range(nh):
        qa_scr[h] = _augment(q_ref[0, h], pen_ref[0, 0])
    outs = _flash_selected(qa_scr, qi, kaug_scr, vaug_scr, (0,) * nh, b_ref, s_scr, m_scr, acc_scr, tq)
    gates = gate_ref[0]
    first = pl.program_id(1) * nh
    o_ref[0] = jnp.concatenate([outs[h] * _gate_column(gates, (first + h) * 3 + 1) for h in range(nh)], axis=-1)


def _flash_state(n_kv, seq, nh, tq):
    return [pltpu.VMEM((n_kv, seq, 2 * HEAD_DIM), BF16),
            pltpu.VMEM((n_kv, seq, 2 * HEAD_DIM), BF16),
            pltpu.VMEM((nh, tq, 2 * HEAD_DIM), BF16),
            pltpu.VMEM((nh, seq // tq, tq, tq), F32),
            pltpu.VMEM((nh, tq, tq // 2), F32),
            pltpu.VMEM((nh, tq, 2 * HEAD_DIM), F32)]


FLASH_TILE = 256
MOBA_HEADS_PER_STEP = 4
SLC_HEADS_PER_STEP = GQA


def _flash_bias_tiles(tab, seq):
    tq = FLASH_TILE
    first_const = -(-(_bucket_thresholds()[-1] + tq - 1) // tq)
    n_off = min(seq // tq, first_const + 1)
    return _bias_tiles(tab, [o * tq for o in range(n_off)], [0] * n_off, tq, tq, seq, 1, 1)


def _block_onehot_t(nblk, seq):
    blk = seq // nblk
    return jnp.asarray((np.arange(seq)[:, None] // blk == np.arange(HEAD_DIM)[None, :]).astype(np.float32), BF16)


def _moba_attention(p, bias):
    b, _, s, _ = p.shape
    tq, nh = FLASH_TILE, MOBA_HEADS_PER_STEP
    assert tq == MOBA_BLOCK
    nblk = s // tq
    nrow = max(nblk, 16)
    n_bias = bias.shape[1]
    pool = jnp.asarray((np.arange(s)[None, :] // tq == np.arange(nrow)[:, None]).astype(np.float32) / tq, BF16)
    once = pl.Buffered(1)
    kv_spec = lambda base: pl.BlockSpec((1, nh, s, HEAD_DIM), lambda i, h, t: (i, base // nh + h, 0, 0),
                                        pipeline_mode=once)
    return pl.pallas_call(
        functools.partial(_moba_kernel, tq=tq, nblk=nrow, nh=nh),
        grid=(b, HEADS // nh, s // tq),
        in_specs=[pl.BlockSpec((1, nh, tq, HEAD_DIM), lambda i, h, t: (i, QC // nh + h, t, 0)),
                  kv_spec(KC), kv_spec(VC),
                  pl.BlockSpec((nrow, s), lambda i, h, t: (0, 0), pipeline_mode=once),
                  pl.BlockSpec((s, HEAD_DIM), lambda i, h, t: (0, 0), pipeline_mode=once),
                  pl.BlockSpec((nh, n_bias, tq, tq), lambda i, h, t: (h, 0, 0, 0), pipeline_mode=once)],
        out_specs=pl.BlockSpec((1, tq, nh * HEAD_DIM), lambda i, h, t: (i, t, h)),
        out_shape=jax.ShapeDtypeStruct((b, s, HEADS * HEAD_DIM), F32),
        scratch_shapes=[pltpu.VMEM((nh, nrow, HEAD_DIM), BF16)] + _flash_state(nh, s, nh, tq),
        compiler_params=_cparams(3),
        name="moba",
    )(p, p, p, pool, _block_onehot_t(nblk, s), bias)


def _nsa_selected_attention(p, pen, gates, bias):
    b, _, s, _ = p.shape
    tq, nh = FLASH_TILE, SLC_HEADS_PER_STEP
    nblk = s // NSA_SLC_BLOCK
    n_bias = bias.shape[1]
    assert nblk <= HEAD_DIM and GQA % nh == 0
    kv_spec = lambda base: pl.BlockSpec((1, 1, s, HEAD_DIM), lambda i, h, t: (i, base + (h * nh) // GQA, 0, 0))
    return pl.pallas_call(
        functools.partial(_slc_kernel, tq=tq, nh=nh),
        grid=(b, HEADS // nh, s // tq),
        in_specs=[pl.BlockSpec((1, nh, tq, HEAD_DIM), lambda i, h, t: (i, QD // nh + h, t, 0)),
                  kv_spec(KDS), kv_spec(VDS),
                  pl.BlockSpec((1, 1, tq, HEAD_DIM), lambda i, h, t: (i, (h * nh) // GQA, t, 0)),
                  pl.BlockSpec((1, tq, LANES), lambda i, h, t: (i, t, 0)),
                  pl.BlockSpec((s, HEAD_DIM), lambda i, h, t: (0, 0)),
                  pl.BlockSpec((nh, n_bias, tq, tq), lambda i, h, t: (h, 0, 0, 0))],
        out_specs=pl.BlockSpec((1, tq, nh * HEAD_DIM), lambda i, h, t: (i, t, h)),
        out_shape=jax.ShapeDtypeStruct((b, s, HEADS * HEAD_DIM), F32),
        scratch_shapes=_flash_state(1, s, nh, tq),
        compiler_params=_cparams(3),
        name="nsa_selected",
    )(p, p, p, pen, gates, _block_onehot_t(nblk, s), bias)


def _compress_kernel(t_ref, pe_ref, w1_ref, w2_ref, o_ref):
    t = t_ref[0, 0].astype(F32)
    half = t.shape[-1]
    top = (t + pe_ref[0, 0:1, :]).astype(BF16)
    bot = (t + pe_ref[0, 1:2, :]).astype(BF16)
    u = jnp.dot(top, w1_ref[0, :half, :], preferred_element_type=F32)
    vv = jnp.dot(bot, w1_ref[0, half:, :], preferred_element_type=F32)
    pre = u + pltpu.roll(vv, vv.shape[0] - 1, axis=0)
    hid = jax.nn.gelu(pre).astype(BF16)
    o_ref[0, 0] = jnp.dot(hid, w2_ref[0], preferred_element_type=F32).astype(BF16)


def _nsa_compress(p, pe, w1, w2):
    b, _, s, _ = p.shape
    n_half = s // NSA_CMP_STRIDE
    half = NSA_CMP_STRIDE * HEAD_DIM
    t = p[:, KDC:KDC + 2 * KV_HEADS].reshape(b, 2 * KV_HEADS, n_half, half)
    pe2 = pe.reshape(2, 2, half)
    return pl.pallas_call(
        _compress_kernel,
        grid=(b, 2 * KV_HEADS),
        in_specs=[pl.BlockSpec((1, 1, n_half, half), lambda i, j: (i, j, 0, 0)),
                  pl.BlockSpec((1, 2, half), lambda i, j: (j // KV_HEADS, 0, 0)),
                  pl.BlockSpec((1, 2 * half, NSA_CMP_HIDDEN), lambda i, j: (j // KV_HEADS, 0, 0)),
                  pl.BlockSpec((1, NSA_CMP_HIDDEN, HEAD_DIM), lambda i, j: (j // KV_HEADS, 0, 0))],
        out_specs=pl.BlockSpec((1, 1, n_half, HEAD_DIM), lambda i, j: (i, j, 0, 0)),
        out_shape=jax.ShapeDtypeStruct((b, 2 * KV_HEADS, n_half, HEAD_DIM), BF16),
        compiler_params=_cparams(2),
        name="nsa_compress",
    )(t, pe2, w1.astype(BF16), w2.astype(BF16))


def _cmp_attn_kernel(q_ref, kc_ref, vc_ref, ov_ref, gate_ref, o_ref, pen_ref, *, tq, n_slc):
    qi = pl.program_id(2)
    rows = GQA * tq
    q = q_ref[0].reshape(rows, HEAD_DIM)
    kc = kc_ref[0, 0]
    n_cmp = kc.shape[0]
    s = lax.dot_general(q, kc, (((1,), (1,)), ((), ())), preferred_element_type=F32) * SCALE
    row = lax.broadcasted_iota(jnp.int32, (rows, n_cmp), 0)
    pos = qi * tq + (row & (tq - 1))
    blk_end = lax.broadcasted_iota(jnp.int32, (rows, n_cmp), 1) * NSA_CMP_STRIDE + (NSA_CMP_BLOCK - 1)
    mask = blk_end <= pos
    s = jnp.where(mask, s, NEG_INF)
    m = jnp.max(s, axis=-1, keepdims=True)
    e = jnp.where(mask, jnp.exp(s - m), 0.0)
    den = jnp.maximum(jnp.sum(e, axis=-1, keepdims=True), TINY)
    pr = e / den
    o = jnp.dot(pr.astype(BF16), vc_ref[0, 0], preferred_element_type=F32)
    gates = gate_ref[0]
    first = pl.program_id(1) * GQA
    o_ref[0] = jnp.concatenate([o[g * tq:(g + 1) * tq] * _gate_column(gates, (first + g) * 3)
                                for g in range(GQA)], axis=-1)
    psum = pr[0:tq]
    for g in range(1, GQA):
        psum = psum + pr[g * tq:(g + 1) * tq]
    ov_t = ov_ref[...]
    hi = psum.astype(BF16)
    r1 = psum - hi.astype(F32)
    mid = r1.astype(BF16)
    lo = (r1 - mid.astype(F32)).astype(BF16)
    nt = (((1,), (1,)), ((), ()))
    imp_t = (lax.dot_general(ov_t, hi, nt, preferred_element_type=F32)
             + lax.dot_general(ov_t, mid, nt, preferred_element_type=F32)
             + lax.dot_general(ov_t, lo, nt, preferred_element_type=F32))
    jj = lax.broadcasted_iota(jnp.int32, (n_slc, tq), 0)
    cur = (qi * tq + lax.broadcasted_iota(jnp.int32, (n_slc, tq), 1)) // NSA_SLC_BLOCK
    forced = (jj == 0) | (jj == cur) | (jj == cur - 1)
    imp_t = jnp.where(forced, -NEG_INF, jnp.where(jj <= cur, imp_t, NEG_INF))
    picked = (_ranks_along_rows(imp_t) < NSA_SLC_TOPN) & (jj <= cur)
    pen_ref[0, 0] = _rows_to_lanes(jnp.where(picked, 0.0, NEG_INF), HEAD_DIM).astype(BF16)


def _nsa_compressed_attention(p, kvc, gates):
    b, _, s, _ = p.shape
    tq = 256
    n_cmp = kvc.shape[2]
    n_slc = s // NSA_SLC_BLOCK
    assert n_slc == HEAD_DIM
    c_start = np.arange(n_cmp) * NSA_CMP_STRIDE
    j_start = np.arange(n_slc) * NSA_SLC_BLOCK
    overlap = ((c_start[:, None] < j_start[None, :] + NSA_SLC_BLOCK)
               & (c_start[:, None] + NSA_CMP_BLOCK > j_start[None, :])).astype(np.float32)
    overlap[n_cmp - 1] = 0.0
    return pl.pallas_call(
        functools.partial(_cmp_attn_kernel, tq=tq, n_slc=n_slc),
        grid=(b, KV_HEADS, s // tq),
        in_specs=[pl.BlockSpec((1, GQA, tq, HEAD_DIM), lambda i, h, t: (i, QD // GQA + h, t, 0)),
                  pl.BlockSpec((1, 1, n_cmp, HEAD_DIM), lambda i, h, t: (i, h, 0, 0)),
                  pl.BlockSpec((1, 1, n_cmp, HEAD_DIM), lambda i, h, t: (i, KV_HEADS + h, 0, 0)),
                  pl.BlockSpec((n_slc, n_cmp), lambda i, h, t: (0, 0)),
                  pl.BlockSpec((1, tq, LANES), lambda i, h, t: (i, t, 0))],
        out_specs=[pl.BlockSpec((1, tq, GQA * HEAD_DIM), lambda i, h, t: (i, t, h)),
                   pl.BlockSpec((1, 1, tq, HEAD_DIM), lambda i, h, t: (i, h, t, 0))],
        out_shape=[jax.ShapeDtypeStruct((b, s, HEADS * HEAD_DIM), F32),
                   jax.ShapeDtypeStruct((b, KV_HEADS, s, HEAD_DIM), BF16)],
        compiler_params=_cparams(3),
        name="nsa_compressed",
    )(p, kvc, kvc, jnp.asarray(overlap.T, BF16), gates)


def _group_norm(o, g):
    ms = jnp.mean(o * o, axis=-1, keepdims=True)
    return (o * lax.rsqrt(ms + NORM_EPS)) * g


def _route(logits_t, bias_t):
    aff = jax.nn.sigmoid(logits_t)
    score = aff + bias_t
    rows = [score[e:e + 1] for e in range(N_EXPERTS)]
    affs = [aff[e:e + 1] for e in range(N_EXPERTS)]
    gscore = []
    for g in range(N_EXPERT_GROUPS):
        mem = rows[g * PER_GROUP:(g + 1) * PER_GROUP]
        best = None
        for a in range(PER_GROUP):
            for c in range(a + 1, PER_GROUP):
                pair = mem[a] + mem[c]
                best = pair if best is None else jnp.maximum(best, pair)
        gscore.append(best)
    gsel = jnp.zeros_like(gscore[0], dtype=jnp.int32)
    gbest = gscore[0]
    for g in range(1, N_EXPERT_GROUPS):
        better = gscore[g] > gbest
        gsel = jnp.where(better, g, gsel)
        gbest = jnp.where(better, gscore[g], gbest)
    masked = [jnp.where(gsel == e // PER_GROUP, rows[e], NEG_INF) for e in range(N_EXPERTS)]

    def argbest(vals):
        idx = jnp.zeros_like(gsel)
        best = vals[0]
        for e in range(1, N_EXPERTS):
            better = vals[e] > best
            idx = jnp.where(better, e, idx)
            best = jnp.where(better, vals[e], best)
        return idx

    e1 = argbest(masked)
    e2 = argbest([jnp.where(e1 == e, -jnp.inf, masked[e]) for e in range(N_EXPERTS)])
    w1 = sum(jnp.where(e1 == e, affs[e], 0.0) for e in range(N_EXPERTS))
    w2 = sum(jnp.where(e2 == e, affs[e], 0.0) for e in range(N_EXPERTS))
    tot = w1 + w2
    w1 = w1 / tot
    w2 = w2 / tot
    comb = jnp.concatenate([jnp.where(e1 == e, w1, 0.0) + jnp.where(e2 == e, w2, 0.0)
                            for e in range(N_EXPERTS)], axis=0)
    return comb, gsel


def _out_kernel(x_ref, oa_ref, ob_ref, oc_ref, dc_ref, ds_ref, dw_ref,
                mixg_ref, wout_ref, g1_ref, n2g_ref, sc2_ref, sh2_ref, rw_ref, rb_ref,
                xo_ref, h2_ref, gsel_ref):
    o_a = jnp.concatenate([oa_ref[0, h] for h in range(HEADS)], axis=-1)
    o_b = ob_ref[0]
    o_c = oc_ref[0]
    o_d = dc_ref[0] + ds_ref[0] + dw_ref[0]
    mixed = jnp.concatenate([_group_norm(o, mixg_ref[i:i + 1, :]).astype(BF16)
                             for i, o in enumerate((o_a, o_b, o_c, o_d))], axis=-1)
    y = jnp.dot(mixed, wout_ref[...], preferred_element_type=F32)
    x = x_ref[0] + g1_ref[0] * y
    xo_ref[0] = x
    ms = jnp.mean(x * x, axis=-1, keepdims=True)
    h2 = (x * lax.rsqrt(ms + NORM_EPS)) * n2g_ref[...] * (1.0 + sc2_ref[0]) + sh2_ref[0]
    logits_t = lax.dot_general(rw_ref[...], h2.astype(BF16), (((1,), (1,)), ((), ())),
                               preferred_element_type=F32)
    comb_t, gsel = _route(logits_t, rb_ref[...])
    h2_ref[0] = jnp.concatenate([h2, _rows_to_lanes(comb_t, LANES)], axis=-1)
    gsel_ref[0] = gsel


def _mixer_out(x, o_a, o_b, o_c, o_dc, o_ds, o_dw, mix_g, w_out, g1, n2g, sc2, sh2, rw_t, rb):
    b, s, d = x.shape
    tm = min(256, s)
    head_spec = pl.BlockSpec((1, HEADS, tm, HEAD_DIM), lambda i, m: (i, 0, m, 0))
    mixer_spec = pl.BlockSpec((1, tm, GROUP_WIDTH), lambda i, m: (i, m, 0))
    vec_b = pl.BlockSpec((1, 1, d), lambda i, m: (i, 0, 0))
    return pl.pallas_call(
        _out_kernel,
        grid=(b, s // tm),
        in_specs=[pl.BlockSpec((1, tm, d), lambda i, m: (i, m, 0)),
                  head_spec, mixer_spec, mixer_spec, mixer_spec, mixer_spec, mixer_spec,
                  pl.BlockSpec((N_MIXERS, GROUP_WIDTH), lambda i, m: (0, 0)),
                  pl.BlockSpec((N_MIXERS * GROUP_WIDTH, d), lambda i, m: (0, 0)),
                  vec_b,
                  pl.BlockSpec((1, d), lambda i, m: (0, 0)),
                  vec_b, vec_b,
                  pl.BlockSpec((N_EXPERTS, d), lambda i, m: (0, 0)),
                  pl.BlockSpec((N_EXPERTS, 1), lambda i, m: (0, 0))],
        out_specs=[pl.BlockSpec((1, tm, d), lambda i, m: (i, m, 0)),
                   pl.BlockSpec((1, tm, d + LANES), lambda i, m: (i, m, 0)),
                   pl.BlockSpec((1, 1, tm), lambda i, m: (i, 0, m))],
        out_shape=[jax.ShapeDtypeStruct((b, s, d), F32),
                   jax.ShapeDtypeStruct((b, s, d + LANES), F32),
                   jax.ShapeDtypeStruct((b, 1, s), jnp.int32)],
        compiler_params=_cparams(2),
        name="mixer_out",
    )(x, o_a, o_b, o_c, o_dc, o_ds, o_dw, mix_g, w_out, g1.reshape(b, 1, d), n2g.reshape(1, d),
      sc2.reshape(b, 1, d), sh2.reshape(b, 1, d), rw_t, rb.reshape(N_EXPERTS, 1))


EXPERT_TILE = 512


def _dispatch_plan(gsel, tm):
    gid = gsel.reshape(-1)
    n_tok = gid.shape[0]
    onehot = (gid[:, None] == jnp.arange(N_EXPERT_GROUPS, dtype=jnp.int32)[None, :]).astype(jnp.int32)
    csum = jnp.cumsum(onehot, axis=0)
    rank = jnp.sum((csum - onehot) * onehot, axis=1)
    padded = ((csum[-1] + tm - 1) // tm) * tm
    ends = jnp.cumsum(padded)
    dest = jnp.sum(onehot * (ends - padded)[None, :], axis=1) + rank
    n_rows = n_tok + N_EXPERT_GROUPS * tm
    src = jnp.zeros((n_rows,), jnp.int32).at[dest].set(jnp.arange(n_tok, dtype=jnp.int32), unique_indices=True)
    tile_start = jnp.arange(n_rows // tm, dtype=jnp.int32) * tm
    tile_gid = jnp.minimum(jnp.sum((tile_start[:, None] >= ends[None, :]).astype(jnp.int32), axis=1),
                           N_EXPERT_GROUPS - 1)
    return dest.astype(jnp.int32), src, tile_gid, (ends[-1] // tm).reshape(1).astype(jnp.int32)


def _start_row_gather(idx_ref, base, src_hbm, buf, sem):
    def issue(r, c):
        pltpu.make_async_copy(src_hbm.at[pl.ds(idx_ref[base + r], 1)], buf.at[pl.ds(r, 1)], sem).start()
        return c

    lax.fori_loop(0, buf.shape[0], issue, 0, unroll=8)


def _wait_row_gather(src_hbm, buf, sem):
    pltpu.make_async_copy(src_hbm.at[pl.ds(0, buf.shape[0])], buf, sem).wait()


def _expert_kernel(src_ref, gid_ref, nvalid_ref, h_hbm, wgu_ref, wd_ref, o_ref, rows_buf, sems, h_scr, acc_ref, *, tm):
    t = pl.program_id(0)
    e = pl.program_id(1)
    slot = t % 2
    n_valid = nvalid_ref[0]

    @pl.when(t < n_valid)
    def _():
        @pl.when(e == 0)
        def _():
            @pl.when(t == 0)
            def _():
                _start_row_gather(src_ref, 0, h_hbm, rows_buf.at[0], sems.at[0])

            _wait_row_gather(h_hbm, rows_buf.at[slot], sems.at[slot])

            @pl.when(t + 1 < n_valid)
            def _():
                _start_row_gather(src_ref, (t + 1) * tm, h_hbm, rows_buf.at[1 - slot], sems.at[1 - slot])

            h_scr[...] = rows_buf[slot, :, :D_MODEL].astype(BF16)
            acc_ref[...] = jnp.zeros_like(acc_ref)

        gu = jnp.dot(h_scr[...], wgu_ref[0], preferred_element_type=F32)
        gate = gu[:, :D_EXPERT]
        he = (gate * jax.nn.sigmoid(gate)) * gu[:, D_EXPERT:]
        cw = rows_buf[slot, :, D_MODEL:]
        col = lax.broadcasted_iota(jnp.int32, cw.shape, 1)
        w = jnp.sum(jnp.where(col == gid_ref[t] * PER_GROUP + e, cw, 0.0), axis=-1, keepdims=True)
        acc_ref[...] += w * jnp.dot(he.astype(BF16), wd_ref[0], preferred_element_type=F32)

        @pl.when(e == PER_GROUP - 1)
        def _():
            o_ref[...] = acc_ref[...]

    @pl.when((t >= n_valid) & (e == PER_GROUP - 1))
    def _():
        o_ref[...] = jnp.zeros_like(o_ref)


def _experts(h2ext, src, tile_gid, n_valid, w_gu, w_d, tm):
    n_rows = src.shape[0]
    d = D_MODEL
    expert = lambda t, e, src_ref, gid, nv: (gid[t] * PER_GROUP + e, 0, 0)
    return pl.pallas_call(
        functools.partial(_expert_kernel, tm=tm),
        grid_spec=pltpu.PrefetchScalarGridSpec(
            num_scalar_prefetch=3,
            grid=(n_rows // tm, PER_GROUP),
            in_specs=[pl.BlockSpec(memory_space=pl.ANY),
                      pl.BlockSpec((1, d, 2 * D_EXPERT), expert),
                      pl.BlockSpec((1, D_EXPERT, d), expert)],
            out_specs=pl.BlockSpec((tm, d), lambda t, e, src_ref, gid, nv: (t, 0)),
            scratch_shapes=[pltpu.VMEM((2, tm, d + LANES), F32), pltpu.SemaphoreType.DMA((2,)),
                            pltpu.VMEM((tm, d), BF16), pltpu.VMEM((tm, d), F32)]),
        out_shape=jax.ShapeDtypeStruct((n_rows, d), F32),
        compiler_params=_cparams(2),
        name="experts",
    )(src, tile_gid, n_valid, h2ext, w_gu, w_d)


COMBINE_TILE = 512


def _combine_kernel(dest_ref, ys_hbm, x_ref, g2_ref, fg_ref, o_ref, rows_buf, sems, *, tm, final_norm):
    i = pl.program_id(0)
    slot = i % 2

    @pl.when(i == 0)
    def _():
        _start_row_gather(dest_ref, 0, ys_hbm, rows_buf.at[0], sems.at[0])

    _wait_row_gather(ys_hbm, rows_buf.at[slot], sems.at[slot])

    @pl.when(i + 1 < pl.num_programs(0))
    def _():
        _start_row_gather(dest_ref, (i + 1) * tm, ys_hbm, rows_buf.at[1 - slot], sems.at[1 - slot])

    x = x_ref[0] + g2_ref[0] * rows_buf[slot]
    if final_norm:
        ms = jnp.mean(x * x, axis=-1, keepdims=True)
        x = (x * lax.rsqrt(ms + NORM_EPS)) * fg_ref[...]
    o_ref[0] = x


def _combine(x, ys, dest, g2, final_g, final_norm):
    b, s, d = x.shape
    tm = min(COMBINE_TILE, s)
    per_seq = s // tm
    return pl.pallas_call(
        functools.partial(_combine_kernel, tm=tm, final_norm=final_norm),
        grid_spec=pltpu.PrefetchScalarGridSpec(
            num_scalar_prefetch=1,
            grid=(b * per_seq,),
            in_specs=[pl.BlockSpec(memory_space=pl.ANY),
                      pl.BlockSpec((1, tm, d), lambda i, dest_ref: (i // per_seq, i % per_seq, 0)),
                      pl.BlockSpec((1, 1, d), lambda i, dest_ref: (i // per_seq, 0, 0)),
                      pl.BlockSpec((1, d), lambda i, dest_ref: (0, 0))],
            out_specs=pl.BlockSpec((1, tm, d), lambda i, dest_ref: (i // per_seq, i % per_seq, 0)),
            scratch_shapes=[pltpu.VMEM((2, tm, d), F32), pltpu.SemaphoreType.DMA((2,))]),
        out_shape=jax.ShapeDtypeStruct((b, s, d), F32),
        compiler_params=_cparams(1),
        name="moe_combine",
    )(dest, ys, x, g2.reshape(b, 1, d), final_g.reshape(1, d))


def _moe(x, h2ext, gsel, w_gu, w_d, g2, final_g, final_norm):
    b, s, _ = x.shape
    dest, src, tile_gid, n_valid = _dispatch_plan(gsel, EXPERT_TILE)
    ys = _experts(h2ext.reshape(b * s, -1), src, tile_gid, n_valid, w_gu, w_d, EXPERT_TILE)
    return _combine(x, ys, dest, g2, final_g, final_norm)


def _bias_tables(rel_bias, s):
    tabs = [rel_bias[:, i * HEADS:(i + 1) * HEADS].T for i in range(N_MIXERS)]
    t = {}
    t["a"] = jnp.stack([_band_bias_tiles(tabs[0], window // r, DILATED_WINDOW, r, 1)
                        for window, r in DILATED_CONFIGS], axis=1)
    t["b"] = _band_bias_tiles(tabs[1], SWA_WINDOW - 1, 128, 1, GQA)
    t["c"] = _flash_bias_tiles(tabs[2], s)
    t["ds"] = _flash_bias_tiles(tabs[3], s)
    t["dw"] = _band_bias_tiles(tabs[3], NSA_WINDOW - 1, NSA_WINDOW, 1, GQA)
    return t


def _token_mixer_branches(p, gates, bias, sink, pe, w1, w2):
    b, _, s, _ = p.shape
    o_a = _dilated_attention(p, bias["a"])
    o_b = _banded_attention(p, p, p, bias["b"], grid01=(b, 1),
                            q_map=lambda i, j: (i, QB // HEADS), kv_maps=(lambda i, j: (i, KB // KV_HEADS),
                                                                         lambda i, j: (i, VB // KV_HEADS)),
                            bias_map=lambda i, j: 0, r_heads=GQA, units=KV_HEADS, w=128, out_heads=HEADS,
                            sink=sink, name="swa")
    o_c = _moba_attention(p, bias["c"])
    kvc = _nsa_compress(p, pe, w1, w2)
    o_dc, pen = _nsa_compressed_attention(p, kvc, gates)
    o_ds = _nsa_selected_attention(p, pen, gates, bias["ds"])
    o_dw = _banded_attention(p, p, p, bias["dw"], grid01=(b, KV_HEADS),
                             q_map=lambda i, j: (i, QD // GQA + j), kv_maps=(lambda i, j: (i, KDW + j),
                                                                            lambda i, j: (i, VDW + j)),
                             bias_map=lambda i, j: j, r_heads=GQA, units=1, w=NSA_WINDOW, out_heads=HEADS,
                             gates=gates, gate_branch=2, name="nsa_window")
    return o_a, o_b, o_c, o_dc, o_ds, o_dw


def _split_w_in(w_in_l):
    w_heads = w_in_l[:, :PROJ_COLS].astype(BF16)
    w_gate = jnp.pad(w_in_l[:, PROJ_COLS:], ((0, 0), (0, LANES - N_GATE_COLS))).astype(BF16)
    return w_heads, w_gate


def _sink_rows(sink_l):
    return jnp.repeat(sink_l.astype(F32).reshape(KV_HEADS, GQA), BAND_TILE, axis=1)[..., None]


def kernel(x, c, rel_bias, router_w, router_bias, norm1_g, norm2_g, ada_w, ada_b, w_in, nsa_pe_k, nsa_pe_v,
           nsa_cmp_w1_k, nsa_cmp_w2_k, nsa_cmp_w1_v, nsa_cmp_w2_v, sinks, mix_norm_g, w_out, exp_w_gate,
           exp_w_up, exp_w_down, final_g):
    b, s, d = x.shape
    mod = _modulation(c, ada_w, ada_b)
    bias = _bias_tables(rel_bias, s)
    rw_t = router_w.T.astype(BF16)
    for l in range(DEPTH):
        sh1, sc1, g1, sh2, sc2, g2 = jnp.split(mod[l], 6, axis=-1)
        w_heads, w_gate = _split_w_in(w_in[l])
        p, gates = _in_projection(x, norm1_g[l], sc1, sh1, w_heads, w_gate)
        pe = jnp.stack([nsa_pe_k[l], nsa_pe_v[l]])
        w1 = jnp.stack([nsa_cmp_w1_k[l], nsa_cmp_w1_v[l]])
        w2 = jnp.stack([nsa_cmp_w2_k[l], nsa_cmp_w2_v[l]])
        o_a, o_b, o_c, o_dc, o_ds, o_dw = _token_mixer_branches(p, gates, bias, _sink_rows(sinks[l]), pe, w1, w2)
        x, h2ext, gsel = _mixer_out(x, o_a, o_b, o_c, o_dc, o_ds, o_dw, mix_norm_g[l],
                                    w_out[l].astype(BF16), g1, norm2_g[l], sc2, sh2, rw_t, router_bias)
        w_gu = jnp.concatenate([exp_w_gate[l], exp_w_up[l]], axis=-1).astype(BF16)
        x = _moe(x, h2ext, gsel, w_gu, exp_w_down[l].astype(BF16), g2, final_g, l == DEPTH - 1)
    return x
```

```python
import functools
import math

import numpy as np
import jax
import jax.numpy as jnp
from jax import lax
from jax.experimental import pallas as pl
from jax.experimental.pallas import tpu as pltpu

F32 = jnp.float32
BF16 = jnp.bfloat16

D_MODEL = 2048
DEPTH = 4
HEAD_DIM = 64
N_MIXERS = 4
HEADS = 8
GROUP_WIDTH = HEADS * HEAD_DIM
SCALE = HEAD_DIM ** -0.5
REL_BUCKETS = 32
REL_MAX_DIST = 2048
DILATED_CONFIGS = ((128, 1), (512, 4), (2048, 16))
SWA_WINDOW = 128
KV_HEADS = 2
GQA = HEADS // KV_HEADS
MOBA_BLOCK = 256
MOBA_TOPK = 3
NSA_CMP_BLOCK = 32
NSA_CMP_STRIDE = 16
NSA_CMP_HIDDEN = 128
NSA_SLC_BLOCK = 64
NSA_SLC_TOPN = 16
NSA_WINDOW = 512
N_EXPERTS = 16
N_EXPERT_GROUPS = 4
PER_GROUP = N_EXPERTS // N_EXPERT_GROUPS
D_EXPERT = 512
NORM_EPS = 1e-6
NEG_INF = -1e30
TINY = float(np.finfo(np.float32).tiny)

N_PROJ_HEADS = 80
PROJ_COLS = N_PROJ_HEADS * HEAD_DIM
N_GATE_COLS = HEADS * 3
LANES = 128
SUBLANES = 8
QA, KA, VA = 0, 8, 16
QB, KB, VB = 24, 32, 34
QC, KC, VC = 36, 44, 52
QD, KDC, VDC, KDS, VDS, KDW, VDW = 60, 68, 70, 72, 74, 76, 78

VMEM_LIMIT = 56 * 1024 * 1024


def _cparams(n_axes):
    return pltpu.CompilerParams(dimension_semantics=("arbitrary",) * n_axes,
                                vmem_limit_bytes=VMEM_LIMIT)


def _mod_kernel(c_ref, w_ref, b_ref, o_ref):
    c = c_ref[...]
    ca = (c * jax.nn.sigmoid(c)).astype(BF16)
    o_ref[0] = jnp.dot(ca, w_ref[0].astype(BF16), preferred_element_type=F32) + b_ref[0]


def _modulation(c, ada_w, ada_b):
    depth, d, n = ada_w.shape
    b = c.shape[0]
    bp = 8
    tn = 1024
    cp = jnp.pad(c, ((0, bp - b), (0, 0)))
    out = pl.pallas_call(
        _mod_kernel,
        grid=(depth, n // tn),
        in_specs=[pl.BlockSpec((bp, d), lambda l, j: (0, 0)),
                  pl.BlockSpec((1, d, tn), lambda l, j: (l, 0, j)),
                  pl.BlockSpec((1, 1, tn), lambda l, j: (l, 0, j))],
        out_specs=pl.BlockSpec((1, bp, tn), lambda l, j: (l, 0, j)),
        out_shape=jax.ShapeDtypeStruct((depth, bp, n), F32),
        compiler_params=_cparams(2),
        name="adaln_mod",
    )(cp, ada_w, ada_b.reshape(depth, 1, n))
    return out[:, :b]


CMP_COL_GROUP = KDC // HEADS


def _inproj_kernel(x_ref, g_ref, sc_ref, sh_ref, w_ref, wg_ref, p_ref, gate_ref, cmp_ref, h_scr, res_scr):
    n = pl.program_id(2)

    @pl.when(n == 0)
    def _():
        x = x_ref[0]
        ms = jnp.mean(x * x, axis=-1, keepdims=True)
        h = (x * lax.rsqrt(ms + NORM_EPS)) * g_ref[...] * (1.0 + sc_ref[0]) + sh_ref[0]
        hb = h.astype(BF16)
        h_scr[...] = hb
        gate_ref[0] = jax.nn.sigmoid(jnp.dot(hb, wg_ref[...], preferred_element_type=F32))

    res = jnp.dot(h_scr[...], w_ref[...], preferred_element_type=F32)
    for j in range(HEADS):
        p_ref[0, j] = res[:, j * HEAD_DIM:(j + 1) * HEAD_DIM].astype(BF16)

    @pl.when(n == CMP_COL_GROUP)
    def _():
        first = (KDC % HEADS) * HEAD_DIM
        n_slab = 2 * KV_HEADS * HEAD_DIM // LANES
        for sl in range(n_slab):
            res_scr[sl] = res[:, first + sl * LANES:first + (sl + 1) * LANES]
        rows = res.shape[0] // NSA_CMP_STRIDE
        for sl in range(n_slab):
            taps = [res_scr[sl, pl.ds(i, rows, stride=NSA_CMP_STRIDE), :] for i in range(NSA_CMP_STRIDE)]
            for half in range(LANES // HEAD_DIM):
                lanes = slice(half * HEAD_DIM, (half + 1) * HEAD_DIM)
                cmp_ref[0, sl * (LANES // HEAD_DIM) + half] = jnp.concatenate(
                    [tap[:, lanes] for tap in taps], axis=-1).astype(BF16)


def _in_projection(x, g, sc, sh, w_heads, w_gate):
    b, s, d = x.shape
    tm = min(1024, s)
    tn = HEADS * HEAD_DIM
    n_cmp_heads = 2 * KV_HEADS
    half = NSA_CMP_STRIDE * HEAD_DIM
    return pl.pallas_call(
        _inproj_kernel,
        grid=(b, s // tm, PROJ_COLS // tn),
        in_specs=[pl.BlockSpec((1, tm, d), lambda i, m, n: (i, m, 0)),
                  pl.BlockSpec((1, d), lambda i, m, n: (0, 0)),
                  pl.BlockSpec((1, 1, d), lambda i, m, n: (i, 0, 0)),
                  pl.BlockSpec((1, 1, d), lambda i, m, n: (i, 0, 0)),
                  pl.BlockSpec((d, tn), lambda i, m, n: (0, n)),
                  pl.BlockSpec((d, LANES), lambda i, m, n: (0, 0))],
        out_specs=[pl.BlockSpec((1, HEADS, tm, HEAD_DIM), lambda i, m, n: (i, n, m, 0)),
                   pl.BlockSpec((1, tm, LANES), lambda i, m, n: (i, m, 0)),
                   pl.BlockSpec((1, n_cmp_heads, tm // NSA_CMP_STRIDE, half), lambda i, m, n: (i, 0, m, 0))],
        out_shape=[jax.ShapeDtypeStruct((b, N_PROJ_HEADS, s, HEAD_DIM), BF16),
                   jax.ShapeDtypeStruct((b, s, LANES), F32),
                   jax.ShapeDtypeStruct((b, n_cmp_heads, s // NSA_CMP_STRIDE, half), BF16)],
        scratch_shapes=[pltpu.VMEM((tm, d), BF16),
                        pltpu.VMEM((n_cmp_heads * HEAD_DIM // LANES, tm, LANES), F32)],
        compiler_params=_cparams(3),
        name="in_proj",
    )(x, g.reshape(1, d), sc.reshape(b, 1, d), sh.reshape(b, 1, d), w_heads, w_gate)


def _bucket_thresholds():
    d = np.arange(REL_MAX_DIST + 1)
    max_exact = REL_BUCKETS // 2
    large = max_exact + (np.log(np.maximum(d, 1).astype(np.float32) / np.float32(max_exact))
                         / np.float32(math.log(REL_MAX_DIST / max_exact))
                         * np.float32(REL_BUCKETS - max_exact)).astype(np.int32)
    bucket = np.where(d < max_exact, d, np.minimum(large, REL_BUCKETS - 1))
    return [int(np.argmax(bucket >= b)) for b in range(REL_BUCKETS)]


def _bias_kernel(off_ref, cmin_ref, tab_ref, o_ref, *, max_dist, dist_scale, thresholds):
    h = pl.program_id(0)
    t = pl.program_id(1)
    shape = o_ref.shape[-2:]
    rel = off_ref[t] + lax.broadcasted_iota(jnp.int32, shape, 0) - lax.broadcasted_iota(jnp.int32, shape, 1)
    acc = jnp.full(shape, tab_ref[h, 0], F32)
    for b in range(1, REL_BUCKETS):
        acc = jnp.where(rel >= -(-thresholds[b] // dist_scale), tab_ref[h, b], acc)
    col = lax.broadcasted_iota(jnp.int32, shape, 1)
    valid = (rel >= 0) & (rel <= max_dist) & (col >= cmin_ref[t])
    o_ref[...] = jnp.where(valid, acc, NEG_INF).reshape(o_ref.shape)


def _bias_tiles(tab, offsets, col_mins, rows, cols, max_dist, dist_scale, group):
    n_heads = tab.shape[0]
    n_tiles = len(offsets)
    smem = pl.BlockSpec(memory_space=pltpu.SMEM)
    out = pl.pallas_call(
        functools.partial(_bias_kernel, max_dist=max_dist, dist_scale=dist_scale,
                          thresholds=_bucket_thresholds()),
        grid=(n_heads, n_tiles),
        in_specs=[smem, smem, smem],
        out_specs=pl.BlockSpec((1, 1, 1, rows, cols), lambda h, t: (h // group, t, h % group, 0, 0)),
        out_shape=jax.ShapeDtypeStruct((n_heads // group, n_tiles, group, rows, cols), F32),
        compiler_params=_cparams(2),
        name="bias_tiles",
    )(jnp.asarray(offsets, jnp.int32), jnp.asarray(col_mins, jnp.int32), tab.astype(F32))
    return out.reshape(n_heads // group, n_tiles, group * rows, cols)


BAND_TILE = 128
BAND_STEP = 512


def _gate_column(gates, idx):
    col = lax.broadcasted_iota(jnp.int32, gates.shape, 1)
    return jnp.sum(jnp.where(col == idx, gates, 0.0), axis=-1, keepdims=True)


def _banded_kernel(*refs, units, r_heads, nsub, w, has_sink, gate_branch):
    q_ref, kp_ref, kc_ref, vp_ref, vc_ref, b_ref = refs[:6]
    pos = 6
    sink_ref = gate_ref = None
    if has_sink:
        sink_ref = refs[pos]
        pos += 1
    if gate_branch is not None:
        gate_ref = refs[pos]
        pos += 1
    o_ref, kw_scr, vw_scr = refs[pos:pos + 3]
    t = pl.program_id(2)
    tq = BAND_TILE
    rows = r_heads * tq
    nk = w + tq
    kw_scr[:, :w] = kp_ref[0]
    kw_scr[:, w:] = kc_ref[0]
    vw_scr[:, :w, :HEAD_DIM] = vp_ref[0]
    vw_scr[:, w:, :HEAD_DIM] = vc_ref[0]
    vw_scr[:, :, HEAD_DIM:] = jnp.ones(vw_scr.shape[:2] + (HEAD_DIM,), BF16)
    for u in range(units):
        for j in range(nsub):
            q = q_ref[0, u * r_heads:(u + 1) * r_heads, j * tq:(j + 1) * tq, :].reshape(rows, HEAD_DIM) * SCALE
            k = kw_scr[u, j * tq:j * tq + nk, :]
            v = vw_scr[u, j * tq:j * tq + nk, :]
            tile = jnp.minimum(t * nsub + j, w // tq)
            s = lax.dot_general(q, k, (((1,), (1,)), ((), ())), preferred_element_type=F32) + b_ref[u, tile]
            m = jnp.max(s, axis=-1, keepdims=True)
            if has_sink:
                m = jnp.maximum(m, sink_ref[u])
            e = jnp.exp(s - m)
            num_den = jnp.dot(e.astype(BF16), v, preferred_element_type=F32)
            den = num_den[:, HEAD_DIM:]
            if has_sink:
                den = den + jnp.exp(sink_ref[u] - m)
            o = num_den[:, :HEAD_DIM] / jnp.maximum(den, TINY)
            parts = []
            for g in range(r_heads):
                og = o[g * tq:(g + 1) * tq]
                if gate_ref is not None:
                    head = (pl.program_id(1) * units + u) * r_heads + g
                    og = og * _gate_column(gate_ref[0, j * tq:(j + 1) * tq, :], head * 3 + gate_branch)
                parts.append(og)
            lanes = slice(u * r_heads * HEAD_DIM, (u + 1) * r_heads * HEAD_DIM)
            o_ref[0, j * tq:(j + 1) * tq, lanes] = jnp.concatenate(parts, axis=-1)


def _band_bias_tiles(tab, max_dist, w, dist_scale, group):
    tq = BAND_TILE
    n_early = w // tq
    return _bias_tiles(tab, [w] * (n_early + 1), [w - e * tq for e in range(n_early)] + [0], tq, w + tq,
                       max_dist, dist_scale, group)


def _banded_attention(q_arr, k_arr, v_arr, bias, *, grid01, q_map, kv_maps, bias_map, r_heads, units, w,
                      out_heads, sink=None, gates=None, gate_branch=None, name):
    seq = q_arr.shape[2]
    tq = BAND_TILE
    step = min(BAND_STEP, seq)
    nsub = step // tq
    assert seq % step == 0 and w % tq == 0 and step % w == 0
    k_map, v_map = kv_maps
    prev = lambda t: jnp.maximum(t * (step // w) - 1, 0)
    in_specs = [pl.BlockSpec((1, units * r_heads, step, HEAD_DIM), lambda i, j, t: q_map(i, j) + (t, 0)),
                pl.BlockSpec((1, units, w, HEAD_DIM), lambda i, j, t: k_map(i, j) + (prev(t), 0)),
                pl.BlockSpec((1, units, step, HEAD_DIM), lambda i, j, t: k_map(i, j) + (t, 0)),
                pl.BlockSpec((1, units, w, HEAD_DIM), lambda i, j, t: v_map(i, j) + (prev(t), 0)),
                pl.BlockSpec((1, units, step, HEAD_DIM), lambda i, j, t: v_map(i, j) + (t, 0)),
                pl.BlockSpec((units, w // tq + 1, r_heads * tq, w + tq), lambda i, j, t: (bias_map(i, j), 0, 0, 0))]
    args = [q_arr, k_arr, k_arr, v_arr, v_arr, bias]
    if sink is not None:
        in_specs.append(pl.BlockSpec((units, r_heads * tq, 1), lambda i, j, t: (bias_map(i, j), 0, 0)))
        args.append(sink)
    if gates is not None:
        in_specs.append(pl.BlockSpec((1, step, LANES), lambda i, j, t: (i, t, 0)))
        args.append(gates)
    n0, n1 = grid01
    return pl.pallas_call(
        functools.partial(_banded_kernel, units=units, r_heads=r_heads, nsub=nsub, w=w,
                          has_sink=sink is not None, gate_branch=gate_branch if gates is not None else None),
        grid=(n0, n1, seq // step),
        in_specs=in_specs,
        out_specs=pl.BlockSpec((1, step, units * r_heads * HEAD_DIM), lambda i, j, t: (i, t, j)),
        out_shape=jax.ShapeDtypeStruct((n0, seq, out_heads * HEAD_DIM), F32),
        scratch_shapes=[pltpu.VMEM((units, w + step, HEAD_DIM), BF16),
                        pltpu.VMEM((units, w + step, 2 * HEAD_DIM), BF16)],
        compiler_params=_cparams(3),
        name=name,
    )(*args)


DILATED_WINDOW = 128


def _dilated_kernel(q_ref, k_ref, v_ref, b_ref, o_ref, q_scr, kv_scr, o_scr, l_scr, *, seq, pad):
    tq = BAND_TILE
    w = DILATED_WINDOW
    nt = (((1,), (1,)), ((), ()))
    zeros = jnp.zeros((seq, HEAD_DIM), F32)
    q_scr[...] = jnp.concatenate([q_ref[0, 0].astype(F32) * SCALE, zeros], axis=-1)
    kv_scr[:pad] = jnp.zeros((pad, 2 * HEAD_DIM), F32)
    kv_scr[pad:] = jnp.concatenate([k_ref[0, 0].astype(F32), v_ref[0, 0].astype(F32)], axis=-1)

    def rows(start, n, r):
        return pl.ds(start, n) if r == 1 else pl.ds(start, n, stride=r)

    def tile(cfg, r, j, t):
        first = j + r * (t * tq)
        q = q_scr[rows(first, tq, r), :].astype(BF16)
        kv = kv_scr[rows(pad + first - r * w, w + tq, r), :].astype(BF16)
        s = lax.dot_general(q, kv, nt, preferred_element_type=F32) + b_ref[0, cfg, jnp.minimum(t, 1)]
        m = jnp.max(s, axis=-1, keepdims=True)
        e = jnp.exp(s - m)
        den = jnp.maximum(jnp.sum(e, axis=-1, keepdims=True), TINY)
        o_scr[cfg, rows(first, tq, r), :] = jnp.dot(e.astype(BF16), kv, preferred_element_type=F32) / den
        l_scr[cfg, rows(first, tq, r), :] = jnp.broadcast_to(m + jnp.log(den), (tq, 2 * HEAD_DIM))

    group = 8
    for cfg, (_, r) in enumerate(DILATED_CONFIGS):
        nsub = seq // r // tq

        def body(g, c, cfg=cfg, r=r, nsub=nsub):
            for u in range(group):
                tid = g * group + u
                tile(cfg, r, tid // nsub, tid % nsub)
            return c

        lax.fori_loop(0, r * nsub // group, body, 0)

    chunk = 512
    for c in range(seq // chunk):
        sl = slice(c * chunk, (c + 1) * chunk)
        ls = [l_scr[i, sl] for i in range(len(DILATED_CONFIGS))]
        mx = functools.reduce(jnp.maximum, ls)
        ws = [jnp.exp(l - mx) for l in ls]
        mixed = sum(wi * o_scr[i, sl] for i, wi in enumerate(ws)) / sum(ws)
        o_ref[0, 0, sl, :] = mixed[:, HEAD_DIM:]


def _dilated_attention(p, bias):
    b, _, s, _ = p.shape
    pad = max(r for _, r in DILATED_CONFIGS) * DILATED_WINDOW
    assert all(win // r == DILATED_WINDOW and (s // r) % BAND_TILE == 0 for win, r in DILATED_CONFIGS)
    head_spec = lambda base: pl.BlockSpec((1, 1, s, HEAD_DIM), lambda i, h: (i, base + h, 0, 0))
    n_cfg = len(DILATED_CONFIGS)
    return pl.pallas_call(
        functools.partial(_dilated_kernel, seq=s, pad=pad),
        grid=(b, HEADS),
        in_specs=[head_spec(QA), head_spec(KA), head_spec(VA),
                  pl.BlockSpec((1, n_cfg, 2, BAND_TILE, DILATED_WINDOW + BAND_TILE), lambda i, h: (h, 0, 0, 0, 0))],
        out_specs=pl.BlockSpec((1, 1, s, HEAD_DIM), lambda i, h: (i, h, 0, 0)),
        out_shape=jax.ShapeDtypeStruct((b, HEADS, s, HEAD_DIM), F32),
        scratch_shapes=[pltpu.VMEM((s, 2 * HEAD_DIM), F32),
                        pltpu.VMEM((pad + s, 2 * HEAD_DIM), F32),
                        pltpu.VMEM((n_cfg, s, 2 * HEAD_DIM), F32),
                        pltpu.VMEM((n_cfg, s, 2 * HEAD_DIM), F32)],
        compiler_params=_cparams(2),
        name="dilated",
    )(p, p, p, bias)


def _ranks_along_rows(vals):
    n, t = vals.shape
    groups = [vals[g * SUBLANES:(g + 1) * SUBLANES] for g in range(n // SUBLANES)]
    row = lax.broadcasted_iota(jnp.int32, (SUBLANES, t), 0)
    later = [jnp.where(row > r, 1, 0) for r in range(SUBLANES)]
    cnt = [jnp.zeros((SUBLANES, t), jnp.int32) for _ in groups]
    for i in range(n):
        vi = vals[i:i + 1]
        for g, vg in enumerate(groups):
            if g > i // SUBLANES:
                cnt[g] = cnt[g] + jnp.where(vi >= vg, 1, 0)
            elif g < i // SUBLANES:
                cnt[g] = cnt[g] + jnp.where(vi > vg, 1, 0)
            else:
                cnt[g] = cnt[g] + jnp.where(vi > vg, 1, 0) + jnp.where(vi == vg, later[i % SUBLANES], 0)
    return jnp.concatenate(cnt, axis=0)


def _rows_to_lanes(x_t, width):
    n, t = x_t.shape
    padded = jnp.concatenate([x_t, jnp.zeros((LANES - n, t), x_t.dtype)], axis=0)
    return padded.T[:, :width]


def _tile_groups(n, fn):
    def body(j, c):
        fn(j * 4, 4)
        return c

    lax.fori_loop(0, n // 4, body, 0)
    base = (n // 4) * 4

    @pl.when((n & 2) != 0)
    def _():
        fn(base, 2)

    @pl.when((n & 1) != 0)
    def _():
        fn(base + (n & 2), 1)


def _flash_selected(qa_scr, qi, kaug_scr, vaug_scr, kv_of, b_ref, s_scr, m_scr, acc_scr, tq):
    nh = qa_scr.shape[0]
    half = tq // 2
    n = qi + 1
    nt = (((1,), (1,)), ((), ()))
    m_scr[...] = jnp.full(m_scr.shape, NEG_INF, F32)

    def logits(kj0, cnt):
        ks = pl.multiple_of(kj0 * tq, tq)
        for h in range(nh):
            s = lax.dot_general(qa_scr[h], kaug_scr[kv_of[h], pl.ds(ks, cnt * tq), :], nt,
                                preferred_element_type=F32)
            mloc = None
            for u in range(cnt):
                su = s[:, u * tq:(u + 1) * tq] + b_ref[h, jnp.minimum(qi - kj0 - u, b_ref.shape[1] - 1)]
                s_scr[h, kj0 + u] = su
                mu = jnp.maximum(su[:, :half], su[:, half:])
                mloc = mu if mloc is None else jnp.maximum(mloc, mu)
            m_scr[h] = jnp.maximum(m_scr[h], mloc)

    _tile_groups(n, logits)
    for h in range(nh):
        m_scr[h] = jnp.broadcast_to(jnp.max(m_scr[h], axis=-1, keepdims=True), (tq, half))
    acc_scr[...] = jnp.zeros(acc_scr.shape, F32)

    def weigh(kj0, cnt):
        ks = pl.multiple_of(kj0 * tq, tq)
        for h in range(nh):
            mb = m_scr[h]
            parts = []
            for u in range(cnt):
                su = s_scr[h, kj0 + u]
                parts += [jnp.exp(su[:, :half] - mb).astype(BF16), jnp.exp(su[:, half:] - mb).astype(BF16)]
            acc_scr[h] += jnp.dot(jnp.concatenate(parts, axis=-1), vaug_scr[kv_of[h], pl.ds(ks, cnt * tq), :],
                                  preferred_element_type=F32)

    _tile_groups(n, weigh)
    outs = []
    for h in range(nh):
        acc = acc_scr[h]
        outs.append(acc[:, :HEAD_DIM] / jnp.maximum(acc[:, HEAD_DIM:], TINY))
    return outs


def _fill_kv_aug(kaug_scr, vaug_scr, k_ref, v_ref, et_ref):
    for kv in range(kaug_scr.shape[0]):
        kaug_scr[kv, :, :HEAD_DIM] = k_ref[0, kv]
        kaug_scr[kv, :, HEAD_DIM:] = et_ref[...]
        vaug_scr[kv, :, :HEAD_DIM] = v_ref[0, kv]
        vaug_scr[kv, :, HEAD_DIM:] = jnp.ones((vaug_scr.shape[1], HEAD_DIM), BF16)


def _augment(q, pen):
    return jnp.concatenate([q * SCALE, pen.astype(BF16)], axis=-1)


def _moba_kernel(q_ref, k_ref, v_ref, pool_ref, et_ref, b_ref, o_ref, kmean_scr, kaug_scr, vaug_scr, qa_scr, s_scr,
                 m_scr, acc_scr, *, tq, nblk, nh):
    qi = pl.program_id(2)

    @pl.when(qi == 0)
    def _():
        _fill_kv_aug(kaug_scr, vaug_scr, k_ref, v_ref, et_ref)
        for h in range(nh):
            kmean_scr[h] = jnp.dot(pool_ref[...], k_ref[0, h], preferred_element_type=F32).astype(BF16)

    for h in range(nh):
        q = q_ref[0, h]
        gate = lax.dot_general(kmean_scr[h], q, (((1,), (1,)), ((), ())), preferred_element_type=F32)
        blk = lax.broadcasted_iota(jnp.int32, (nblk, tq), 0)
        gate = jnp.where(blk < qi, gate, NEG_INF)
        picked = ((_ranks_along_rows(gate) < MOBA_TOPK) & (blk < qi)) | (blk == qi)
        pen_t = jnp.where(picked, 0.0, NEG_INF)
        qa_scr[h] = _augment(q, _rows_to_lanes(pen_t, HEAD_DIM))
    outs = _flash_selected(qa_scr, qi, kaug_scr, vaug_scr, tuple(range(nh)), b_ref, s_scr, m_scr, acc_scr, tq)
    o_ref[0] = jnp.concatenate(outs, axis=-1)


def _slc_kernel(q_ref, k_ref, v_ref, pen_ref, gate_ref, et_ref, b_ref, o_ref, kaug_scr, vaug_scr, qa_scr, s_scr,
                m_scr, acc_scr, *, tq, nh):
    qi = pl.program_id(2)

    @pl.when(qi == 0)
    def _():
        _fill_kv_aug(kaug_scr, vaug_scr, k_ref, v_ref, et_ref)

    for h in range(nh):
        qa_scr[h] = _augment(q_ref[0, h], pen_ref[0, 0])
    outs = _flash_selected(qa_scr, qi, kaug_scr, vaug_scr, (0,) * nh, b_ref, s_scr, m_scr, acc_scr, tq)
    gates = gate_ref[0]
    first = pl.program_id(1) * nh
    o_ref[0] = jnp.concatenate([outs[h] * _gate_column(gates, (first + h) * 3 + 1) for h in range(nh)], axis=-1)


def _flash_state(n_kv, seq, nh, tq):
    return [pltpu.VMEM((n_kv, seq, 2 * HEAD_DIM), BF16),
            pltpu.VMEM((n_kv, seq, 2 * HEAD_DIM), BF16),
            pltpu.VMEM((nh, tq, 2 * HEAD_DIM), BF16),
            pltpu.VMEM((nh, seq // tq, tq, tq), F32),
            pltpu.VMEM((nh, tq, tq // 2), F32),
            pltpu.VMEM((nh, tq, 2 * HEAD_DIM), F32)]


FLASH_TILE = 256
MOBA_HEADS_PER_STEP = 4
SLC_HEADS_PER_STEP = GQA


def _flash_bias_tiles(tab, seq):
    tq = FLASH_TILE
    first_const = -(-(_bucket_thresholds()[-1] + tq - 1) // tq)
    n_off = min(seq // tq, first_const + 1)
    return _bias_tiles(tab, [o * tq for o in range(n_off)], [0] * n_off, tq, tq, seq, 1, 1)


def _block_onehot_t(nblk, seq):
    blk = seq // nblk
    return jnp.asarray((np.arange(seq)[:, None] // blk == np.arange(HEAD_DIM)[None, :]).astype(np.float32), BF16)


def _moba_attention(p, bias):
    b, _, s, _ = p.shape
    tq, nh = FLASH_TILE, MOBA_HEADS_PER_STEP
    assert tq == MOBA_BLOCK
    nblk = s // tq
    nrow = max(nblk, 16)
    n_bias = bias.shape[1]
    pool = jnp.asarray((np.arange(s)[None, :] // tq == np.arange(nrow)[:, None]).astype(np.float32) / tq, BF16)
    once = pl.Buffered(1)
    kv_spec = lambda base: pl.BlockSpec((1, nh, s, HEAD_DIM), lambda i, h, t: (i, base // nh + h, 0, 0),
                                        pipeline_mode=once)
    return pl.pallas_call(
        functools.partial(_moba_kernel, tq=tq, nblk=nrow, nh=nh),
        grid=(b, HEADS // nh, s // tq),
        in_specs=[pl.BlockSpec((1, nh, tq, HEAD_DIM), lambda i, h, t: (i, QC // nh + h, t, 0)),
                  kv_spec(KC), kv_spec(VC),
                  pl.BlockSpec((nrow, s), lambda i, h, t: (0, 0), pipeline_mode=once),
                  pl.BlockSpec((s, HEAD_DIM), lambda i, h, t: (0, 0), pipeline_mode=once),
                  pl.BlockSpec((nh, n_bias, tq, tq), lambda i, h, t: (h, 0, 0, 0), pipeline_mode=once)],
        out_specs=pl.BlockSpec((1, tq, nh * HEAD_DIM), lambda i, h, t: (i, t, h)),
        out_shape=jax.ShapeDtypeStruct((b, s, HEADS * HEAD_DIM), F32),
        scratch_shapes=[pltpu.VMEM((nh, nrow, HEAD_DIM), BF16)] + _flash_state(nh, s, nh, tq),
        compiler_params=_cparams(3),
        name="moba",
    )(p, p, p, pool, _block_onehot_t(nblk, s), bias)


def _nsa_selected_attention(p, pen, gates, bias):
    b, _, s, _ = p.shape
    tq, nh = FLASH_TILE, SLC_HEADS_PER_STEP
    nblk = s // NSA_SLC_BLOCK
    n_bias = bias.shape[1]
    assert nblk <= HEAD_DIM and GQA % nh == 0
    kv_spec = lambda base: pl.BlockSpec((1, 1, s, HEAD_DIM), lambda i, h, t: (i, base + (h * nh) // GQA, 0, 0))
    return pl.pallas_call(
        functools.partial(_slc_kernel, tq=tq, nh=nh),
        grid=(b, HEADS // nh, s // tq),
        in_specs=[pl.BlockSpec((1, nh, tq, HEAD_DIM), lambda i, h, t: (i, QD // nh + h, t, 0)),
                  kv_spec(KDS), kv_spec(VDS),
                  pl.BlockSpec((1, 1, tq, HEAD_DIM), lambda i, h, t: (i, (h * nh) // GQA, t, 0)),
                  pl.BlockSpec((1, tq, LANES), lambda i, h, t: (i, t, 0)),
                  pl.BlockSpec((s, HEAD_DIM), lambda i, h, t: (0, 0)),
                  pl.BlockSpec((nh, n_bias, tq, tq), lambda i, h, t: (h, 0, 0, 0))],
        out_specs=pl.BlockSpec((1, tq, nh * HEAD_DIM), lambda i, h, t: (i, t, h)),
        out_shape=jax.ShapeDtypeStruct((b, s, HEADS * HEAD_DIM), F32),
        scratch_shapes=_flash_state(1, s, nh, tq),
        compiler_params=_cparams(3),
        name="nsa_selected",
    )(p, p, p, pen, gates, _block_onehot_t(nblk, s), bias)


def _compress_kernel(t_ref, pe_ref, w1_ref, w2_ref, o_ref):
    t = t_ref[0, 0].astype(F32)
    half = t.shape[-1]
    top = (t + pe_ref[0, 0:1, :]).astype(BF16)
    bot = (t + pe_ref[0, 1:2, :]).astype(BF16)
    u = jnp.dot(top, w1_ref[0, :half, :], preferred_element_type=F32)
    vv = jnp.dot(bot, w1_ref[0, half:, :], preferred_element_type=F32)
    pre = u + pltpu.roll(vv, vv.shape[0] - 1, axis=0)
    hid = jax.nn.gelu(pre).astype(BF16)
    o_ref[0, 0] = jnp.dot(hid, w2_ref[0], preferred_element_type=F32).astype(BF16)


def _nsa_compress(t, pe, w1, w2):
    b, _, n_half, half = t.shape
    pe2 = pe.reshape(2, 2, half)
    return pl.pallas_call(
        _compress_kernel,
        grid=(b, 2 * KV_HEADS),
        in_specs=[pl.BlockSpec((1, 1, n_half, half), lambda i, j: (i, j, 0, 0)),
                  pl.BlockSpec((1, 2, half), lambda i, j: (j // KV_HEADS, 0, 0)),
                  pl.BlockSpec((1, 2 * half, NSA_CMP_HIDDEN), lambda i, j: (j // KV_HEADS, 0, 0)),
                  pl.BlockSpec((1, NSA_CMP_HIDDEN, HEAD_DIM), lambda i, j: (j // KV_HEADS, 0, 0))],
        out_specs=pl.BlockSpec((1, 1, n_half, HEAD_DIM), lambda i, j: (i, j, 0, 0)),
        out_shape=jax.ShapeDtypeStruct((b, 2 * KV_HEADS, n_half, HEAD_DIM), BF16),
        compiler_params=_cparams(2),
        name="nsa_compress",
    )(t, pe2, w1.astype(BF16), w2.astype(BF16))


def _cmp_attn_kernel(q_ref, kc_ref, vc_ref, ov_ref, gate_ref, o_ref, pen_ref, *, tq, n_slc):
    qi = pl.program_id(2)
    rows = GQA * tq
    q = q_ref[0].reshape(rows, HEAD_DIM)
    kc = kc_ref[0, 0]
    n_cmp = kc.shape[0]
    s = lax.dot_general(q, kc, (((1,), (1,)), ((), ())), preferred_element_type=F32) * SCALE
    row = lax.broadcasted_iota(jnp.int32, (rows, n_cmp), 0)
    pos = qi * tq + (row & (tq - 1))
    blk_end = lax.broadcasted_iota(jnp.int32, (rows, n_cmp), 1) * NSA_CMP_STRIDE + (NSA_CMP_BLOCK - 1)
    mask = blk_end <= pos
    s = jnp.where(mask, s, NEG_INF)
    m = jnp.max(s, axis=-1, keepdims=True)
    e = jnp.where(mask, jnp.exp(s - m), 0.0)
    den = jnp.maximum(jnp.sum(e, axis=-1, keepdims=True), TINY)
    pr = e / den
    o = jnp.dot(pr.astype(BF16), vc_ref[0, 0], preferred_element_type=F32)
    gates = gate_ref[0]
    first = pl.program_id(1) * GQA
    o_ref[0] = jnp.concatenate([o[g * tq:(g + 1) * tq] * _gate_column(gates, (first + g) * 3)
                                for g in range(GQA)], axis=-1)
    psum = pr[0:tq]
    for g in range(1, GQA):
        psum = psum + pr[g * tq:(g + 1) * tq]
    ov_t = ov_ref[...]
    hi = psum.astype(BF16)
    r1 = psum - hi.astype(F32)
    mid = r1.astype(BF16)
    lo = (r1 - mid.astype(F32)).astype(BF16)
    nt = (((1,), (1,)), ((), ()))
    imp_t = (lax.dot_general(ov_t, hi, nt, preferred_element_type=F32)
             + lax.dot_general(ov_t, mid, nt, preferred_element_type=F32)
             + lax.dot_general(ov_t, lo, nt, preferred_element_type=F32))
    jj = lax.broadcasted_iota(jnp.int32, (n_slc, tq), 0)
    cur = (qi * tq + lax.broadcasted_iota(jnp.int32, (n_slc, tq), 1)) // NSA_SLC_BLOCK
    forced = (jj == 0) | (jj == cur) | (jj == cur - 1)
    imp_t = jnp.where(forced, -NEG_INF, jnp.where(jj <= cur, imp_t, NEG_INF))
    picked = (_ranks_along_rows(imp_t) < NSA_SLC_TOPN) & (jj <= cur)
    pen_ref[0, 0] = _rows_to_lanes(jnp.where(picked, 0.0, NEG_INF), HEAD_DIM).astype(BF16)


def _nsa_compressed_attention(p, kvc, gates):
    b, _, s, _ = p.shape
    tq = 256
    n_cmp = kvc.shape[2]
    n_slc = s // NSA_SLC_BLOCK
    assert n_slc == HEAD_DIM
    c_start = np.arange(n_cmp) * NSA_CMP_STRIDE
    j_start = np.arange(n_slc) * NSA_SLC_BLOCK
    overlap = ((c_start[:, None] < j_start[None, :] + NSA_SLC_BLOCK)
               & (c_start[:, None] + NSA_CMP_BLOCK > j_start[None, :])).astype(np.float32)
    overlap[n_cmp - 1] = 0.0
    return pl.pallas_call(
        functools.partial(_cmp_attn_kernel, tq=tq, n_slc=n_slc),
        grid=(b, KV_HEADS, s // tq),
        in_specs=[pl.BlockSpec((1, GQA, tq, HEAD_DIM), lambda i, h, t: (i, QD // GQA + h, t, 0)),
                  pl.BlockSpec((1, 1, n_cmp, HEAD_DIM), lambda i, h, t: (i, h, 0, 0)),
                  pl.BlockSpec((1, 1, n_cmp, HEAD_DIM), lambda i, h, t: (i, KV_HEADS + h, 0, 0)),
                  pl.BlockSpec((n_slc, n_cmp), lambda i, h, t: (0, 0)),
                  pl.BlockSpec((1, tq, LANES), lambda i, h, t: (i, t, 0))],
        out_specs=[pl.BlockSpec((1, tq, GQA * HEAD_DIM), lambda i, h, t: (i, t, h)),
                   pl.BlockSpec((1, 1, tq, HEAD_DIM), lambda i, h, t: (i, h, t, 0))],
        out_shape=[jax.ShapeDtypeStruct((b, s, HEADS * HEAD_DIM), F32),
                   jax.ShapeDtypeStruct((b, KV_HEADS, s, HEAD_DIM), BF16)],
        compiler_params=_cparams(3),
        name="nsa_compressed",
    )(p, kvc, kvc, jnp.asarray(overlap.T, BF16), gates)


def _group_norm(o, g):
    ms = jnp.mean(o * o, axis=-1, keepdims=True)
    return (o * lax.rsqrt(ms + NORM_EPS)) * g


def _route(logits_t, bias_t):
    aff = jax.nn.sigmoid(logits_t)
    score = aff + bias_t
    rows = [score[e:e + 1] for e in range(N_EXPERTS)]
    affs = [aff[e:e + 1] for e in range(N_EXPERTS)]
    gscore = []
    for g in range(N_EXPERT_GROUPS):
        mem = rows[g * PER_GROUP:(g + 1) * PER_GROUP]
        best = None
        for a in range(PER_GROUP):
            for c in range(a + 1, PER_GROUP):
                pair = mem[a] + mem[c]
                best = pair if best is None else jnp.maximum(best, pair)
        gscore.append(best)
    gsel = jnp.zeros_like(gscore[0], dtype=jnp.int32)
    gbest = gscore[0]
    for g in range(1, N_EXPERT_GROUPS):
        better = gscore[g] > gbest
        gsel = jnp.where(better, g, gsel)
        gbest = jnp.where(better, gscore[g], gbest)
    masked = [jnp.where(gsel == e // PER_GROUP, rows[e], NEG_INF) for e in range(N_EXPERTS)]

    def argbest(vals):
        idx = jnp.zeros_like(gsel)
        best = vals[0]
        for e in range(1, N_EXPERTS):
            better = vals[e] > best
            idx = jnp.where(better, e, idx)
            best = jnp.where(better, vals[e], best)
        return idx

    e1 = argbest(masked)
    e2 = argbest([jnp.where(e1 == e, -jnp.inf, masked[e]) for e in range(N_EXPERTS)])
    w1 = sum(jnp.where(e1 == e, affs[e], 0.0) for e in range(N_EXPERTS))
    w2 = sum(jnp.where(e2 == e, affs[e], 0.0) for e in range(N_EXPERTS))
    tot = w1 + w2
    w1 = w1 / tot
    w2 = w2 / tot
    comb = jnp.concatenate([jnp.where(e1 == e, w1, 0.0) + jnp.where(e2 == e, w2, 0.0)
                            for e in range(N_EXPERTS)], axis=0)
    return comb, gsel


def _out_kernel(x_ref, oa_ref, ob_ref, oc_ref, dc_ref, ds_ref, dw_ref,
                mixg_ref, wout_ref, g1_ref, n2g_ref, sc2_ref, sh2_ref, rw_ref, rb_ref,
                xo_ref, h2_ref, gsel_ref):
    o_a = jnp.concatenate([oa_ref[0, h] for h in range(HEADS)], axis=-1)
    o_b = ob_ref[0]
    o_c = oc_ref[0]
    o_d = dc_ref[0] + ds_ref[0] + dw_ref[0]
    mixed = jnp.concatenate([_group_norm(o, mixg_ref[i:i + 1, :]).astype(BF16)
                             for i, o in enumerate((o_a, o_b, o_c, o_d))], axis=-1)
    y = jnp.dot(mixed, wout_ref[...], preferred_element_type=F32)
    x = x_ref[0] + g1_ref[0] * y
    xo_ref[0] = x
    ms = jnp.mean(x * x, axis=-1, keepdims=True)
    h2 = (x * lax.rsqrt(ms + NORM_EPS)) * n2g_ref[...] * (1.0 + sc2_ref[0]) + sh2_ref[0]
    logits_t = lax.dot_general(rw_ref[...], h2.astype(BF16), (((1,), (1,)), ((), ())),
                               preferred_element_type=F32)
    comb_t, gsel = _route(logits_t, rb_ref[...])
    h2_ref[0] = jnp.concatenate([h2, _rows_to_lanes(comb_t, LANES)], axis=-1)
    gsel_ref[0] = gsel


MIXER_OUT_TILE = 512


def _mixer_out(x, o_a, o_b, o_c, o_dc, o_ds, o_dw, mix_g, w_out, g1, n2g, sc2, sh2, rw_t, rb):
    b, s, d = x.shape
    tm = min(MIXER_OUT_TILE, s)
    head_spec = pl.BlockSpec((1, HEADS, tm, HEAD_DIM), lambda i, m: (i, 0, m, 0))
    mixer_spec = pl.BlockSpec((1, tm, GROUP_WIDTH), lambda i, m: (i, m, 0))
    vec_b = pl.BlockSpec((1, 1, d), lambda i, m: (i, 0, 0))
    return pl.pallas_call(
        _out_kernel,
        grid=(b, s // tm),
        in_specs=[pl.BlockSpec((1, tm, d), lambda i, m: (i, m, 0)),
                  head_spec, mixer_spec, mixer_spec, mixer_spec, mixer_spec, mixer_spec,
                  pl.BlockSpec((N_MIXERS, GROUP_WIDTH), lambda i, m: (0, 0)),
                  pl.BlockSpec((N_MIXERS * GROUP_WIDTH, d), lambda i, m: (0, 0), pipeline_mode=pl.Buffered(1)),
                  vec_b,
                  pl.BlockSpec((1, d), lambda i, m: (0, 0)),
                  vec_b, vec_b,
                  pl.BlockSpec((N_EXPERTS, d), lambda i, m: (0, 0)),
                  pl.BlockSpec((N_EXPERTS, 1), lambda i, m: (0, 0))],
        out_specs=[pl.BlockSpec((1, tm, d), lambda i, m: (i, m, 0)),
                   pl.BlockSpec((1, tm, d + LANES), lambda i, m: (i, m, 0)),
                   pl.BlockSpec((1, 1, tm), lambda i, m: (i, 0, m))],
        out_shape=[jax.ShapeDtypeStruct((b, s, d), F32),
                   jax.ShapeDtypeStruct((b, s, d + LANES), F32),
                   jax.ShapeDtypeStruct((b, 1, s), jnp.int32)],
        compiler_params=_cparams(2),
        name="mixer_out",
    )(x, o_a, o_b, o_c, o_dc, o_ds, o_dw, mix_g, w_out, g1.reshape(b, 1, d), n2g.reshape(1, d),
      sc2.reshape(b, 1, d), sh2.reshape(b, 1, d), rw_t, rb.reshape(N_EXPERTS, 1))


EXPERT_TILE = 512


def _dispatch_plan(gsel, tm):
    gid = gsel.reshape(-1)
    n_tok = gid.shape[0]
    onehot = (gid[:, None] == jnp.arange(N_EXPERT_GROUPS, dtype=jnp.int32)[None, :]).astype(jnp.int32)
    csum = jnp.cumsum(onehot, axis=0)
    rank = jnp.sum((csum - onehot) * onehot, axis=1)
    padded = ((csum[-1] + tm - 1) // tm) * tm
    ends = jnp.cumsum(padded)
    dest = jnp.sum(onehot * (ends - padded)[None, :], axis=1) + rank
    n_rows = n_tok + N_EXPERT_GROUPS * tm
    src = jnp.zeros((n_rows,), jnp.int32).at[dest].set(jnp.arange(n_tok, dtype=jnp.int32), unique_indices=True)
    tile_start = jnp.arange(n_rows // tm, dtype=jnp.int32) * tm
    tile_gid = jnp.minimum(jnp.sum((tile_start[:, None] >= ends[None, :]).astype(jnp.int32), axis=1),
                           N_EXPERT_GROUPS - 1)
    return dest.astype(jnp.int32), src, tile_gid, (ends[-1] // tm).reshape(1).astype(jnp.int32)


def _start_row_gather(idx_ref, base, src_hbm, buf, sem):
    def issue(r, c):
        pltpu.make_async_copy(src_hbm.at[pl.ds(idx_ref[base + r], 1)], buf.at[pl.ds(r, 1)], sem).start()
        return c

    lax.fori_loop(0, buf.shape[0], issue, 0, unroll=8)


def _wait_row_gather(src_hbm, buf, sem):
    pltpu.make_async_copy(src_hbm.at[pl.ds(0, buf.shape[0])], buf, sem).wait()


def _expert_kernel(src_ref, gid_ref, nvalid_ref, h_hbm, wgu_ref, wd_ref, o_ref, rows_buf, sems, h_scr, acc_ref, *, tm):
    t = pl.program_id(0)
    e = pl.program_id(1)
    slot = t % 2
    n_valid = nvalid_ref[0]

    @pl.when(t < n_valid)
    def _():
        @pl.when(e == 0)
        def _():
            @pl.when(t == 0)
            def _():
                _start_row_gather(src_ref, 0, h_hbm, rows_buf.at[0], sems.at[0])

            _wait_row_gather(h_hbm, rows_buf.at[slot], sems.at[slot])

            @pl.when(t + 1 < n_valid)
            def _():
                _start_row_gather(src_ref, (t + 1) * tm, h_hbm, rows_buf.at[1 - slot], sems.at[1 - slot])

            h_scr[...] = rows_buf[slot, :, :D_MODEL].astype(BF16)
            acc_ref[...] = jnp.zeros_like(acc_ref)

        gu = jnp.dot(h_scr[...], wgu_ref[0], preferred_element_type=F32)
        gate = gu[:, :D_EXPERT]
        he = (gate * jax.nn.sigmoid(gate)) * gu[:, D_EXPERT:]
        cw = rows_buf[slot, :, D_MODEL:]
        col = lax.broadcasted_iota(jnp.int32, cw.shape, 1)
        w = jnp.sum(jnp.where(col == gid_ref[t] * PER_GROUP + e, cw, 0.0), axis=-1, keepdims=True)
        acc_ref[...] += w * jnp.dot(he.astype(BF16), wd_ref[0], preferred_element_type=F32)

        @pl.when(e == PER_GROUP - 1)
        def _():
            o_ref[...] = acc_ref[...]

    @pl.when((t >= n_valid) & (e == PER_GROUP - 1))
    def _():
        o_ref[...] = jnp.zeros_like(o_ref)


def _experts(h2ext, src, tile_gid, n_valid, w_gu, w_d, tm):
    n_rows = src.shape[0]
    d = D_MODEL
    expert = lambda t, e, src_ref, gid, nv: (gid[t] * PER_GROUP + e, 0, 0)
    return pl.pallas_call(
        functools.partial(_expert_kernel, tm=tm),
        grid_spec=pltpu.PrefetchScalarGridSpec(
            num_scalar_prefetch=3,
            grid=(n_rows // tm, PER_GROUP),
            in_specs=[pl.BlockSpec(memory_space=pl.ANY),
                      pl.BlockSpec((1, d, 2 * D_EXPERT), expert),
                      pl.BlockSpec((1, D_EXPERT, d), expert)],
            out_specs=pl.BlockSpec((tm, d), lambda t, e, src_ref, gid, nv: (t, 0)),
            scratch_shapes=[pltpu.VMEM((2, tm, d + LANES), F32), pltpu.SemaphoreType.DMA((2,)),
                            pltpu.VMEM((tm, d), BF16), pltpu.VMEM((tm, d), F32)]),
        out_shape=jax.ShapeDtypeStruct((n_rows, d), F32),
        compiler_params=_cparams(2),
        name="experts",
    )(src, tile_gid, n_valid, h2ext, w_gu, w_d)


COMBINE_TILE = 512


def _combine_kernel(dest_ref, ys_hbm, x_ref, g2_ref, fg_ref, o_ref, rows_buf, sems, *, tm, final_norm):
    i = pl.program_id(0)
    slot = i % 2

    @pl.when(i == 0)
    def _():
        _start_row_gather(dest_ref, 0, ys_hbm, rows_buf.at[0], sems.at[0])

    _wait_row_gather(ys_hbm, rows_buf.at[slot], sems.at[slot])

    @pl.when(i + 1 < pl.num_programs(0))
    def _():
        _start_row_gather(dest_ref, (i + 1) * tm, ys_hbm, rows_buf.at[1 - slot], sems.at[1 - slot])

    x = x_ref[0] + g2_ref[0] * rows_buf[slot]
    if final_norm:
        ms = jnp.mean(x * x, axis=-1, keepdims=True)
        x = (x * lax.rsqrt(ms + NORM_EPS)) * fg_ref[...]
    o_ref[0] = x


def _combine(x, ys, dest, g2, final_g, final_norm):
    b, s, d = x.shape
    tm = min(COMBINE_TILE, s)
    per_seq = s // tm
    return pl.pallas_call(
        functools.partial(_combine_kernel, tm=tm, final_norm=final_norm),
        grid_spec=pltpu.PrefetchScalarGridSpec(
            num_scalar_prefetch=1,
            grid=(b * per_seq,),
            in_specs=[pl.BlockSpec(memory_space=pl.ANY),
                      pl.BlockSpec((1, tm, d), lambda i, dest_ref: (i // per_seq, i % per_seq, 0)),
                      pl.BlockSpec((1, 1, d), lambda i, dest_ref: (i // per_seq, 0, 0)),
                      pl.BlockSpec((1, d), lambda i, dest_ref: (0, 0))],
            out_specs=pl.BlockSpec((1, tm, d), lambda i, dest_ref: (i // per_seq, i % per_seq, 0)),
            scratch_shapes=[pltpu.VMEM((2, tm, d), F32), pltpu.SemaphoreType.DMA((2,))]),
        out_shape=jax.ShapeDtypeStruct((b, s, d), F32),
        compiler_params=_cparams(1),
        name="moe_combine",
    )(dest, ys, x, g2.reshape(b, 1, d), final_g.reshape(1, d))


def _moe(x, h2ext, gsel, w_gu, w_d, g2, final_g, final_norm):
    b, s, _ = x.shape
    dest, src, tile_gid, n_valid = _dispatch_plan(gsel, EXPERT_TILE)
    ys = _experts(h2ext.reshape(b * s, -1), src, tile_gid, n_valid, w_gu, w_d, EXPERT_TILE)
    return _combine(x, ys, dest, g2, final_g, final_norm)


def _bias_tables(rel_bias, s):
    tabs = [rel_bias[:, i * HEADS:(i + 1) * HEADS].T for i in range(N_MIXERS)]
    t = {}
    t["a"] = jnp.stack([_band_bias_tiles(tabs[0], window // r, DILATED_WINDOW, r, 1)
                        for window, r in DILATED_CONFIGS], axis=1)
    t["b"] = _band_bias_tiles(tabs[1], SWA_WINDOW - 1, 128, 1, GQA)
    t["c"] = _flash_bias_tiles(tabs[2], s)
    t["ds"] = _flash_bias_tiles(tabs[3], s)
    t["dw"] = _band_bias_tiles(tabs[3], NSA_WINDOW - 1, NSA_WINDOW, 1, GQA)
    return t


def _token_mixer_branches(p, gates, p_cmp, bias, sink, pe, w1, w2):
    b, _, s, _ = p.shape
    o_a = _dilated_attention(p, bias["a"])
    o_b = _banded_attention(p, p, p, bias["b"], grid01=(b, 1),
                            q_map=lambda i, j: (i, QB // HEADS), kv_maps=(lambda i, j: (i, KB // KV_HEADS),
                                                                         lambda i, j: (i, VB // KV_HEADS)),
                            bias_map=lambda i, j: 0, r_heads=GQA, units=KV_HEADS, w=128, out_heads=HEADS,
                            sink=sink, name="swa")
    o_c = _moba_attention(p, bias["c"])
    kvc = _nsa_compress(p_cmp, pe, w1, w2)
    o_dc, pen = _nsa_compressed_attention(p, kvc, gates)
    o_ds = _nsa_selected_attention(p, pen, gates, bias["ds"])
    o_dw = _banded_attention(p, p, p, bias["dw"], grid01=(b, KV_HEADS),
                             q_map=lambda i, j: (i, QD // GQA + j), kv_maps=(lambda i, j: (i, KDW + j),
                                                                            lambda i, j: (i, VDW + j)),
                             bias_map=lambda i, j: j, r_heads=GQA, units=1, w=NSA_WINDOW, out_heads=HEADS,
                             gates=gates, gate_branch=2, name="nsa_window")
    return o_a, o_b, o_c, o_dc, o_ds, o_dw


def _split_w_in(w_in_l):
    w_heads = w_in_l[:, :PROJ_COLS].astype(BF16)
    w_gate = jnp.pad(w_in_l[:, PROJ_COLS:], ((0, 0), (0, LANES - N_GATE_COLS))).astype(BF16)
    return w_heads, w_gate


def _sink_rows(sink_l):
    return jnp.repeat(sink_l.astype(F32).reshape(KV_HEADS, GQA), BAND_TILE, axis=1)[..., None]


def kernel(x, c, rel_bias, router_w, router_bias, norm1_g, norm2_g, ada_w, ada_b, w_in, nsa_pe_k, nsa_pe_v,
           nsa_cmp_w1_k, nsa_cmp_w2_k, nsa_cmp_w1_v, nsa_cmp_w2_v, sinks, mix_norm_g, w_out, exp_w_gate,
           exp_w_up, exp_w_down, final_g):
    b, s, d = x.shape
    mod = _modulation(c, ada_w, ada_b)
    bias = _bias_tables(rel_bias, s)
    rw_t = router_w.T.astype(BF16)
    for l in range(DEPTH):
        sh1, sc1, g1, sh2, sc2, g2 = jnp.split(mod[l], 6, axis=-1)
        w_heads, w_gate = _split_w_in(w_in[l])
        p, gates, p_cmp = _in_projection(x, norm1_g[l], sc1, sh1, w_heads, w_gate)
        pe = jnp.stack([nsa_pe_k[l], nsa_pe_v[l]])
        w1 = jnp.stack([nsa_cmp_w1_k[l], nsa_cmp_w1_v[l]])
        w2 = jnp.stack([nsa_cmp_w2_k[l], nsa_cmp_w2_v[l]])
        o_a, o_b, o_c, o_dc, o_ds, o_dw = _token_mixer_branches(p, gates, p_cmp, bias, _sink_rows(sinks[l]), pe,
                                                                   w1, w2)
        x, h2ext, gsel = _mixer_out(x, o_a, o_b, o_c, o_dc, o_ds, o_dw, mix_norm_g[l],
                                    w_out[l].astype(BF16), g1, norm2_g[l], sc2, sh2, rw_t, router_bias)
        w_gu = jnp.concatenate([exp_w_gate[l], exp_w_up[l]], axis=-1).astype(BF16)
        x = _moe(x, h2ext, gsel, w_gu, exp_w_down[l].astype(BF16), g2, final_g, l == DEPTH - 1)
    return x
```

```python
import functools
import math

import numpy as np
import jax
import jax.numpy as jnp
from jax import lax
from jax.experimental import pallas as pl
from jax.experimental.pallas import tpu as pltpu

F32 = jnp.float32
BF16 = jnp.bfloat16

D_MODEL = 2048
DEPTH = 4
HEAD_DIM = 64
N_MIXERS = 4
HEADS = 8
GROUP_WIDTH = HEADS * HEAD_DIM
SCALE = HEAD_DIM ** -0.5
REL_BUCKETS = 32
REL_MAX_DIST = 2048
DILATED_CONFIGS = ((128, 1), (512, 4), (2048, 16))
SWA_WINDOW = 128
KV_HEADS = 2
GQA = HEADS // KV_HEADS
MOBA_BLOCK = 256
MOBA_TOPK = 3
NSA_CMP_BLOCK = 32
NSA_CMP_STRIDE = 16
NSA_CMP_HIDDEN = 128
NSA_SLC_BLOCK = 64
NSA_SLC_TOPN = 16
NSA_WINDOW = 512
N_EXPERTS = 16
N_EXPERT_GROUPS = 4
PER_GROUP = N_EXPERTS // N_EXPERT_GROUPS
D_EXPERT = 512
NORM_EPS = 1e-6
NEG_INF = -1e30
TINY = float(np.finfo(np.float32).tiny)

N_PROJ_HEADS = 80
PROJ_COLS = N_PROJ_HEADS * HEAD_DIM
N_GATE_COLS = HEADS * 3
LANES = 128
SUBLANES = 8
QA, KA, VA = 0, 8, 16
QB, KB, VB = 24, 32, 34
QC, KC, VC = 36, 44, 52
QD, KDC, VDC, KDS, VDS, KDW, VDW = 60, 68, 70, 72, 74, 76, 78

VMEM_LIMIT = 56 * 1024 * 1024


def _cparams(n_axes):
    return pltpu.CompilerParams(dimension_semantics=("arbitrary",) * n_axes,
                                vmem_limit_bytes=VMEM_LIMIT)


def _mod_kernel(c_ref, w_ref, b_ref, o_ref):
    c = c_ref[...]
    ca = (c * jax.nn.sigmoid(c)).astype(BF16)
    o_ref[0] = jnp.dot(ca, w_ref[0].astype(BF16), preferred_element_type=F32) + b_ref[0]


def _modulation(c, ada_w, ada_b):
    depth, d, n = ada_w.shape
    b = c.shape[0]
    bp = 8
    tn = 1024
    cp = jnp.pad(c, ((0, bp - b), (0, 0)))
    out = pl.pallas_call(
        _mod_kernel,
        grid=(depth, n // tn),
        in_specs=[pl.BlockSpec((bp, d), lambda l, j: (0, 0)),
                  pl.BlockSpec((1, d, tn), lambda l, j: (l, 0, j)),
                  pl.BlockSpec((1, 1, tn), lambda l, j: (l, 0, j))],
        out_specs=pl.BlockSpec((1, bp, tn), lambda l, j: (l, 0, j)),
        out_shape=jax.ShapeDtypeStruct((depth, bp, n), F32),
        compiler_params=_cparams(2),
        name="adaln_mod",
    )(cp, ada_w, ada_b.reshape(depth, 1, n))
    return out[:, :b]


CMP_COL_GROUP = KDC // HEADS


def _inproj_kernel(x_ref, g_ref, sc_ref, sh_ref, w_ref, wg_ref, p_ref, gate_ref, cmp_ref, h_scr, res_scr):
    n = pl.program_id(2)

    @pl.when(n == 0)
    def _():
        x = x_ref[0]
        ms = jnp.mean(x * x, axis=-1, keepdims=True)
        h = (x * lax.rsqrt(ms + NORM_EPS)) * g_ref[...] * (1.0 + sc_ref[0]) + sh_ref[0]
        hb = h.astype(BF16)
        h_scr[...] = hb
        gate_ref[0] = jax.nn.sigmoid(jnp.dot(hb, wg_ref[...], preferred_element_type=F32))

    res = jnp.dot(h_scr[...], w_ref[...], preferred_element_type=F32)
    for j in range(HEADS):
        p_ref[0, j] = res[:, j * HEAD_DIM:(j + 1) * HEAD_DIM].astype(BF16)

    @pl.when(n == CMP_COL_GROUP)
    def _():
        first = (KDC % HEADS) * HEAD_DIM
        n_slab = 2 * KV_HEADS * HEAD_DIM // LANES
        for sl in range(n_slab):
            res_scr[sl] = res[:, first + sl * LANES:first + (sl + 1) * LANES]
        rows = res.shape[0] // NSA_CMP_STRIDE
        for sl in range(n_slab):
            taps = [res_scr[sl, pl.ds(i, rows, stride=NSA_CMP_STRIDE), :] for i in range(NSA_CMP_STRIDE)]
            for half in range(LANES // HEAD_DIM):
                lanes = slice(half * HEAD_DIM, (half + 1) * HEAD_DIM)
                cmp_ref[0, sl * (LANES // HEAD_DIM) + half] = jnp.concatenate(
                    [tap[:, lanes] for tap in taps], axis=-1).astype(BF16)


def _in_projection(x, g, sc, sh, w_heads, w_gate):
    b, s, d = x.shape
    tm = min(1024, s)
    tn = HEADS * HEAD_DIM
    n_cmp_heads = 2 * KV_HEADS
    half = NSA_CMP_STRIDE * HEAD_DIM
    return pl.pallas_call(
        _inproj_kernel,
        grid=(b, s // tm, PROJ_COLS // tn),
        in_specs=[pl.BlockSpec((1, tm, d), lambda i, m, n: (i, m, 0)),
                  pl.BlockSpec((1, d), lambda i, m, n: (0, 0)),
                  pl.BlockSpec((1, 1, d), lambda i, m, n: (i, 0, 0)),
                  pl.BlockSpec((1, 1, d), lambda i, m, n: (i, 0, 0)),
                  pl.BlockSpec((d, tn), lambda i, m, n: (0, n)),
                  pl.BlockSpec((d, LANES), lambda i, m, n: (0, 0))],
        out_specs=[pl.BlockSpec((1, HEADS, tm, HEAD_DIM), lambda i, m, n: (i, n, m, 0)),
                   pl.BlockSpec((1, tm, LANES), lambda i, m, n: (i, m, 0)),
                   pl.BlockSpec((1, n_cmp_heads, tm // NSA_CMP_STRIDE, half), lambda i, m, n: (i, 0, m, 0))],
        out_shape=[jax.ShapeDtypeStruct((b, N_PROJ_HEADS, s, HEAD_DIM), BF16),
                   jax.ShapeDtypeStruct((b, s, LANES), F32),
                   jax.ShapeDtypeStruct((b, n_cmp_heads, s // NSA_CMP_STRIDE, half), BF16)],
        scratch_shapes=[pltpu.VMEM((tm, d), BF16),
                        pltpu.VMEM((n_cmp_heads * HEAD_DIM // LANES, tm, LANES), F32)],
        compiler_params=_cparams(3),
        name="in_proj",
    )(x, g.reshape(1, d), sc.reshape(b, 1, d), sh.reshape(b, 1, d), w_heads, w_gate)


def _bucket_thresholds():
    d = np.arange(REL_MAX_DIST + 1)
    max_exact = REL_BUCKETS // 2
    large = max_exact + (np.log(np.maximum(d, 1).astype(np.float32) / np.float32(max_exact))
                         / np.float32(math.log(REL_MAX_DIST / max_exact))
                         * np.float32(REL_BUCKETS - max_exact)).astype(np.int32)
    bucket = np.where(d < max_exact, d, np.minimum(large, REL_BUCKETS - 1))
    return [int(np.argmax(bucket >= b)) for b in range(REL_BUCKETS)]


def _bias_kernel(off_ref, cmin_ref, tab_ref, o_ref, *, max_dist, dist_scale, thresholds):
    h = pl.program_id(0)
    t = pl.program_id(1)
    shape = o_ref.shape[-2:]
    rel = off_ref[t] + lax.broadcasted_iota(jnp.int32, shape, 0) - lax.broadcasted_iota(jnp.int32, shape, 1)
    acc = jnp.full(shape, tab_ref[h, 0], F32)
    for b in range(1, REL_BUCKETS):
        acc = jnp.where(rel >= -(-thresholds[b] // dist_scale), tab_ref[h, b], acc)
    col = lax.broadcasted_iota(jnp.int32, shape, 1)
    valid = (rel >= 0) & (rel <= max_dist) & (col >= cmin_ref[t])
    o_ref[...] = jnp.where(valid, acc, NEG_INF).reshape(o_ref.shape)


def _bias_tiles(tab, offsets, col_mins, rows, cols, max_dist, dist_scale, group):
    n_heads = tab.shape[0]
    n_tiles = len(offsets)
    smem = pl.BlockSpec(memory_space=pltpu.SMEM)
    out = pl.pallas_call(
        functools.partial(_bias_kernel, max_dist=max_dist, dist_scale=dist_scale,
                          thresholds=_bucket_thresholds()),
        grid=(n_heads, n_tiles),
        in_specs=[smem, smem, smem],
        out_specs=pl.BlockSpec((1, 1, 1, rows, cols), lambda h, t: (h // group, t, h % group, 0, 0)),
        out_shape=jax.ShapeDtypeStruct((n_heads // group, n_tiles, group, rows, cols), F32),
        compiler_params=_cparams(2),
        name="bias_tiles",
    )(jnp.asarray(offsets, jnp.int32), jnp.asarray(col_mins, jnp.int32), tab.astype(F32))
    return out.reshape(n_heads // group, n_tiles, group * rows, cols)


BAND_TILE = 128
BAND_STEP = 512


def _gate_column(gates, idx):
    col = lax.broadcasted_iota(jnp.int32, gates.shape, 1)
    return jnp.sum(jnp.where(col == idx, gates, 0.0), axis=-1, keepdims=True)


def _banded_kernel(*refs, units, r_heads, nsub, w, has_sink, gate_branch):
    q_ref, kp_ref, kc_ref, vp_ref, vc_ref, b_ref = refs[:6]
    pos = 6
    sink_ref = gate_ref = None
    if has_sink:
        sink_ref = refs[pos]
        pos += 1
    if gate_branch is not None:
        gate_ref = refs[pos]
        pos += 1
    o_ref, kw_scr, vw_scr = refs[pos:pos + 3]
    t = pl.program_id(2)
    tq = BAND_TILE
    rows = r_heads * tq
    nk = w + tq
    kw_scr[:, :w] = kp_ref[0]
    kw_scr[:, w:] = kc_ref[0]
    vw_scr[:, :w, :HEAD_DIM] = vp_ref[0]
    vw_scr[:, w:, :HEAD_DIM] = vc_ref[0]
    vw_scr[:, :, HEAD_DIM:] = jnp.ones(vw_scr.shape[:2] + (HEAD_DIM,), BF16)
    for u in range(units):
        for j in range(nsub):
            q = q_ref[0, u * r_heads:(u + 1) * r_heads, j * tq:(j + 1) * tq, :].reshape(rows, HEAD_DIM) * SCALE
            k = kw_scr[u, j * tq:j * tq + nk, :]
            v = vw_scr[u, j * tq:j * tq + nk, :]
            tile = jnp.minimum(t * nsub + j, w // tq)
            s = lax.dot_general(q, k, (((1,), (1,)), ((), ())), preferred_element_type=F32) + b_ref[u, tile]
            m = jnp.max(s, axis=-1, keepdims=True)
            if has_sink:
                m = jnp.maximum(m, sink_ref[u])
            e = jnp.exp(s - m)
            num_den = jnp.dot(e.astype(BF16), v, preferred_element_type=F32)
            den = num_den[:, HEAD_DIM:]
            if has_sink:
                den = den + jnp.exp(sink_ref[u] - m)
            o = num_den[:, :HEAD_DIM] / jnp.maximum(den, TINY)
            parts = []
            for g in range(r_heads):
                og = o[g * tq:(g + 1) * tq]
                if gate_ref is not None:
                    head = (pl.program_id(1) * units + u) * r_heads + g
                    og = og * _gate_column(gate_ref[0, j * tq:(j + 1) * tq, :], head * 3 + gate_branch)
                parts.append(og)
            lanes = slice(u * r_heads * HEAD_DIM, (u + 1) * r_heads * HEAD_DIM)
            o_ref[0, j * tq:(j + 1) * tq, lanes] = jnp.concatenate(parts, axis=-1)


def _band_bias_tiles(tab, max_dist, w, dist_scale, group):
    tq = BAND_TILE
    n_early = w // tq
    return _bias_tiles(tab, [w] * (n_early + 1), [w - e * tq for e in range(n_early)] + [0], tq, w + tq,
                       max_dist, dist_scale, group)


def _banded_attention(q_arr, k_arr, v_arr, bias, *, grid01, q_map, kv_maps, bias_map, r_heads, units, w,
                      out_heads, sink=None, gates=None, gate_branch=None, name):
    seq = q_arr.shape[2]
    tq = BAND_TILE
    step = min(BAND_STEP, seq)
    nsub = step // tq
    assert seq % step == 0 and w % tq == 0 and step % w == 0
    k_map, v_map = kv_maps
    prev = lambda t: jnp.maximum(t * (step // w) - 1, 0)
    in_specs = [pl.BlockSpec((1, units * r_heads, step, HEAD_DIM), lambda i, j, t: q_map(i, j) + (t, 0)),
                pl.BlockSpec((1, units, w, HEAD_DIM), lambda i, j, t: k_map(i, j) + (prev(t), 0)),
                pl.BlockSpec((1, units, step, HEAD_DIM), lambda i, j, t: k_map(i, j) + (t, 0)),
                pl.BlockSpec((1, units, w, HEAD_DIM), lambda i, j, t: v_map(i, j) + (prev(t), 0)),
                pl.BlockSpec((1, units, step, HEAD_DIM), lambda i, j, t: v_map(i, j) + (t, 0)),
                pl.BlockSpec((units, w // tq + 1, r_heads * tq, w + tq), lambda i, j, t: (bias_map(i, j), 0, 0, 0))]
    args = [q_arr, k_arr, k_arr, v_arr, v_arr, bias]
    if sink is not None:
        in_specs.append(pl.BlockSpec((units, r_heads * tq, 1), lambda i, j, t: (bias_map(i, j), 0, 0)))
        args.append(sink)
    if gates is not None:
        in_specs.append(pl.BlockSpec((1, step, LANES), lambda i, j, t: (i, t, 0)))
        args.append(gates)
    n0, n1 = grid01
    return pl.pallas_call(
        functools.partial(_banded_kernel, units=units, r_heads=r_heads, nsub=nsub, w=w,
                          has_sink=sink is not None, gate_branch=gate_branch if gates is not None else None),
        grid=(n0, n1, seq // step),
        in_specs=in_specs,
        out_specs=pl.BlockSpec((1, step, units * r_heads * HEAD_DIM), lambda i, j, t: (i, t, j)),
        out_shape=jax.ShapeDtypeStruct((n0, seq, out_heads * HEAD_DIM), F32),
        scratch_shapes=[pltpu.VMEM((units, w + step, HEAD_DIM), BF16),
                        pltpu.VMEM((units, w + step, 2 * HEAD_DIM), BF16)],
        compiler_params=_cparams(3),
        name=name,
    )(*args)


DILATED_WINDOW = 128


DEINTERLEAVE = 4


def _dilated_kernel(q_ref, k_ref, v_ref, b_ref, o_ref, q_scr, kv_scr, q4_scr, kv4_scr, o_scr, l_scr, *, seq, pad):
    tq = BAND_TILE
    w = DILATED_WINDOW
    nt = (((1,), (1,)), ((), ()))
    zeros = jnp.zeros((seq, HEAD_DIM), F32)
    q_scr[...] = jnp.concatenate([q_ref[0, 0].astype(F32) * SCALE, zeros], axis=-1)
    kv_scr[:pad] = jnp.zeros((pad, 2 * HEAD_DIM), F32)
    kv_scr[pad:] = jnp.concatenate([k_ref[0, 0].astype(F32), v_ref[0, 0].astype(F32)], axis=-1)

    def rows(start, n, r):
        return pl.ds(start, n) if r == 1 else pl.ds(start, n, stride=r)

    sub = seq // DEINTERLEAVE
    pad4 = pad // DEINTERLEAVE
    for c in range(DEINTERLEAVE):
        q4_scr[c] = q_scr[pl.ds(c, sub, stride=DEINTERLEAVE), :]
        kv4_scr[c, :pad4] = jnp.zeros((pad4, 2 * HEAD_DIM), F32)
        kv4_scr[c, pad4:] = kv_scr[pl.ds(pad + c, sub, stride=DEINTERLEAVE), :]

    def tile(cfg, r, j, t):
        first = j + r * (t * tq)
        if r % DEINTERLEAVE == 0:
            rr = r // DEINTERLEAVE
            c = j % DEINTERLEAVE
            first4 = j // DEINTERLEAVE + rr * (t * tq)
            q = q4_scr[c, rows(first4, tq, rr), :].astype(BF16)
            kv = kv4_scr[c, rows(pad4 + first4 - rr * w, w + tq, rr), :].astype(BF16)
        else:
            q = q_scr[rows(first, tq, r), :].astype(BF16)
            kv = kv_scr[rows(pad + first - r * w, w + tq, r), :].astype(BF16)
        s = lax.dot_general(q, kv, nt, preferred_element_type=F32) + b_ref[0, cfg, jnp.minimum(t, 1)]
        m = jnp.max(s, axis=-1, keepdims=True)
        e = jnp.exp(s - m)
        den = jnp.maximum(jnp.sum(e, axis=-1, keepdims=True), TINY)
        o_scr[cfg, rows(first, tq, r), :] = jnp.dot(e.astype(BF16), kv, preferred_element_type=F32) / den
        l_scr[cfg, rows(first, tq, r), :] = jnp.broadcast_to(m + jnp.log(den), (tq, 2 * HEAD_DIM))

    group = 16
    for cfg, (_, r) in enumerate(DILATED_CONFIGS):
        nsub = seq // r // tq

        def body(g, c, cfg=cfg, r=r, nsub=nsub):
            for u in range(group):
                tid = g * group + u
                tile(cfg, r, tid // nsub, tid % nsub)
            return c

        lax.fori_loop(0, r * nsub // group, body, 0)

    chunk = 512
    for c in range(seq // chunk):
        sl = slice(c * chunk, (c + 1) * chunk)
        ls = [l_scr[i, sl] for i in range(len(DILATED_CONFIGS))]
        mx = functools.reduce(jnp.maximum, ls)
        ws = [jnp.exp(l - mx) for l in ls]
        mixed = sum(wi * o_scr[i, sl] for i, wi in enumerate(ws)) / sum(ws)
        o_ref[0, 0, sl, :] = mixed[:, HEAD_DIM:]


def _dilated_attention(p, bias):
    b, _, s, _ = p.shape
    pad = max(r for _, r in DILATED_CONFIGS) * DILATED_WINDOW
    assert all(win // r == DILATED_WINDOW and (s // r) % BAND_TILE == 0 for win, r in DILATED_CONFIGS)
    head_spec = lambda base: pl.BlockSpec((1, 1, s, HEAD_DIM), lambda i, h: (i, base + h, 0, 0))
    n_cfg = len(DILATED_CONFIGS)
    return pl.pallas_call(
        functools.partial(_dilated_kernel, seq=s, pad=pad),
        grid=(b, HEADS),
        in_specs=[head_spec(QA), head_spec(KA), head_spec(VA),
                  pl.BlockSpec((1, n_cfg, 2, BAND_TILE, DILATED_WINDOW + BAND_TILE), lambda i, h: (h, 0, 0, 0, 0))],
        out_specs=pl.BlockSpec((1, 1, s, HEAD_DIM), lambda i, h: (i, h, 0, 0)),
        out_shape=jax.ShapeDtypeStruct((b, HEADS, s, HEAD_DIM), F32),
        scratch_shapes=[pltpu.VMEM((s, 2 * HEAD_DIM), F32),
                        pltpu.VMEM((pad + s, 2 * HEAD_DIM), F32),
                        pltpu.VMEM((DEINTERLEAVE, s // DEINTERLEAVE, 2 * HEAD_DIM), F32),
                        pltpu.VMEM((DEINTERLEAVE, (pad + s) // DEINTERLEAVE, 2 * HEAD_DIM), F32),
                        pltpu.VMEM((n_cfg, s, 2 * HEAD_DIM), F32),
                        pltpu.VMEM((n_cfg, s, 2 * HEAD_DIM), F32)],
        compiler_params=_cparams(2),
        name="dilated",
    )(p, p, p, bias)


def _ranks_along_rows(vals):
    n, t = vals.shape
    groups = [vals[g * SUBLANES:(g + 1) * SUBLANES] for g in range(n // SUBLANES)]
    row = lax.broadcasted_iota(jnp.int32, (SUBLANES, t), 0)
    later = [jnp.where(row > r, 1, 0) for r in range(SUBLANES)]
    cnt = [jnp.zeros((SUBLANES, t), jnp.int32) for _ in groups]
    for i in range(n):
        vi = vals[i:i + 1]
        for g, vg in enumerate(groups):
            if g > i // SUBLANES:
                cnt[g] = cnt[g] + jnp.where(vi >= vg, 1, 0)
            elif g < i // SUBLANES:
                cnt[g] = cnt[g] + jnp.where(vi > vg, 1, 0)
            else:
                cnt[g] = cnt[g] + jnp.where(vi > vg, 1, 0) + jnp.where(vi == vg, later[i % SUBLANES], 0)
    return jnp.concatenate(cnt, axis=0)


def _rows_to_lanes(x_t, width):
    n, t = x_t.shape
    padded = jnp.concatenate([x_t, jnp.zeros((LANES - n, t), x_t.dtype)], axis=0)
    return padded.T[:, :width]


def _tile_groups(n, fn):
    def body(j, c):
        fn(j * 4, 4)
        return c

    lax.fori_loop(0, n // 4, body, 0)
    base = (n // 4) * 4

    @pl.when((n & 2) != 0)
    def _():
        fn(base, 2)

    @pl.when((n & 1) != 0)
    def _():
        fn(base + (n & 2), 1)


def _flash_selected(qa_scr, qi, kaug_scr, vaug_scr, kv_of, b_ref, s_scr, m_scr, acc_scr, tq):
    nh = qa_scr.shape[0]
    half = tq // 2
    n = qi + 1
    nt = (((1,), (1,)), ((), ()))
    m_scr[...] = jnp.full(m_scr.shape, NEG_INF, F32)

    def logits(kj0, cnt):
        ks = pl.multiple_of(kj0 * tq, tq)
        for h in range(nh):
            s = lax.dot_general(qa_scr[h], kaug_scr[kv_of[h], pl.ds(ks, cnt * tq), :], nt,
                                preferred_element_type=F32)
            mloc = None
            for u in range(cnt):
                su = s[:, u * tq:(u + 1) * tq] + b_ref[h, jnp.minimum(qi - kj0 - u, b_ref.shape[1] - 1)]
                s_scr[h, kj0 + u] = su
                mu = jnp.maximum(su[:, :half], su[:, half:])
                mloc = mu if mloc is None else jnp.maximum(mloc, mu)
            m_scr[h] = jnp.maximum(m_scr[h], mloc)

    _tile_groups(n, logits)
    for h in range(nh):
        m_scr[h] = jnp.broadcast_to(jnp.max(m_scr[h], axis=-1, keepdims=True), (tq, half))
    acc_scr[...] = jnp.zeros(acc_scr.shape, F32)

    def weigh(kj0, cnt):
        ks = pl.multiple_of(kj0 * tq, tq)
        for h in range(nh):
            mb = m_scr[h]
            parts = []
            for u in range(cnt):
                su = s_scr[h, kj0 + u]
                parts += [jnp.exp(su[:, :half] - mb).astype(BF16), jnp.exp(su[:, half:] - mb).astype(BF16)]
            acc_scr[h] += jnp.dot(jnp.concatenate(parts, axis=-1), vaug_scr[kv_of[h], pl.ds(ks, cnt * tq), :],
                                  preferred_element_type=F32)

    _tile_groups(n, weigh)
    outs = []
    for h in range(nh):
        acc = acc_scr[h]
        outs.append(acc[:, :HEAD_DIM] / jnp.maximum(acc[:, HEAD_DIM:], TINY))
    return outs


def _fill_kv_aug(kaug_scr, vaug_scr, k_ref, v_ref, et_ref):
    for kv in range(kaug_scr.shape[0]):
        kaug_scr[kv, :, :HEAD_DIM] = k_ref[0, kv]
        kaug_scr[kv, :, HEAD_DIM:] = et_ref[...]
        vaug_scr[kv, :, :HEAD_DIM] = v_ref[0, kv]
        vaug_scr[kv, :, HEAD_DIM:] = jnp.ones((vaug_scr.shape[1], HEAD_DIM), BF16)


def _augment(q, pen):
    return jnp.concatenate([q * SCALE, pen.astype(BF16)], axis=-1)


def _moba_kernel(q_ref, k_ref, v_ref, pool_ref, et_ref, b_ref, o_ref, kmean_scr, kaug_scr, vaug_scr, qa_scr, s_scr,
                 m_scr, acc_scr, *, tq, nblk, nh):
    qi = pl.program_id(2)

    @pl.when(qi == 0)
    def _():
        _fill_kv_aug(kaug_scr, vaug_scr, k_ref, v_ref, et_ref)
        for h in range(nh):
            kmean_scr[h] = jnp.dot(pool_ref[...], k_ref[0, h], preferred_element_type=F32).astype(BF16)

    for h in range(nh):
        q = q_ref[0, h]
        gate = lax.dot_general(kmean_scr[h], q, (((1,), (1,)), ((), ())), preferred_element_type=F32)
        blk = lax.broadcasted_iota(jnp.int32, (nblk, tq), 0)
        gate = jnp.where(blk < qi, gate, NEG_INF)
        picked = ((_ranks_along_rows(gate) < MOBA_TOPK) & (blk < qi)) | (blk == qi)
        pen_t = jnp.where(picked, 0.0, NEG_INF)
        qa_scr[h] = _augment(q, _rows_to_lanes(pen_t, HEAD_DIM))
    outs = _flash_selected(qa_scr, qi, kaug_scr, vaug_scr, tuple(range(nh)), b_ref, s_scr, m_scr, acc_scr, tq)
    o_ref[0] = jnp.concatenate(outs, axis=-1)


def _slc_kernel(q_ref, k_ref, v_ref, pen_ref, gate_ref, et_ref, b_ref, o_ref, kaug_scr, vaug_scr, qa_scr, s_scr,
                m_scr, acc_scr, *, tq, nh):
    qi = pl.program_id(2)

    @pl.when(qi == 0)
    def _():
        _fill_kv_aug(kaug_scr, vaug_scr, k_ref, v_ref, et_ref)

    for h in range(nh):
        qa_scr[h] = _augment(q_ref[0, h], pen_ref[0, 0])
    outs = _flash_selected(qa_scr, qi, kaug_scr, vaug_scr, (0,) * nh, b_ref, s_scr, m_scr, acc_scr, tq)
    gates = gate_ref[0]
    first = pl.program_id(1) * nh
    o_ref[0] = jnp.concatenate([outs[h] * _gate_column(gates, (first + h) * 3 + 1) for h in range(nh)], axis=-1)


def _flash_state(n_kv, seq, nh, tq):
    return [pltpu.VMEM((n_kv, seq, 2 * HEAD_DIM), BF16),
            pltpu.VMEM((n_kv, seq, 2 * HEAD_DIM), BF16),
            pltpu.VMEM((nh, tq, 2 * HEAD_DIM), BF16),
            pltpu.VMEM((nh, seq // tq, tq, tq), F32),
            pltpu.VMEM((nh, tq, tq // 2), F32),
            pltpu.VMEM((nh, tq, 2 * HEAD_DIM), F32)]


FLASH_TILE = 256
MOBA_HEADS_PER_STEP = 4
SLC_HEADS_PER_STEP = GQA


def _flash_bias_tiles(tab, seq):
    tq = FLASH_TILE
    first_const = -(-(_bucket_thresholds()[-1] + tq - 1) // tq)
    n_off = min(seq // tq, first_const + 1)
    return _bias_tiles(tab, [o * tq for o in range(n_off)], [0] * n_off, tq, tq, seq, 1, 1)


def _block_onehot_t(nblk, seq):
    blk = seq // nblk
    return jnp.asarray((np.arange(seq)[:, None] // blk == np.arange(HEAD_DIM)[None, :]).astype(np.float32), BF16)


def _moba_attention(p, bias):
    b, _, s, _ = p.shape
    tq, nh = FLASH_TILE, MOBA_HEADS_PER_STEP
    assert tq == MOBA_BLOCK
    nblk = s // tq
    nrow = max(nblk, 16)
    n_bias = bias.shape[1]
    pool = jnp.asarray((np.arange(s)[None, :] // tq == np.arange(nrow)[:, None]).astype(np.float32) / tq, BF16)
    once = pl.Buffered(1)
    kv_spec = lambda base: pl.BlockSpec((1, nh, s, HEAD_DIM), lambda i, h, t: (i, base // nh + h, 0, 0),
                                        pipeline_mode=once)
    return pl.pallas_call(
        functools.partial(_moba_kernel, tq=tq, nblk=nrow, nh=nh),
        grid=(b, HEADS // nh, s // tq),
        in_specs=[pl.BlockSpec((1, nh, tq, HEAD_DIM), lambda i, h, t: (i, QC // nh + h, t, 0)),
                  kv_spec(KC), kv_spec(VC),
                  pl.BlockSpec((nrow, s), lambda i, h, t: (0, 0), pipeline_mode=once),
                  pl.BlockSpec((s, HEAD_DIM), lambda i, h, t: (0, 0), pipeline_mode=once),
                  pl.BlockSpec((nh, n_bias, tq, tq), lambda i, h, t: (h, 0, 0, 0), pipeline_mode=once)],
        out_specs=pl.BlockSpec((1, tq, nh * HEAD_DIM), lambda i, h, t: (i, t, h)),
        out_shape=jax.ShapeDtypeStruct((b, s, HEADS * HEAD_DIM), F32),
        scratch_shapes=[pltpu.VMEM((nh, nrow, HEAD_DIM), BF16)] + _flash_state(nh, s, nh, tq),
        compiler_params=_cparams(3),
        name="moba",
    )(p, p, p, pool, _block_onehot_t(nblk, s), bias)


def _nsa_selected_attention(p, pen, gates, bias):
    b, _, s, _ = p.shape
    tq, nh = FLASH_TILE, SLC_HEADS_PER_STEP
    nblk = s // NSA_SLC_BLOCK
    n_bias = bias.shape[1]
    assert nblk <= HEAD_DIM and GQA % nh == 0
    kv_spec = lambda base: pl.BlockSpec((1, 1, s, HEAD_DIM), lambda i, h, t: (i, base + (h * nh) // GQA, 0, 0))
    return pl.pallas_call(
        functools.partial(_slc_kernel, tq=tq, nh=nh),
        grid=(b, HEADS // nh, s // tq),
        in_specs=[pl.BlockSpec((1, nh, tq, HEAD_DIM), lambda i, h, t: (i, QD // nh + h, t, 0)),
                  kv_spec(KDS), kv_spec(VDS),
                  pl.BlockSpec((1, 1, tq, HEAD_DIM), lambda i, h, t: (i, (h * nh) // GQA, t, 0)),
                  pl.BlockSpec((1, tq, LANES), lambda i, h, t: (i, t, 0)),
                  pl.BlockSpec((s, HEAD_DIM), lambda i, h, t: (0, 0)),
                  pl.BlockSpec((nh, n_bias, tq, tq), lambda i, h, t: (h, 0, 0, 0))],
        out_specs=pl.BlockSpec((1, tq, nh * HEAD_DIM), lambda i, h, t: (i, t, h)),
        out_shape=jax.ShapeDtypeStruct((b, s, HEADS * HEAD_DIM), F32),
        scratch_shapes=_flash_state(1, s, nh, tq),
        compiler_params=_cparams(3),
        name="nsa_selected",
    )(p, p, p, pen, gates, _block_onehot_t(nblk, s), bias)


def _compress_kernel(t_ref, pe_ref, w1_ref, w2_ref, o_ref):
    t = t_ref[0, 0].astype(F32)
    half = t.shape[-1]
    top = (t + pe_ref[0, 0:1, :]).astype(BF16)
    bot = (t + pe_ref[0, 1:2, :]).astype(BF16)
    u = jnp.dot(top, w1_ref[0, :half, :], preferred_element_type=F32)
    vv = jnp.dot(bot, w1_ref[0, half:, :], preferred_element_type=F32)
    pre = u + pltpu.roll(vv, vv.shape[0] - 1, axis=0)
    hid = jax.nn.gelu(pre).astype(BF16)
    o_ref[0, 0] = jnp.dot(hid, w2_ref[0], preferred_element_type=F32).astype(BF16)


def _nsa_compress(t, pe, w1, w2):
    b, _, n_half, half = t.shape
    pe2 = pe.reshape(2, 2, half)
    return pl.pallas_call(
        _compress_kernel,
        grid=(b, 2 * KV_HEADS),
        in_specs=[pl.BlockSpec((1, 1, n_half, half), lambda i, j: (i, j, 0, 0)),
                  pl.BlockSpec((1, 2, half), lambda i, j: (j // KV_HEADS, 0, 0)),
                  pl.BlockSpec((1, 2 * half, NSA_CMP_HIDDEN), lambda i, j: (j // KV_HEADS, 0, 0)),
                  pl.BlockSpec((1, NSA_CMP_HIDDEN, HEAD_DIM), lambda i, j: (j // KV_HEADS, 0, 0))],
        out_specs=pl.BlockSpec((1, 1, n_half, HEAD_DIM), lambda i, j: (i, j, 0, 0)),
        out_shape=jax.ShapeDtypeStruct((b, 2 * KV_HEADS, n_half, HEAD_DIM), BF16),
        compiler_params=_cparams(2),
        name="nsa_compress",
    )(t, pe2, w1.astype(BF16), w2.astype(BF16))


def _cmp_attn_kernel(q_ref, kc_ref, vc_ref, ov_ref, gate_ref, o_ref, pen_ref, *, tq, n_slc):
    qi = pl.program_id(2)
    rows = GQA * tq
    q = q_ref[0].reshape(rows, HEAD_DIM)
    kc = kc_ref[0, 0]
    n_cmp = kc.shape[0]
    s = lax.dot_general(q, kc, (((1,), (1,)), ((), ())), preferred_element_type=F32) * SCALE
    row = lax.broadcasted_iota(jnp.int32, (rows, n_cmp), 0)
    pos = qi * tq + (row & (tq - 1))
    blk_end = lax.broadcasted_iota(jnp.int32, (rows, n_cmp), 1) * NSA_CMP_STRIDE + (NSA_CMP_BLOCK - 1)
    mask = blk_end <= pos
    s = jnp.where(mask, s, NEG_INF)
    m = jnp.max(s, axis=-1, keepdims=True)
    e = jnp.where(mask, jnp.exp(s - m), 0.0)
    den = jnp.maximum(jnp.sum(e, axis=-1, keepdims=True), TINY)
    pr = e / den
    o = jnp.dot(pr.astype(BF16), vc_ref[0, 0], preferred_element_type=F32)
    gates = gate_ref[0]
    first = pl.program_id(1) * GQA
    o_ref[0] = jnp.concatenate([o[g * tq:(g + 1) * tq] * _gate_column(gates, (first + g) * 3)
                                for g in range(GQA)], axis=-1)
    psum = pr[0:tq]
    for g in range(1, GQA):
        psum = psum + pr[g * tq:(g + 1) * tq]
    ov_t = ov_ref[...]
    hi = psum.astype(BF16)
    r1 = psum - hi.astype(F32)
    mid = r1.astype(BF16)
    lo = (r1 - mid.astype(F32)).astype(BF16)
    nt = (((1,), (1,)), ((), ()))
    imp_t = (lax.dot_general(ov_t, hi, nt, preferred_element_type=F32)
             + lax.dot_general(ov_t, mid, nt, preferred_element_type=F32)
             + lax.dot_general(ov_t, lo, nt, preferred_element_type=F32))
    jj = lax.broadcasted_iota(jnp.int32, (n_slc, tq), 0)
    cur = (qi * tq + lax.broadcasted_iota(jnp.int32, (n_slc, tq), 1)) // NSA_SLC_BLOCK
    forced = (jj == 0) | (jj == cur) | (jj == cur - 1)
    imp_t = jnp.where(forced, -NEG_INF, jnp.where(jj <= cur, imp_t, NEG_INF))
    picked = (_ranks_along_rows(imp_t) < NSA_SLC_TOPN) & (jj <= cur)
    pen_ref[0, 0] = _rows_to_lanes(jnp.where(picked, 0.0, NEG_INF), HEAD_DIM).astype(BF16)


def _nsa_compressed_attention(p, kvc, gates):
    b, _, s, _ = p.shape
    tq = 256
    n_cmp = kvc.shape[2]
    n_slc = s // NSA_SLC_BLOCK
    assert n_slc == HEAD_DIM
    c_start = np.arange(n_cmp) * NSA_CMP_STRIDE
    j_start = np.arange(n_slc) * NSA_SLC_BLOCK
    overlap = ((c_start[:, None] < j_start[None, :] + NSA_SLC_BLOCK)
               & (c_start[:, None] + NSA_CMP_BLOCK > j_start[None, :])).astype(np.float32)
    overlap[n_cmp - 1] = 0.0
    return pl.pallas_call(
        functools.partial(_cmp_attn_kernel, tq=tq, n_slc=n_slc),
        grid=(b, KV_HEADS, s // tq),
        in_specs=[pl.BlockSpec((1, GQA, tq, HEAD_DIM), lambda i, h, t: (i, QD // GQA + h, t, 0)),
                  pl.BlockSpec((1, 1, n_cmp, HEAD_DIM), lambda i, h, t: (i, h, 0, 0)),
                  pl.BlockSpec((1, 1, n_cmp, HEAD_DIM), lambda i, h, t: (i, KV_HEADS + h, 0, 0)),
                  pl.BlockSpec((n_slc, n_cmp), lambda i, h, t: (0, 0)),
                  pl.BlockSpec((1, tq, LANES), lambda i, h, t: (i, t, 0))],
        out_specs=[pl.BlockSpec((1, tq, GQA * HEAD_DIM), lambda i, h, t: (i, t, h)),
                   pl.BlockSpec((1, 1, tq, HEAD_DIM), lambda i, h, t: (i, h, t, 0))],
        out_shape=[jax.ShapeDtypeStruct((b, s, HEADS * HEAD_DIM), F32),
                   jax.ShapeDtypeStruct((b, KV_HEADS, s, HEAD_DIM), BF16)],
        compiler_params=_cparams(3),
        name="nsa_compressed",
    )(p, kvc, kvc, jnp.asarray(overlap.T, BF16), gates)


def _group_norm(o, g):
    ms = jnp.mean(o * o, axis=-1, keepdims=True)
    return (o * lax.rsqrt(ms + NORM_EPS)) * g


def _route(logits_t, bias_t):
    aff = jax.nn.sigmoid(logits_t)
    score = aff + bias_t
    rows = [score[e:e + 1] for e in range(N_EXPERTS)]
    affs = [aff[e:e + 1] for e in range(N_EXPERTS)]
    gscore = []
    for g in range(N_EXPERT_GROUPS):
        mem = rows[g * PER_GROUP:(g + 1) * PER_GROUP]
        best = None
        for a in range(PER_GROUP):
            for c in range(a + 1, PER_GROUP):
                pair = mem[a] + mem[c]
                best = pair if best is None else jnp.maximum(best, pair)
        gscore.append(best)
    gsel = jnp.zeros_like(gscore[0], dtype=jnp.int32)
    gbest = gscore[0]
    for g in range(1, N_EXPERT_GROUPS):
        better = gscore[g] > gbest
        gsel = jnp.where(better, g, gsel)
        gbest = jnp.where(better, gscore[g], gbest)
    masked = [jnp.where(gsel == e // PER_GROUP, rows[e], NEG_INF) for e in range(N_EXPERTS)]

    def argbest(vals):
        idx = jnp.zeros_like(gsel)
        best = vals[0]
        for e in range(1, N_EXPERTS):
            better = vals[e] > best
            idx = jnp.where(better, e, idx)
            best = jnp.where(better, vals[e], best)
        return idx

    e1 = argbest(masked)
    e2 = argbest([jnp.where(e1 == e, -jnp.inf, masked[e]) for e in range(N_EXPERTS)])
    w1 = sum(jnp.where(e1 == e, affs[e], 0.0) for e in range(N_EXPERTS))
    w2 = sum(jnp.where(e2 == e, affs[e], 0.0) for e in range(N_EXPERTS))
    tot = w1 + w2
    w1 = w1 / tot
    w2 = w2 / tot
    comb = jnp.concatenate([jnp.where(e1 == e, w1, 0.0) + jnp.where(e2 == e, w2, 0.0)
                            for e in range(N_EXPERTS)], axis=0)
    return comb, gsel


def _out_kernel(x_ref, oa_ref, ob_ref, oc_ref, dc_ref, ds_ref, dw_ref,
                mixg_ref, wout_ref, g1_ref, n2g_ref, sc2_ref, sh2_ref, rw_ref, rb_ref,
                xo_ref, h2_ref, gsel_ref):
    o_a = jnp.concatenate([oa_ref[0, h] for h in range(HEADS)], axis=-1)
    o_b = ob_ref[0]
    o_c = oc_ref[0]
    o_d = dc_ref[0] + ds_ref[0] + dw_ref[0]
    mixed = jnp.concatenate([_group_norm(o, mixg_ref[i:i + 1, :]).astype(BF16)
                             for i, o in enumerate((o_a, o_b, o_c, o_d))], axis=-1)
    y = jnp.dot(mixed, wout_ref[...], preferred_element_type=F32)
    x = x_ref[0] + g1_ref[0] * y
    xo_ref[0] = x
    ms = jnp.mean(x * x, axis=-1, keepdims=True)
    h2 = (x * lax.rsqrt(ms + NORM_EPS)) * n2g_ref[...] * (1.0 + sc2_ref[0]) + sh2_ref[0]
    logits_t = lax.dot_general(rw_ref[...], h2.astype(BF16), (((1,), (1,)), ((), ())),
                               preferred_element_type=F32)
    comb_t, gsel = _route(logits_t, rb_ref[...])
    h2_ref[0] = jnp.concatenate([h2, _rows_to_lanes(comb_t, LANES)], axis=-1)
    gsel_ref[0] = gsel


MIXER_OUT_TILE = 512


def _mixer_out(x, o_a, o_b, o_c, o_dc, o_ds, o_dw, mix_g, w_out, g1, n2g, sc2, sh2, rw_t, rb):
    b, s, d = x.shape
    tm = min(MIXER_OUT_TILE, s)
    head_spec = pl.BlockSpec((1, HEADS, tm, HEAD_DIM), lambda i, m: (i, 0, m, 0))
    mixer_spec = pl.BlockSpec((1, tm, GROUP_WIDTH), lambda i, m: (i, m, 0))
    vec_b = pl.BlockSpec((1, 1, d), lambda i, m: (i, 0, 0))
    return pl.pallas_call(
        _out_kernel,
        grid=(b, s // tm),
        in_specs=[pl.BlockSpec((1, tm, d), lambda i, m: (i, m, 0)),
                  head_spec, mixer_spec, mixer_spec, mixer_spec, mixer_spec, mixer_spec,
                  pl.BlockSpec((N_MIXERS, GROUP_WIDTH), lambda i, m: (0, 0)),
                  pl.BlockSpec((N_MIXERS * GROUP_WIDTH, d), lambda i, m: (0, 0), pipeline_mode=pl.Buffered(1)),
                  vec_b,
                  pl.BlockSpec((1, d), lambda i, m: (0, 0)),
                  vec_b, vec_b,
                  pl.BlockSpec((N_EXPERTS, d), lambda i, m: (0, 0)),
                  pl.BlockSpec((N_EXPERTS, 1), lambda i, m: (0, 0))],
        out_specs=[pl.BlockSpec((1, tm, d), lambda i, m: (i, m, 0)),
                   pl.BlockSpec((1, tm, d + LANES), lambda i, m: (i, m, 0)),
                   pl.BlockSpec((1, 1, tm), lambda i, m: (i, 0, m))],
        out_shape=[jax.ShapeDtypeStruct((b, s, d), F32),
                   jax.ShapeDtypeStruct((b, s, d + LANES), F32),
                   jax.ShapeDtypeStruct((b, 1, s), jnp.int32)],
        compiler_params=_cparams(2),
        name="mixer_out",
    )(x, o_a, o_b, o_c, o_dc, o_ds, o_dw, mix_g, w_out, g1.reshape(b, 1, d), n2g.reshape(1, d),
      sc2.reshape(b, 1, d), sh2.reshape(b, 1, d), rw_t, rb.reshape(N_EXPERTS, 1))


EXPERT_TILE = 512


def _dispatch_plan(gsel, tm):
    gid = gsel.reshape(-1)
    n_tok = gid.shape[0]
    onehot = (gid[:, None] == jnp.arange(N_EXPERT_GROUPS, dtype=jnp.int32)[None, :]).astype(jnp.int32)
    csum = jnp.cumsum(onehot, axis=0)
    rank = jnp.sum((csum - onehot) * onehot, axis=1)
    padded = ((csum[-1] + tm - 1) // tm) * tm
    ends = jnp.cumsum(padded)
    dest = jnp.sum(onehot * (ends - padded)[None, :], axis=1) + rank
    n_rows = n_tok + N_EXPERT_GROUPS * tm
    src = jnp.zeros((n_rows,), jnp.int32).at[dest].set(jnp.arange(n_tok, dtype=jnp.int32), unique_indices=True)
    tile_start = jnp.arange(n_rows // tm, dtype=jnp.int32) * tm
    tile_gid = jnp.minimum(jnp.sum((tile_start[:, None] >= ends[None, :]).astype(jnp.int32), axis=1),
                           N_EXPERT_GROUPS - 1)
    return dest.astype(jnp.int32), src, tile_gid, (ends[-1] // tm).reshape(1).astype(jnp.int32)


def _start_row_gather(idx_ref, base, src_hbm, buf, sem):
    def issue(r, c):
        pltpu.make_async_copy(src_hbm.at[pl.ds(idx_ref[base + r], 1)], buf.at[pl.ds(r, 1)], sem).start()
        return c

    lax.fori_loop(0, buf.shape[0], issue, 0, unroll=8)


def _wait_row_gather(src_hbm, buf, sem):
    pltpu.make_async_copy(src_hbm.at[pl.ds(0, buf.shape[0])], buf, sem).wait()


def _expert_kernel(src_ref, gid_ref, nvalid_ref, h_hbm, wgu_ref, wd_ref, o_ref, rows_buf, sems, h_scr, acc_ref, *, tm):
    t = pl.program_id(0)
    e = pl.program_id(1)
    slot = t % 2
    n_valid = nvalid_ref[0]

    @pl.when(t < n_valid)
    def _():
        @pl.when(e == 0)
        def _():
            @pl.when(t == 0)
            def _():
                _start_row_gather(src_ref, 0, h_hbm, rows_buf.at[0], sems.at[0])

            _wait_row_gather(h_hbm, rows_buf.at[slot], sems.at[slot])

            @pl.when(t + 1 < n_valid)
            def _():
                _start_row_gather(src_ref, (t + 1) * tm, h_hbm, rows_buf.at[1 - slot], sems.at[1 - slot])

            h_scr[...] = rows_buf[slot, :, :D_MODEL].astype(BF16)
            acc_ref[...] = jnp.zeros_like(acc_ref)

        gu = jnp.dot(h_scr[...], wgu_ref[0], preferred_element_type=F32)
        gate = gu[:, :D_EXPERT]
        he = (gate * jax.nn.sigmoid(gate)) * gu[:, D_EXPERT:]
        cw = rows_buf[slot, :, D_MODEL:]
        col = lax.broadcasted_iota(jnp.int32, cw.shape, 1)
        w = jnp.sum(jnp.where(col == gid_ref[t] * PER_GROUP + e, cw, 0.0), axis=-1, keepdims=True)
        acc_ref[...] += w * jnp.dot(he.astype(BF16), wd_ref[0], preferred_element_type=F32)

        @pl.when(e == PER_GROUP - 1)
        def _():
            o_ref[...] = acc_ref[...]

    @pl.when((t >= n_valid) & (e == PER_GROUP - 1))
    def _():
        o_ref[...] = jnp.zeros_like(o_ref)


def _experts(h2ext, src, tile_gid, n_valid, w_gu, w_d, tm):
    n_rows = src.shape[0]
    d = D_MODEL
    expert = lambda t, e, src_ref, gid, nv: (gid[t] * PER_GROUP + e, 0, 0)
    return pl.pallas_call(
        functools.partial(_expert_kernel, tm=tm),
        grid_spec=pltpu.PrefetchScalarGridSpec(
            num_scalar_prefetch=3,
            grid=(n_rows // tm, PER_GROUP),
            in_specs=[pl.BlockSpec(memory_space=pl.ANY),
                      pl.BlockSpec((1, d, 2 * D_EXPERT), expert),
                      pl.BlockSpec((1, D_EXPERT, d), expert)],
            out_specs=pl.BlockSpec((tm, d), lambda t, e, src_ref, gid, nv: (t, 0)),
            scratch_shapes=[pltpu.VMEM((2, tm, d + LANES), F32), pltpu.SemaphoreType.DMA((2,)),
                            pltpu.VMEM((tm, d), BF16), pltpu.VMEM((tm, d), F32)]),
        out_shape=jax.ShapeDtypeStruct((n_rows, d), F32),
        compiler_params=_cparams(2),
        name="experts",
    )(src, tile_gid, n_valid, h2ext, w_gu, w_d)


COMBINE_TILE = 512


def _combine_kernel(dest_ref, ys_hbm, x_ref, g2_ref, fg_ref, o_ref, rows_buf, sems, *, tm, final_norm):
    i = pl.program_id(0)
    slot = i % 2

    @pl.when(i == 0)
    def _():
        _start_row_gather(dest_ref, 0, ys_hbm, rows_buf.at[0], sems.at[0])

    _wait_row_gather(ys_hbm, rows_buf.at[slot], sems.at[slot])

    @pl.when(i + 1 < pl.num_programs(0))
    def _():
        _start_row_gather(dest_ref, (i + 1) * tm, ys_hbm, rows_buf.at[1 - slot], sems.at[1 - slot])

    x = x_ref[0] + g2_ref[0] * rows_buf[slot]
    if final_norm:
        ms = jnp.mean(x * x, axis=-1, keepdims=True)
        x = (x * lax.rsqrt(ms + NORM_EPS)) * fg_ref[...]
    o_ref[0] = x


def _combine(x, ys, dest, g2, final_g, final_norm):
    b, s, d = x.shape
    tm = min(COMBINE_TILE, s)
    per_seq = s // tm
    return pl.pallas_call(
        functools.partial(_combine_kernel, tm=tm, final_norm=final_norm),
        grid_spec=pltpu.PrefetchScalarGridSpec(
            num_scalar_prefetch=1,
            grid=(b * per_seq,),
            in_specs=[pl.BlockSpec(memory_space=pl.ANY),
                      pl.BlockSpec((1, tm, d), lambda i, dest_ref: (i // per_seq, i % per_seq, 0)),
                      pl.BlockSpec((1, 1, d), lambda i, dest_ref: (i // per_seq, 0, 0)),
                      pl.BlockSpec((1, d), lambda i, dest_ref: (0, 0))],
            out_specs=pl.BlockSpec((1, tm, d), lambda i, dest_ref: (i // per_seq, i % per_seq, 0)),
            scratch_shapes=[pltpu.VMEM((2, tm, d), F32), pltpu.SemaphoreType.DMA((2,))]),
        out_shape=jax.ShapeDtypeStruct((b, s, d), F32),
        compiler_params=_cparams(1),
        name="moe_combine",
    )(dest, ys, x, g2.reshape(b, 1, d), final_g.reshape(1, d))


def _moe(x, h2ext, gsel, w_gu, w_d, g2, final_g, final_norm):
    b, s, _ = x.shape
    dest, src, tile_gid, n_valid = _dispatch_plan(gsel, EXPERT_TILE)
    ys = _experts(h2ext.reshape(b * s, -1), src, tile_gid, n_valid, w_gu, w_d, EXPERT_TILE)
    return _combine(x, ys, dest, g2, final_g, final_norm)


def _bias_tables(rel_bias, s):
    tabs = [rel_bias[:, i * HEADS:(i + 1) * HEADS].T for i in range(N_MIXERS)]
    t = {}
    t["a"] = jnp.stack([_band_bias_tiles(tabs[0], window // r, DILATED_WINDOW, r, 1)
                        for window, r in DILATED_CONFIGS], axis=1)
    t["b"] = _band_bias_tiles(tabs[1], SWA_WINDOW - 1, 128, 1, GQA)
    t["c"] = _flash_bias_tiles(tabs[2], s)
    t["ds"] = _flash_bias_tiles(tabs[3], s)
    t["dw"] = _band_bias_tiles(tabs[3], NSA_WINDOW - 1, NSA_WINDOW, 1, GQA)
    return t


def _token_mixer_branches(p, gates, p_cmp, bias, sink, pe, w1, w2):
    b, _, s, _ = p.shape
    o_a = _dilated_attention(p, bias["a"])
    o_b = _banded_attention(p, p, p, bias["b"], grid01=(b, 1),
                            q_map=lambda i, j: (i, QB // HEADS), kv_maps=(lambda i, j: (i, KB // KV_HEADS),
                                                                         lambda i, j: (i, VB // KV_HEADS)),
                            bias_map=lambda i, j: 0, r_heads=GQA, units=KV_HEADS, w=128, out_heads=HEADS,
                            sink=sink, name="swa")
    o_c = _moba_attention(p, bias["c"])
    kvc = _nsa_compress(p_cmp, pe, w1, w2)
    o_dc, pen = _nsa_compressed_attention(p, kvc, gates)
    o_ds = _nsa_selected_attention(p, pen, gates, bias["ds"])
    o_dw = _banded_attention(p, p, p, bias["dw"], grid01=(b, KV_HEADS),
                             q_map=lambda i, j: (i, QD // GQA + j), kv_maps=(lambda i, j: (i, KDW + j),
                                                                            lambda i, j: (i, VDW + j)),
                             bias_map=lambda i, j: j, r_heads=GQA, units=1, w=NSA_WINDOW, out_heads=HEADS,
                             gates=gates, gate_branch=2, name="nsa_window")
    return o_a, o_b, o_c, o_dc, o_ds, o_dw


def _split_w_in(w_in_l):
    w_heads = w_in_l[:, :PROJ_COLS].astype(BF16)
    w_gate = jnp.pad(w_in_l[:, PROJ_COLS:], ((0, 0), (0, LANES - N_GATE_COLS))).astype(BF16)
    return w_heads, w_gate


def _sink_rows(sink_l):
    return jnp.repeat(sink_l.astype(F32).reshape(KV_HEADS, GQA), BAND_TILE, axis=1)[..., None]


def kernel(x, c, rel_bias, router_w, router_bias, norm1_g, norm2_g, ada_w, ada_b, w_in, nsa_pe_k, nsa_pe_v,
           nsa_cmp_w1_k, nsa_cmp_w2_k, nsa_cmp_w1_v, nsa_cmp_w2_v, sinks, mix_norm_g, w_out, exp_w_gate,
           exp_w_up, exp_w_down, final_g):
    b, s, d = x.shape
    mod = _modulation(c, ada_w, ada_b)
    bias = _bias_tables(rel_bias, s)
    rw_t = router_w.T.astype(BF16)
    for l in range(DEPTH):
        sh1, sc1, g1, sh2, sc2, g2 = jnp.split(mod[l], 6, axis=-1)
        w_heads, w_gate = _split_w_in(w_in[l])
        p, gates, p_cmp = _in_projection(x, norm1_g[l], sc1, sh1, w_heads, w_gate)
        pe = jnp.stack([nsa_pe_k[l], nsa_pe_v[l]])
        w1 = jnp.stack([nsa_cmp_w1_k[l], nsa_cmp_w1_v[l]])
        w2 = jnp.stack([nsa_cmp_w2_k[l], nsa_cmp_w2_v[l]])
        o_a, o_b, o_c, o_dc, o_ds, o_dw = _token_mixer_branches(p, gates, p_cmp, bias, _sink_rows(sinks[l]), pe,
                                                                   w1, w2)
        x, h2ext, gsel = _mixer_out(x, o_a, o_b, o_c, o_dc, o_ds, o_dw, mix_norm_g[l],
                                    w_out[l].astype(BF16), g1, norm2_g[l], sc2, sh2, rw_t, router_bias)
        w_gu = jnp.concatenate([exp_w_gate[l], exp_w_up[l]], axis=-1).astype(BF16)
        x = _moe(x, h2ext, gsel, w_gu, exp_w_down[l].astype(BF16), g2, final_g, l == DEPTH - 1)
    return x
```

```python
import functools
import math

import numpy as np
import jax
import jax.numpy as jnp
from jax import lax
from jax.experimental import pallas as pl
from jax.experimental.pallas import tpu as pltpu

F32 = jnp.float32
BF16 = jnp.bfloat16

D_MODEL = 2048
DEPTH = 4
HEAD_DIM = 64
N_MIXERS = 4
HEADS = 8
GROUP_WIDTH = HEADS * HEAD_DIM
SCALE = HEAD_DIM ** -0.5
REL_BUCKETS = 32
REL_MAX_DIST = 2048
DILATED_CONFIGS = ((128, 1), (512, 4), (2048, 16))
SWA_WINDOW = 128
KV_HEADS = 2
GQA = HEADS // KV_HEADS
MOBA_BLOCK = 256
MOBA_TOPK = 3
NSA_CMP_BLOCK = 32
NSA_CMP_STRIDE = 16
NSA_CMP_HIDDEN = 128
NSA_SLC_BLOCK = 64
NSA_SLC_TOPN = 16
NSA_WINDOW = 512
N_EXPERTS = 16
N_EXPERT_GROUPS = 4
PER_GROUP = N_EXPERTS // N_EXPERT_GROUPS
D_EXPERT = 512
NORM_EPS = 1e-6
NEG_INF = -1e30
TINY = float(np.finfo(np.float32).tiny)

N_PROJ_HEADS = 80
PROJ_COLS = N_PROJ_HEADS * HEAD_DIM
N_GATE_COLS = HEADS * 3
LANES = 128
SUBLANES = 8
QA, KA, VA = 0, 8, 16
QB, KB, VB = 24, 32, 34
QC, KC, VC = 36, 44, 52
QD, KDC, VDC, KDS, VDS, KDW, VDW = 60, 68, 70, 72, 74, 76, 78

VMEM_LIMIT = 56 * 1024 * 1024


def _cparams(n_axes):
    return pltpu.CompilerParams(dimension_semantics=("arbitrary",) * n_axes,
                                vmem_limit_bytes=VMEM_LIMIT)


def _mod_kernel(c_ref, w_ref, b_ref, o_ref):
    c = c_ref[...]
    ca = (c * jax.nn.sigmoid(c)).astype(BF16)
    o_ref[0] = jnp.dot(ca, w_ref[0].astype(BF16), preferred_element_type=F32) + b_ref[0]


def _modulation(c, ada_w, ada_b):
    depth, d, n = ada_w.shape
    b = c.shape[0]
    bp = 8
    tn = 1024
    cp = jnp.pad(c, ((0, bp - b), (0, 0)))
    out = pl.pallas_call(
        _mod_kernel,
        grid=(depth, n // tn),
        in_specs=[pl.BlockSpec((bp, d), lambda l, j: (0, 0)),
                  pl.BlockSpec((1, d, tn), lambda l, j: (l, 0, j)),
                  pl.BlockSpec((1, 1, tn), lambda l, j: (l, 0, j))],
        out_specs=pl.BlockSpec((1, bp, tn), lambda l, j: (l, 0, j)),
        out_shape=jax.ShapeDtypeStruct((depth, bp, n), F32),
        compiler_params=_cparams(2),
        name="adaln_mod",
    )(cp, ada_w, ada_b.reshape(depth, 1, n))
    return out[:, :b]


CMP_COL_GROUP = KDC // HEADS


def _inproj_kernel(x_ref, g_ref, sc_ref, sh_ref, w_ref, wg_ref, p_ref, gate_ref, cmp_ref, h_scr, res_scr):
    n = pl.program_id(2)

    @pl.when(n == 0)
    def _():
        x = x_ref[0]
        ms = jnp.mean(x * x, axis=-1, keepdims=True)
        h = (x * lax.rsqrt(ms + NORM_EPS)) * g_ref[...] * (1.0 + sc_ref[0]) + sh_ref[0]
        hb = h.astype(BF16)
        h_scr[...] = hb
        gate_ref[0] = jax.nn.sigmoid(jnp.dot(hb, wg_ref[...], preferred_element_type=F32))

    res = jnp.dot(h_scr[...], w_ref[...], preferred_element_type=F32)
    for j in range(HEADS):
        p_ref[0, j] = res[:, j * HEAD_DIM:(j + 1) * HEAD_DIM].astype(BF16)

    @pl.when(n == CMP_COL_GROUP)
    def _():
        first = (KDC % HEADS) * HEAD_DIM
        n_slab = 2 * KV_HEADS * HEAD_DIM // LANES
        for sl in range(n_slab):
            res_scr[sl] = res[:, first + sl * LANES:first + (sl + 1) * LANES]
        rows = res.shape[0] // NSA_CMP_STRIDE
        for sl in range(n_slab):
            taps = [res_scr[sl, pl.ds(i, rows, stride=NSA_CMP_STRIDE), :] for i in range(NSA_CMP_STRIDE)]
            for half in range(LANES // HEAD_DIM):
                lanes = slice(half * HEAD_DIM, (half + 1) * HEAD_DIM)
                cmp_ref[0, sl * (LANES // HEAD_DIM) + half] = jnp.concatenate(
                    [tap[:, lanes] for tap in taps], axis=-1).astype(BF16)


def _in_projection(x, g, sc, sh, w_heads, w_gate):
    b, s, d = x.shape
    tm = min(1024, s)
    tn = HEADS * HEAD_DIM
    n_cmp_heads = 2 * KV_HEADS
    half = NSA_CMP_STRIDE * HEAD_DIM
    return pl.pallas_call(
        _inproj_kernel,
        grid=(b, s // tm, PROJ_COLS // tn),
        in_specs=[pl.BlockSpec((1, tm, d), lambda i, m, n: (i, m, 0)),
                  pl.BlockSpec((1, d), lambda i, m, n: (0, 0)),
                  pl.BlockSpec((1, 1, d), lambda i, m, n: (i, 0, 0)),
                  pl.BlockSpec((1, 1, d), lambda i, m, n: (i, 0, 0)),
                  pl.BlockSpec((d, tn), lambda i, m, n: (0, n)),
                  pl.BlockSpec((d, LANES), lambda i, m, n: (0, 0))],
        out_specs=[pl.BlockSpec((1, HEADS, tm, HEAD_DIM), lambda i, m, n: (i, n, m, 0)),
                   pl.BlockSpec((1, tm, LANES), lambda i, m, n: (i, m, 0)),
                   pl.BlockSpec((1, n_cmp_heads, tm // NSA_CMP_STRIDE, half), lambda i, m, n: (i, 0, m, 0))],
        out_shape=[jax.ShapeDtypeStruct((b, N_PROJ_HEADS, s, HEAD_DIM), BF16),
                   jax.ShapeDtypeStruct((b, s, LANES), F32),
                   jax.ShapeDtypeStruct((b, n_cmp_heads, s // NSA_CMP_STRIDE, half), BF16)],
        scratch_shapes=[pltpu.VMEM((tm, d), BF16),
                        pltpu.VMEM((n_cmp_heads * HEAD_DIM // LANES, tm, LANES), F32)],
        compiler_params=_cparams(3),
        name="in_proj",
    )(x, g.reshape(1, d), sc.reshape(b, 1, d), sh.reshape(b, 1, d), w_heads, w_gate)


def _bucket_thresholds():
    d = np.arange(REL_MAX_DIST + 1)
    max_exact = REL_BUCKETS // 2
    large = max_exact + (np.log(np.maximum(d, 1).astype(np.float32) / np.float32(max_exact))
                         / np.float32(math.log(REL_MAX_DIST / max_exact))
                         * np.float32(REL_BUCKETS - max_exact)).astype(np.int32)
    bucket = np.where(d < max_exact, d, np.minimum(large, REL_BUCKETS - 1))
    return [int(np.argmax(bucket >= b)) for b in range(REL_BUCKETS)]


def _bias_kernel(off_ref, cmin_ref, tab_ref, o_ref, *, max_dist, dist_scale, thresholds):
    h = pl.program_id(0)
    t = pl.program_id(1)
    shape = o_ref.shape[-2:]
    rel = off_ref[t] + lax.broadcasted_iota(jnp.int32, shape, 0) - lax.broadcasted_iota(jnp.int32, shape, 1)
    acc = jnp.full(shape, tab_ref[h, 0], F32)
    for b in range(1, REL_BUCKETS):
        acc = jnp.where(rel >= -(-thresholds[b] // dist_scale), tab_ref[h, b], acc)
    col = lax.broadcasted_iota(jnp.int32, shape, 1)
    valid = (rel >= 0) & (rel <= max_dist) & (col >= cmin_ref[t])
    o_ref[...] = jnp.where(valid, acc, NEG_INF).reshape(o_ref.shape)


def _bias_tiles(tab, offsets, col_mins, rows, cols, max_dist, dist_scale, group):
    n_heads = tab.shape[0]
    n_tiles = len(offsets)
    smem = pl.BlockSpec(memory_space=pltpu.SMEM)
    out = pl.pallas_call(
        functools.partial(_bias_kernel, max_dist=max_dist, dist_scale=dist_scale,
                          thresholds=_bucket_thresholds()),
        grid=(n_heads, n_tiles),
        in_specs=[smem, smem, smem],
        out_specs=pl.BlockSpec((1, 1, 1, rows, cols), lambda h, t: (h // group, t, h % group, 0, 0)),
        out_shape=jax.ShapeDtypeStruct((n_heads // group, n_tiles, group, rows, cols), F32),
        compiler_params=_cparams(2),
        name="bias_tiles",
    )(jnp.asarray(offsets, jnp.int32), jnp.asarray(col_mins, jnp.int32), tab.astype(F32))
    return out.reshape(n_heads // group, n_tiles, group * rows, cols)


BAND_TILE = 128
BAND_STEP = 1024


def _gate_column(gates, idx):
    col = lax.broadcasted_iota(jnp.int32, gates.shape, 1)
    return jnp.sum(jnp.where(col == idx, gates, 0.0), axis=-1, keepdims=True)


def _banded_kernel(*refs, units, r_heads, nsub, w, has_sink, gate_branch):
    q_ref, kp_ref, kc_ref, vp_ref, vc_ref, b_ref = refs[:6]
    pos = 6
    sink_ref = gate_ref = None
    if has_sink:
        sink_ref = refs[pos]
        pos += 1
    if gate_branch is not None:
        gate_ref = refs[pos]
        pos += 1
    o_ref, kw_scr, vw_scr = refs[pos:pos + 3]
    t = pl.program_id(2)
    tq = BAND_TILE
    rows = r_heads * tq
    nk = w + tq
    kw_scr[:, :w] = kp_ref[0]
    kw_scr[:, w:] = kc_ref[0]
    vw_scr[:, :w, :HEAD_DIM] = vp_ref[0]
    vw_scr[:, w:, :HEAD_DIM] = vc_ref[0]
    vw_scr[:, :, HEAD_DIM:] = jnp.ones(vw_scr.shape[:2] + (HEAD_DIM,), BF16)
    for u in range(units):
        for j in range(nsub):
            q = q_ref[0, u * r_heads:(u + 1) * r_heads, j * tq:(j + 1) * tq, :].reshape(rows, HEAD_DIM) * SCALE
            k = kw_scr[u, j * tq:j * tq + nk, :]
            v = vw_scr[u, j * tq:j * tq + nk, :]
            tile = jnp.minimum(t * nsub + j, w // tq)
            s = lax.dot_general(q, k, (((1,), (1,)), ((), ())), preferred_element_type=F32) + b_ref[u, tile]
            m = jnp.max(s, axis=-1, keepdims=True)
            if has_sink:
                m = jnp.maximum(m, sink_ref[u])
            e = jnp.exp(s - m)
            num_den = jnp.dot(e.astype(BF16), v, preferred_element_type=F32)
            den = num_den[:, HEAD_DIM:]
            if has_sink:
                den = den + jnp.exp(sink_ref[u] - m)
            o = num_den[:, :HEAD_DIM] / jnp.maximum(den, TINY)
            parts = []
            for g in range(r_heads):
                og = o[g * tq:(g + 1) * tq]
                if gate_ref is not None:
                    head = (pl.program_id(1) * units + u) * r_heads + g
                    og = og * _gate_column(gate_ref[0, j * tq:(j + 1) * tq, :], head * 3 + gate_branch)
                parts.append(og)
            lanes = slice(u * r_heads * HEAD_DIM, (u + 1) * r_heads * HEAD_DIM)
            o_ref[0, j * tq:(j + 1) * tq, lanes] = jnp.concatenate(parts, axis=-1)


def _band_bias_tiles(tab, max_dist, w, dist_scale, group):
    tq = BAND_TILE
    n_early = w // tq
    return _bias_tiles(tab, [w] * (n_early + 1), [w - e * tq for e in range(n_early)] + [0], tq, w + tq,
                       max_dist, dist_scale, group)


def _banded_attention(q_arr, k_arr, v_arr, bias, *, grid01, q_map, kv_maps, bias_map, r_heads, units, w,
                      out_heads, sink=None, gates=None, gate_branch=None, name):
    seq = q_arr.shape[2]
    tq = BAND_TILE
    step = min(BAND_STEP, seq)
    nsub = step // tq
    assert seq % step == 0 and w % tq == 0 and step % w == 0
    k_map, v_map = kv_maps
    prev = lambda t: jnp.maximum(t * (step // w) - 1, 0)
    in_specs = [pl.BlockSpec((1, units * r_heads, step, HEAD_DIM), lambda i, j, t: q_map(i, j) + (t, 0)),
                pl.BlockSpec((1, units, w, HEAD_DIM), lambda i, j, t: k_map(i, j) + (prev(t), 0)),
                pl.BlockSpec((1, units, step, HEAD_DIM), lambda i, j, t: k_map(i, j) + (t, 0)),
                pl.BlockSpec((1, units, w, HEAD_DIM), lambda i, j, t: v_map(i, j) + (prev(t), 0)),
                pl.BlockSpec((1, units, step, HEAD_DIM), lambda i, j, t: v_map(i, j) + (t, 0)),
                pl.BlockSpec((units, w // tq + 1, r_heads * tq, w + tq), lambda i, j, t: (bias_map(i, j), 0, 0, 0))]
    args = [q_arr, k_arr, k_arr, v_arr, v_arr, bias]
    if sink is not None:
        in_specs.append(pl.BlockSpec((units, r_heads * tq, 1), lambda i, j, t: (bias_map(i, j), 0, 0)))
        args.append(sink)
    if gates is not None:
        in_specs.append(pl.BlockSpec((1, step, LANES), lambda i, j, t: (i, t, 0)))
        args.append(gates)
    n0, n1 = grid01
    return pl.pallas_call(
        functools.partial(_banded_kernel, units=units, r_heads=r_heads, nsub=nsub, w=w,
                          has_sink=sink is not None, gate_branch=gate_branch if gates is not None else None),
        grid=(n0, n1, seq // step),
        in_specs=in_specs,
        out_specs=pl.BlockSpec((1, step, units * r_heads * HEAD_DIM), lambda i, j, t: (i, t, j)),
        out_shape=jax.ShapeDtypeStruct((n0, seq, out_heads * HEAD_DIM), F32),
        scratch_shapes=[pltpu.VMEM((units, w + step, HEAD_DIM), BF16),
                        pltpu.VMEM((units, w + step, 2 * HEAD_DIM), BF16)],
        compiler_params=_cparams(3),
        name=name,
    )(*args)


DILATED_WINDOW = 128


DEINTERLEAVE = 4


def _dilated_kernel(q_ref, k_ref, v_ref, b_ref, o_ref, q_scr, kv_scr, q4_scr, kv4_scr, o_scr, l_scr, *, seq, pad):
    tq = BAND_TILE
    w = DILATED_WINDOW
    nt = (((1,), (1,)), ((), ()))
    zeros = jnp.zeros((seq, HEAD_DIM), F32)
    q_scr[...] = jnp.concatenate([q_ref[0, 0].astype(F32) * SCALE, zeros], axis=-1)
    kv_scr[:pad] = jnp.zeros((pad, 2 * HEAD_DIM), F32)
    kv_scr[pad:] = jnp.concatenate([k_ref[0, 0].astype(F32), v_ref[0, 0].astype(F32)], axis=-1)

    def rows(start, n, r):
        return pl.ds(start, n) if r == 1 else pl.ds(start, n, stride=r)

    sub = seq // DEINTERLEAVE
    pad4 = pad // DEINTERLEAVE
    for c in range(DEINTERLEAVE):
        q4_scr[c] = q_scr[pl.ds(c, sub, stride=DEINTERLEAVE), :]
        kv4_scr[c, :pad4] = jnp.zeros((pad4, 2 * HEAD_DIM), F32)
        kv4_scr[c, pad4:] = kv_scr[pl.ds(pad + c, sub, stride=DEINTERLEAVE), :]

    def tile(cfg, r, j, t):
        first = j + r * (t * tq)
        if r % DEINTERLEAVE == 0:
            rr = r // DEINTERLEAVE
            c = j % DEINTERLEAVE
            first4 = j // DEINTERLEAVE + rr * (t * tq)
            q = q4_scr[c, rows(first4, tq, rr), :].astype(BF16)
            kv = kv4_scr[c, rows(pad4 + first4 - rr * w, w + tq, rr), :].astype(BF16)
        else:
            q = q_scr[rows(first, tq, r), :].astype(BF16)
            kv = kv_scr[rows(pad + first - r * w, w + tq, r), :].astype(BF16)
        s = lax.dot_general(q, kv, nt, preferred_element_type=F32) + b_ref[0, cfg, jnp.minimum(t, 1)]
        m = jnp.max(s, axis=-1, keepdims=True)
        e = jnp.exp(s - m)
        den = jnp.maximum(jnp.sum(e, axis=-1, keepdims=True), TINY)
        o_scr[cfg, rows(first, tq, r), :] = jnp.dot(e.astype(BF16), kv, preferred_element_type=F32) / den
        l_scr[cfg, rows(first, tq, r), :] = jnp.broadcast_to(m + jnp.log(den), (tq, 2 * HEAD_DIM))

    group = 16
    for cfg, (_, r) in enumerate(DILATED_CONFIGS):
        nsub = seq // r // tq

        def body(g, c, cfg=cfg, r=r, nsub=nsub):
            for u in range(group):
                tid = g * group + u
                tile(cfg, r, tid // nsub, tid % nsub)
            return c

        lax.fori_loop(0, r * nsub // group, body, 0)

    chunk = 512
    for c in range(seq // chunk):
        sl = slice(c * chunk, (c + 1) * chunk)
        ls = [l_scr[i, sl] for i in range(len(DILATED_CONFIGS))]
        mx = functools.reduce(jnp.maximum, ls)
        ws = [jnp.exp(l - mx) for l in ls]
        mixed = sum(wi * o_scr[i, sl] for i, wi in enumerate(ws)) / sum(ws)
        o_ref[0, 0, sl, :] = mixed[:, HEAD_DIM:]


def _dilated_attention(p, bias):
    b, _, s, _ = p.shape
    pad = max(r for _, r in DILATED_CONFIGS) * DILATED_WINDOW
    assert all(win // r == DILATED_WINDOW and (s // r) % BAND_TILE == 0 for win, r in DILATED_CONFIGS)
    head_spec = lambda base: pl.BlockSpec((1, 1, s, HEAD_DIM), lambda i, h: (i, base + h, 0, 0))
    n_cfg = len(DILATED_CONFIGS)
    return pl.pallas_call(
        functools.partial(_dilated_kernel, seq=s, pad=pad),
        grid=(b, HEADS),
        in_specs=[head_spec(QA), head_spec(KA), head_spec(VA),
                  pl.BlockSpec((1, n_cfg, 2, BAND_TILE, DILATED_WINDOW + BAND_TILE), lambda i, h: (h, 0, 0, 0, 0))],
        out_specs=pl.BlockSpec((1, 1, s, HEAD_DIM), lambda i, h: (i, h, 0, 0)),
        out_shape=jax.ShapeDtypeStruct((b, HEADS, s, HEAD_DIM), F32),
        scratch_shapes=[pltpu.VMEM((s, 2 * HEAD_DIM), F32),
                        pltpu.VMEM((pad + s, 2 * HEAD_DIM), F32),
                        pltpu.VMEM((DEINTERLEAVE, s // DEINTERLEAVE, 2 * HEAD_DIM), F32),
                        pltpu.VMEM((DEINTERLEAVE, (pad + s) // DEINTERLEAVE, 2 * HEAD_DIM), F32),
                        pltpu.VMEM((n_cfg, s, 2 * HEAD_DIM), F32),
                        pltpu.VMEM((n_cfg, s, 2 * HEAD_DIM), F32)],
        compiler_params=_cparams(2),
        name="dilated",
    )(p, p, p, bias)


def _ranks_along_rows(vals):
    n, t = vals.shape
    groups = [vals[g * SUBLANES:(g + 1) * SUBLANES] for g in range(n // SUBLANES)]
    row = lax.broadcasted_iota(jnp.int32, (SUBLANES, t), 0)
    later = [jnp.where(row > r, 1, 0) for r in range(SUBLANES)]
    cnt = [jnp.zeros((SUBLANES, t), jnp.int32) for _ in groups]
    for i in range(n):
        vi = vals[i:i + 1]
        for g, vg in enumerate(groups):
            if g > i // SUBLANES:
                cnt[g] = cnt[g] + jnp.where(vi >= vg, 1, 0)
            elif g < i // SUBLANES:
                cnt[g] = cnt[g] + jnp.where(vi > vg, 1, 0)
            else:
                cnt[g] = cnt[g] + jnp.where(vi > vg, 1, 0) + jnp.where(vi == vg, later[i % SUBLANES], 0)
    return jnp.concatenate(cnt, axis=0)


def _rows_to_lanes(x_t, width):
    n, t = x_t.shape
    padded = jnp.concatenate([x_t, jnp.zeros((LANES - n, t), x_t.dtype)], axis=0)
    return padded.T[:, :width]


def _tile_groups(n, fn):
    def body(j, c):
        fn(j * 4, 4)
        return c

    lax.fori_loop(0, n // 4, body, 0)
    base = (n // 4) * 4

    @pl.when((n & 2) != 0)
    def _():
        fn(base, 2)

    @pl.when((n & 1) != 0)
    def _():
        fn(base + (n & 2), 1)


def _flash_selected(qa_scr, qi, kaug_scr, vaug_scr, kv_of, b_ref, s_scr, m_scr, acc_scr, tq):
    nh = qa_scr.shape[0]
    half = tq // 2
    n = qi + 1
    nt = (((1,), (1,)), ((), ()))
    m_scr[...] = jnp.full(m_scr.shape, NEG_INF, F32)

    def logits(kj0, cnt):
        ks = pl.multiple_of(kj0 * tq, tq)
        for h in range(nh):
            s = lax.dot_general(qa_scr[h], kaug_scr[kv_of[h], pl.ds(ks, cnt * tq), :], nt,
                                preferred_element_type=F32)
            mloc = None
            for u in range(cnt):
                su = s[:, u * tq:(u + 1) * tq] + b_ref[h, jnp.minimum(qi - kj0 - u, b_ref.shape[1] - 1)]
                s_scr[h, kj0 + u] = su
                mu = jnp.maximum(su[:, :half], su[:, half:])
                mloc = mu if mloc is None else jnp.maximum(mloc, mu)
            m_scr[h] = jnp.maximum(m_scr[h], mloc)

    _tile_groups(n, logits)
    for h in range(nh):
        m_scr[h] = jnp.broadcast_to(jnp.max(m_scr[h], axis=-1, keepdims=True), (tq, half))
    acc_scr[...] = jnp.zeros(acc_scr.shape, F32)

    def weigh(kj0, cnt):
        ks = pl.multiple_of(kj0 * tq, tq)
        for h in range(nh):
            mb = m_scr[h]
            parts = []
            for u in range(cnt):
                su = s_scr[h, kj0 + u]
                parts += [jnp.exp(su[:, :half] - mb).astype(BF16), jnp.exp(su[:, half:] - mb).astype(BF16)]
            acc_scr[h] += jnp.dot(jnp.concatenate(parts, axis=-1), vaug_scr[kv_of[h], pl.ds(ks, cnt * tq), :],
                                  preferred_element_type=F32)

    _tile_groups(n, weigh)
    outs = []
    for h in range(nh):
        acc = acc_scr[h]
        outs.append(acc[:, :HEAD_DIM] / jnp.maximum(acc[:, HEAD_DIM:], TINY))
    return outs


def _fill_kv_aug(kaug_scr, vaug_scr, k_ref, v_ref, et_ref):
    for kv in range(kaug_scr.shape[0]):
        kaug_scr[kv, :, :HEAD_DIM] = k_ref[0, kv]
        kaug_scr[kv, :, HEAD_DIM:] = et_ref[...]
        vaug_scr[kv, :, :HEAD_DIM] = v_ref[0, kv]
        vaug_scr[kv, :, HEAD_DIM:] = jnp.ones((vaug_scr.shape[1], HEAD_DIM), BF16)


def _augment(q, pen):
    return jnp.concatenate([q * SCALE, pen.astype(BF16)], axis=-1)


def _moba_kernel(q_ref, k_ref, v_ref, pool_ref, et_ref, b_ref, o_ref, kmean_scr, kaug_scr, vaug_scr, qa_scr, s_scr,
                 m_scr, acc_scr, *, tq, nblk, nh):
    qi = pl.program_id(2)

    @pl.when(qi == 0)
    def _():
        _fill_kv_aug(kaug_scr, vaug_scr, k_ref, v_ref, et_ref)
        for h in range(nh):
            kmean_scr[h] = jnp.dot(pool_ref[...], k_ref[0, h], preferred_element_type=F32).astype(BF16)

    for h in range(nh):
        q = q_ref[0, h]
        gate = lax.dot_general(kmean_scr[h], q, (((1,), (1,)), ((), ())), preferred_element_type=F32)
        blk = lax.broadcasted_iota(jnp.int32, (nblk, tq), 0)
        gate = jnp.where(blk < qi, gate, NEG_INF)
        picked = ((_ranks_along_rows(gate) < MOBA_TOPK) & (blk < qi)) | (blk == qi)
        pen_t = jnp.where(picked, 0.0, NEG_INF)
        qa_scr[h] = _augment(q, _rows_to_lanes(pen_t, HEAD_DIM))
    outs = _flash_selected(qa_scr, qi, kaug_scr, vaug_scr, tuple(range(nh)), b_ref, s_scr, m_scr, acc_scr, tq)
    o_ref[0] = jnp.concatenate(outs, axis=-1)


def _slc_kernel(q_ref, k_ref, v_ref, pen_ref, gate_ref, et_ref, b_ref, o_ref, kaug_scr, vaug_scr, qa_scr, s_scr,
                m_scr, acc_scr, *, tq, nh):
    qi = pl.program_id(2)

    @pl.when(qi == 0)
    def _():
        _fill_kv_aug(kaug_scr, vaug_scr, k_ref, v_ref, et_ref)

    for h in range(nh):
        qa_scr[h] = _augment(q_ref[0, h], pen_ref[0, 0])
    outs = _flash_selected(qa_scr, qi, kaug_scr, vaug_scr, (0,) * nh, b_ref, s_scr, m_scr, acc_scr, tq)
    gates = gate_ref[0]
    first = pl.program_id(1) * nh
    o_ref[0] = jnp.concatenate([outs[h] * _gate_column(gates, (first + h) * 3 + 1) for h in range(nh)], axis=-1)


def _flash_state(n_kv, seq, nh, tq):
    return [pltpu.VMEM((n_kv, seq, 2 * HEAD_DIM), BF16),
            pltpu.VMEM((n_kv, seq, 2 * HEAD_DIM), BF16),
            pltpu.VMEM((nh, tq, 2 * HEAD_DIM), BF16),
            pltpu.VMEM((nh, seq // tq, tq, tq), F32),
            pltpu.VMEM((nh, tq, tq // 2), F32),
            pltpu.VMEM((nh, tq, 2 * HEAD_DIM), F32)]


FLASH_TILE = 256
MOBA_HEADS_PER_STEP = 4
SLC_HEADS_PER_STEP = GQA


def _flash_bias_tiles(tab, seq):
    tq = FLASH_TILE
    first_const = -(-(_bucket_thresholds()[-1] + tq - 1) // tq)
    n_off = min(seq // tq, first_const + 1)
    return _bias_tiles(tab, [o * tq for o in range(n_off)], [0] * n_off, tq, tq, seq, 1, 1)


def _block_onehot_t(nblk, seq):
    blk = seq // nblk
    return jnp.asarray((np.arange(seq)[:, None] // blk == np.arange(HEAD_DIM)[None, :]).astype(np.float32), BF16)


def _moba_attention(p, bias):
    b, _, s, _ = p.shape
    tq, nh = FLASH_TILE, MOBA_HEADS_PER_STEP
    assert tq == MOBA_BLOCK
    nblk = s // tq
    nrow = max(nblk, 16)
    n_bias = bias.shape[1]
    pool = jnp.asarray((np.arange(s)[None, :] // tq == np.arange(nrow)[:, None]).astype(np.float32) / tq, BF16)
    once = pl.Buffered(1)
    kv_spec = lambda base: pl.BlockSpec((1, nh, s, HEAD_DIM), lambda i, h, t: (i, base // nh + h, 0, 0),
                                        pipeline_mode=once)
    return pl.pallas_call(
        functools.partial(_moba_kernel, tq=tq, nblk=nrow, nh=nh),
        grid=(b, HEADS // nh, s // tq),
        in_specs=[pl.BlockSpec((1, nh, tq, HEAD_DIM), lambda i, h, t: (i, QC // nh + h, t, 0)),
                  kv_spec(KC), kv_spec(VC),
                  pl.BlockSpec((nrow, s), lambda i, h, t: (0, 0), pipeline_mode=once),
                  pl.BlockSpec((s, HEAD_DIM), lambda i, h, t: (0, 0), pipeline_mode=once),
                  pl.BlockSpec((nh, n_bias, tq, tq), lambda i, h, t: (h, 0, 0, 0), pipeline_mode=once)],
        out_specs=pl.BlockSpec((1, tq, nh * HEAD_DIM), lambda i, h, t: (i, t, h)),
        out_shape=jax.ShapeDtypeStruct((b, s, HEADS * HEAD_DIM), F32),
        scratch_shapes=[pltpu.VMEM((nh, nrow, HEAD_DIM), BF16)] + _flash_state(nh, s, nh, tq),
        compiler_params=_cparams(3),
        name="moba",
    )(p, p, p, pool, _block_onehot_t(nblk, s), bias)


def _nsa_selected_attention(p, pen, gates, bias):
    b, _, s, _ = p.shape
    tq, nh = FLASH_TILE, SLC_HEADS_PER_STEP
    nblk = s // NSA_SLC_BLOCK
    n_bias = bias.shape[1]
    assert nblk <= HEAD_DIM and GQA % nh == 0
    kv_spec = lambda base: pl.BlockSpec((1, 1, s, HEAD_DIM), lambda i, h, t: (i, base + (h * nh) // GQA, 0, 0))
    return pl.pallas_call(
        functools.partial(_slc_kernel, tq=tq, nh=nh),
        grid=(b, HEADS // nh, s // tq),
        in_specs=[pl.BlockSpec((1, nh, tq, HEAD_DIM), lambda i, h, t: (i, QD // nh + h, t, 0)),
                  kv_spec(KDS), kv_spec(VDS),
                  pl.BlockSpec((1, 1, tq, HEAD_DIM), lambda i, h, t: (i, (h * nh) // GQA, t, 0)),
                  pl.BlockSpec((1, tq, LANES), lambda i, h, t: (i, t, 0)),
                  pl.BlockSpec((s, HEAD_DIM), lambda i, h, t: (0, 0)),
                  pl.BlockSpec((nh, n_bias, tq, tq), lambda i, h, t: (h, 0, 0, 0))],
        out_specs=pl.BlockSpec((1, tq, nh * HEAD_DIM), lambda i, h, t: (i, t, h)),
        out_shape=jax.ShapeDtypeStruct((b, s, HEADS * HEAD_DIM), F32),
        scratch_shapes=_flash_state(1, s, nh, tq),
        compiler_params=_cparams(3),
        name="nsa_selected",
    )(p, p, p, pen, gates, _block_onehot_t(nblk, s), bias)


def _compress_kernel(t_ref, pe_ref, w1_ref, w2_ref, o_ref):
    t = t_ref[0, 0].astype(F32)
    half = t.shape[-1]
    top = (t + pe_ref[0, 0:1, :]).astype(BF16)
    bot = (t + pe_ref[0, 1:2, :]).astype(BF16)
    u = jnp.dot(top, w1_ref[0, :half, :], preferred_element_type=F32)
    vv = jnp.dot(bot, w1_ref[0, half:, :], preferred_element_type=F32)
    pre = u + pltpu.roll(vv, vv.shape[0] - 1, axis=0)
    hid = jax.nn.gelu(pre).astype(BF16)
    o_ref[0, 0] = jnp.dot(hid, w2_ref[0], preferred_element_type=F32).astype(BF16)


def _nsa_compress(t, pe, w1, w2):
    b, _, n_half, half = t.shape
    pe2 = pe.reshape(2, 2, half)
    return pl.pallas_call(
        _compress_kernel,
        grid=(b, 2 * KV_HEADS),
        in_specs=[pl.BlockSpec((1, 1, n_half, half), lambda i, j: (i, j, 0, 0)),
                  pl.BlockSpec((1, 2, half), lambda i, j: (j // KV_HEADS, 0, 0)),
                  pl.BlockSpec((1, 2 * half, NSA_CMP_HIDDEN), lambda i, j: (j // KV_HEADS, 0, 0)),
                  pl.BlockSpec((1, NSA_CMP_HIDDEN, HEAD_DIM), lambda i, j: (j // KV_HEADS, 0, 0))],
        out_specs=pl.BlockSpec((1, 1, n_half, HEAD_DIM), lambda i, j: (i, j, 0, 0)),
        out_shape=jax.ShapeDtypeStruct((b, 2 * KV_HEADS, n_half, HEAD_DIM), BF16),
        compiler_params=_cparams(2),
        name="nsa_compress",
    )(t, pe2, w1.astype(BF16), w2.astype(BF16))


def _cmp_attn_kernel(q_ref, kc_ref, vc_ref, ov_ref, gate_ref, o_ref, pen_ref, *, tq, n_slc):
    qi = pl.program_id(2)
    rows = GQA * tq
    q = q_ref[0].reshape(rows, HEAD_DIM)
    kc = kc_ref[0, 0]
    n_cmp = kc.shape[0]
    s = lax.dot_general(q, kc, (((1,), (1,)), ((), ())), preferred_element_type=F32) * SCALE
    row = lax.broadcasted_iota(jnp.int32, (rows, n_cmp), 0)
    pos = qi * tq + (row & (tq - 1))
    blk_end = lax.broadcasted_iota(jnp.int32, (rows, n_cmp), 1) * NSA_CMP_STRIDE + (NSA_CMP_BLOCK - 1)
    mask = blk_end <= pos
    s = jnp.where(mask, s, NEG_INF)
    m = jnp.max(s, axis=-1, keepdims=True)
    e = jnp.where(mask, jnp.exp(s - m), 0.0)
    den = jnp.maximum(jnp.sum(e, axis=-1, keepdims=True), TINY)
    pr = e / den
    o = jnp.dot(pr.astype(BF16), vc_ref[0, 0], preferred_element_type=F32)
    gates = gate_ref[0]
    first = pl.program_id(1) * GQA
    o_ref[0] = jnp.concatenate([o[g * tq:(g + 1) * tq] * _gate_column(gates, (first + g) * 3)
                                for g in range(GQA)], axis=-1)
    psum = pr[0:tq]
    for g in range(1, GQA):
        psum = psum + pr[g * tq:(g + 1) * tq]
    ov_t = ov_ref[...]
    hi = psum.astype(BF16)
    r1 = psum - hi.astype(F32)
    mid = r1.astype(BF16)
    lo = (r1 - mid.astype(F32)).astype(BF16)
    nt = (((1,), (1,)), ((), ()))
    imp_t = (lax.dot_general(ov_t, hi, nt, preferred_element_type=F32)
             + lax.dot_general(ov_t, mid, nt, preferred_element_type=F32)
             + lax.dot_general(ov_t, lo, nt, preferred_element_type=F32))
    jj = lax.broadcasted_iota(jnp.int32, (n_slc, tq), 0)
    cur = (qi * tq + lax.broadcasted_iota(jnp.int32, (n_slc, tq), 1)) // NSA_SLC_BLOCK
    forced = (jj == 0) | (jj == cur) | (jj == cur - 1)
    imp_t = jnp.where(forced, -NEG_INF, jnp.where(jj <= cur, imp_t, NEG_INF))
    picked = (_ranks_along_rows(imp_t) < NSA_SLC_TOPN) & (jj <= cur)
    pen_ref[0, 0] = _rows_to_lanes(jnp.where(picked, 0.0, NEG_INF), HEAD_DIM).astype(BF16)


def _nsa_compressed_attention(p, kvc, gates):
    b, _, s, _ = p.shape
    tq = 512
    n_cmp = kvc.shape[2]
    n_slc = s // NSA_SLC_BLOCK
    assert n_slc == HEAD_DIM
    c_start = np.arange(n_cmp) * NSA_CMP_STRIDE
    j_start = np.arange(n_slc) * NSA_SLC_BLOCK
    overlap = ((c_start[:, None] < j_start[None, :] + NSA_SLC_BLOCK)
               & (c_start[:, None] + NSA_CMP_BLOCK > j_start[None, :])).astype(np.float32)
    overlap[n_cmp - 1] = 0.0
    return pl.pallas_call(
        functools.partial(_cmp_attn_kernel, tq=tq, n_slc=n_slc),
        grid=(b, KV_HEADS, s // tq),
        in_specs=[pl.BlockSpec((1, GQA, tq, HEAD_DIM), lambda i, h, t: (i, QD // GQA + h, t, 0)),
                  pl.BlockSpec((1, 1, n_cmp, HEAD_DIM), lambda i, h, t: (i, h, 0, 0)),
                  pl.BlockSpec((1, 1, n_cmp, HEAD_DIM), lambda i, h, t: (i, KV_HEADS + h, 0, 0)),
                  pl.BlockSpec((n_slc, n_cmp), lambda i, h, t: (0, 0)),
                  pl.BlockSpec((1, tq, LANES), lambda i, h, t: (i, t, 0))],
        out_specs=[pl.BlockSpec((1, tq, GQA * HEAD_DIM), lambda i, h, t: (i, t, h)),
                   pl.BlockSpec((1, 1, tq, HEAD_DIM), lambda i, h, t: (i, h, t, 0))],
        out_shape=[jax.ShapeDtypeStruct((b, s, HEADS * HEAD_DIM), F32),
                   jax.ShapeDtypeStruct((b, KV_HEADS, s, HEAD_DIM), BF16)],
        compiler_params=_cparams(3),
        name="nsa_compressed",
    )(p, kvc, kvc, jnp.asarray(overlap.T, BF16), gates)


def _group_norm(o, g):
    ms = jnp.mean(o * o, axis=-1, keepdims=True)
    return (o * lax.rsqrt(ms + NORM_EPS)) * g


def _route(logits_t, bias_t):
    aff = jax.nn.sigmoid(logits_t)
    score = aff + bias_t
    rows = [score[e:e + 1] for e in range(N_EXPERTS)]
    affs = [aff[e:e + 1] for e in range(N_EXPERTS)]
    gscore = []
    for g in range(N_EXPERT_GROUPS):
        mem = rows[g * PER_GROUP:(g + 1) * PER_GROUP]
        best = None
        for a in range(PER_GROUP):
            for c in range(a + 1, PER_GROUP):
                pair = mem[a] + mem[c]
                best = pair if best is None else jnp.maximum(best, pair)
        gscore.append(best)
    gsel = jnp.zeros_like(gscore[0], dtype=jnp.int32)
    gbest = gscore[0]
    for g in range(1, N_EXPERT_GROUPS):
        better = gscore[g] > gbest
        gsel = jnp.where(better, g, gsel)
        gbest = jnp.where(better, gscore[g], gbest)
    masked = [jnp.where(gsel == e // PER_GROUP, rows[e], NEG_INF) for e in range(N_EXPERTS)]

    def argbest(vals):
        idx = jnp.zeros_like(gsel)
        best = vals[0]
        for e in range(1, N_EXPERTS):
            better = vals[e] > best
            idx = jnp.where(better, e, idx)
            best = jnp.where(better, vals[e], best)
        return idx

    e1 = argbest(masked)
    e2 = argbest([jnp.where(e1 == e, -jnp.inf, masked[e]) for e in range(N_EXPERTS)])
    w1 = sum(jnp.where(e1 == e, affs[e], 0.0) for e in range(N_EXPERTS))
    w2 = sum(jnp.where(e2 == e, affs[e], 0.0) for e in range(N_EXPERTS))
    tot = w1 + w2
    w1 = w1 / tot
    w2 = w2 / tot
    comb = jnp.concatenate([jnp.where(e1 == e, w1, 0.0) + jnp.where(e2 == e, w2, 0.0)
                            for e in range(N_EXPERTS)], axis=0)
    return comb, gsel


def _out_kernel(x_ref, oa_ref, ob_ref, oc_ref, dc_ref, ds_ref, dw_ref,
                mixg_ref, wout_ref, g1_ref, n2g_ref, sc2_ref, sh2_ref, rw_ref, rb_ref,
                xo_ref, h2_ref, gsel_ref):
    o_a = jnp.concatenate([oa_ref[0, h] for h in range(HEADS)], axis=-1)
    o_b = ob_ref[0]
    o_c = oc_ref[0]
    o_d = dc_ref[0] + ds_ref[0] + dw_ref[0]
    mixed = jnp.concatenate([_group_norm(o, mixg_ref[i:i + 1, :]).astype(BF16)
                             for i, o in enumerate((o_a, o_b, o_c, o_d))], axis=-1)
    y = jnp.dot(mixed, wout_ref[...], preferred_element_type=F32)
    x = x_ref[0] + g1_ref[0] * y
    xo_ref[0] = x
    ms = jnp.mean(x * x, axis=-1, keepdims=True)
    h2 = (x * lax.rsqrt(ms + NORM_EPS)) * n2g_ref[...] * (1.0 + sc2_ref[0]) + sh2_ref[0]
    logits_t = lax.dot_general(rw_ref[...], h2.astype(BF16), (((1,), (1,)), ((), ())),
                               preferred_element_type=F32)
    comb_t, gsel = _route(logits_t, rb_ref[...])
    h2_ref[0] = jnp.concatenate([h2, _rows_to_lanes(comb_t, LANES)], axis=-1)
    gsel_ref[0] = gsel


MIXER_OUT_TILE = 512


def _mixer_out(x, o_a, o_b, o_c, o_dc, o_ds, o_dw, mix_g, w_out, g1, n2g, sc2, sh2, rw_t, rb):
    b, s, d = x.shape
    tm = min(MIXER_OUT_TILE, s)
    head_spec = pl.BlockSpec((1, HEADS, tm, HEAD_DIM), lambda i, m: (i, 0, m, 0))
    mixer_spec = pl.BlockSpec((1, tm, GROUP_WIDTH), lambda i, m: (i, m, 0))
    vec_b = pl.BlockSpec((1, 1, d), lambda i, m: (i, 0, 0))
    return pl.pallas_call(
        _out_kernel,
        grid=(b, s // tm),
        in_specs=[pl.BlockSpec((1, tm, d), lambda i, m: (i, m, 0)),
                  head_spec, mixer_spec, mixer_spec, mixer_spec, mixer_spec, mixer_spec,
                  pl.BlockSpec((N_MIXERS, GROUP_WIDTH), lambda i, m: (0, 0)),
                  pl.BlockSpec((N_MIXERS * GROUP_WIDTH, d), lambda i, m: (0, 0), pipeline_mode=pl.Buffered(1)),
                  vec_b,
                  pl.BlockSpec((1, d), lambda i, m: (0, 0)),
                  vec_b, vec_b,
                  pl.BlockSpec((N_EXPERTS, d), lambda i, m: (0, 0)),
                  pl.BlockSpec((N_EXPERTS, 1), lambda i, m: (0, 0))],
        out_specs=[pl.BlockSpec((1, tm, d), lambda i, m: (i, m, 0)),
                   pl.BlockSpec((1, tm, d + LANES), lambda i, m: (i, m, 0)),
                   pl.BlockSpec((1, 1, tm), lambda i, m: (i, 0, m))],
        out_shape=[jax.ShapeDtypeStruct((b, s, d), F32),
                   jax.ShapeDtypeStruct((b, s, d + LANES), F32),
                   jax.ShapeDtypeStruct((b, 1, s), jnp.int32)],
        compiler_params=_cparams(2),
        name="mixer_out",
    )(x, o_a, o_b, o_c, o_dc, o_ds, o_dw, mix_g, w_out, g1.reshape(b, 1, d), n2g.reshape(1, d),
      sc2.reshape(b, 1, d), sh2.reshape(b, 1, d), rw_t, rb.reshape(N_EXPERTS, 1))


EXPERT_TILE = 512


def _dispatch_plan(gsel, tm):
    gid = gsel.reshape(-1)
    n_tok = gid.shape[0]
    onehot = (gid[:, None] == jnp.arange(N_EXPERT_GROUPS, dtype=jnp.int32)[None, :]).astype(jnp.int32)
    csum = jnp.cumsum(onehot, axis=0)
    rank = jnp.sum((csum - onehot) * onehot, axis=1)
    padded = ((csum[-1] + tm - 1) // tm) * tm
    ends = jnp.cumsum(padded)
    dest = jnp.sum(onehot * (ends - padded)[None, :], axis=1) + rank
    n_rows = n_tok + N_EXPERT_GROUPS * tm
    src = jnp.zeros((n_rows,), jnp.int32).at[dest].set(jnp.arange(n_tok, dtype=jnp.int32), unique_indices=True)
    tile_start = jnp.arange(n_rows // tm, dtype=jnp.int32) * tm
    tile_gid = jnp.minimum(jnp.sum((tile_start[:, None] >= ends[None, :]).astype(jnp.int32), axis=1),
                           N_EXPERT_GROUPS - 1)
    return dest.astype(jnp.int32), src, tile_gid, (ends[-1] // tm).reshape(1).astype(jnp.int32)


def _start_row_gather(idx_ref, base, src_hbm, buf, sem):
    def issue(r, c):
        pltpu.make_async_copy(src_hbm.at[pl.ds(idx_ref[base + r], 1)], buf.at[pl.ds(r, 1)], sem).start()
        return c

    lax.fori_loop(0, buf.shape[0], issue, 0, unroll=8)


def _wait_row_gather(src_hbm, buf, sem):
    pltpu.make_async_copy(src_hbm.at[pl.ds(0, buf.shape[0])], buf, sem).wait()


def _expert_kernel(src_ref, gid_ref, nvalid_ref, h_hbm, wgu_ref, wd_ref, o_ref, rows_buf, sems, h_scr, acc_ref, *, tm):
    t = pl.program_id(0)
    e = pl.program_id(1)
    slot = t % 2
    n_valid = nvalid_ref[0]

    @pl.when(t < n_valid)
    def _():
        @pl.when(e == 0)
        def _():
            @pl.when(t == 0)
            def _():
                _start_row_gather(src_ref, 0, h_hbm, rows_buf.at[0], sems.at[0])

            _wait_row_gather(h_hbm, rows_buf.at[slot], sems.at[slot])

            @pl.when(t + 1 < n_valid)
            def _():
                _start_row_gather(src_ref, (t + 1) * tm, h_hbm, rows_buf.at[1 - slot], sems.at[1 - slot])

            h_scr[...] = rows_buf[slot, :, :D_MODEL].astype(BF16)
            acc_ref[...] = jnp.zeros_like(acc_ref)

        gu = jnp.dot(h_scr[...], wgu_ref[0], preferred_element_type=F32)
        gate = gu[:, :D_EXPERT]
        he = (gate * jax.nn.sigmoid(gate)) * gu[:, D_EXPERT:]
        cw = rows_buf[slot, :, D_MODEL:]
        col = lax.broadcasted_iota(jnp.int32, cw.shape, 1)
        w = jnp.sum(jnp.where(col == gid_ref[t] * PER_GROUP + e, cw, 0.0), axis=-1, keepdims=True)
        acc_ref[...] += w * jnp.dot(he.astype(BF16), wd_ref[0], preferred_element_type=F32)

        @pl.when(e == PER_GROUP - 1)
        def _():
            o_ref[...] = acc_ref[...]

    @pl.when((t >= n_valid) & (e == PER_GROUP - 1))
    def _():
        o_ref[...] = jnp.zeros_like(o_ref)


def _experts(h2ext, src, tile_gid, n_valid, w_gu, w_d, tm):
    n_rows = src.shape[0]
    d = D_MODEL
    expert = lambda t, e, src_ref, gid, nv: (gid[t] * PER_GROUP + e, 0, 0)
    return pl.pallas_call(
        functools.partial(_expert_kernel, tm=tm),
        grid_spec=pltpu.PrefetchScalarGridSpec(
            num_scalar_prefetch=3,
            grid=(n_rows // tm, PER_GROUP),
            in_specs=[pl.BlockSpec(memory_space=pl.ANY),
                      pl.BlockSpec((1, d, 2 * D_EXPERT), expert),
                      pl.BlockSpec((1, D_EXPERT, d), expert)],
            out_specs=pl.BlockSpec((tm, d), lambda t, e, src_ref, gid, nv: (t, 0)),
            scratch_shapes=[pltpu.VMEM((2, tm, d + LANES), F32), pltpu.SemaphoreType.DMA((2,)),
                            pltpu.VMEM((tm, d), BF16), pltpu.VMEM((tm, d), F32)]),
        out_shape=jax.ShapeDtypeStruct((n_rows, d), F32),
        compiler_params=_cparams(2),
        name="experts",
    )(src, tile_gid, n_valid, h2ext, w_gu, w_d)


COMBINE_TILE = 512


def _combine_kernel(dest_ref, ys_hbm, x_ref, g2_ref, fg_ref, o_ref, rows_buf, sems, *, tm, final_norm):
    i = pl.program_id(0)
    slot = i % 2

    @pl.when(i == 0)
    def _():
        _start_row_gather(dest_ref, 0, ys_hbm, rows_buf.at[0], sems.at[0])

    _wait_row_gather(ys_hbm, rows_buf.at[slot], sems.at[slot])

    @pl.when(i + 1 < pl.num_programs(0))
    def _():
        _start_row_gather(dest_ref, (i + 1) * tm, ys_hbm, rows_buf.at[1 - slot], sems.at[1 - slot])

    x = x_ref[0] + g2_ref[0] * rows_buf[slot]
    if final_norm:
        ms = jnp.mean(x * x, axis=-1, keepdims=True)
        x = (x * lax.rsqrt(ms + NORM_EPS)) * fg_ref[...]
    o_ref[0] = x


def _combine(x, ys, dest, g2, final_g, final_norm):
    b, s, d = x.shape
    tm = min(COMBINE_TILE, s)
    per_seq = s // tm
    return pl.pallas_call(
        functools.partial(_combine_kernel, tm=tm, final_norm=final_norm),
        grid_spec=pltpu.PrefetchScalarGridSpec(
            num_scalar_prefetch=1,
            grid=(b * per_seq,),
            in_specs=[pl.BlockSpec(memory_space=pl.ANY),
                      pl.BlockSpec((1, tm, d), lambda i, dest_ref: (i // per_seq, i % per_seq, 0)),
                      pl.BlockSpec((1, 1, d), lambda i, dest_ref: (i // per_seq, 0, 0)),
                      pl.BlockSpec((1, d), lambda i, dest_ref: (0, 0))],
            out_specs=pl.BlockSpec((1, tm, d), lambda i, dest_ref: (i // per_seq, i % per_seq, 0)),
            scratch_shapes=[pltpu.VMEM((2, tm, d), F32), pltpu.SemaphoreType.DMA((2,))]),
        out_shape=jax.ShapeDtypeStruct((b, s, d), F32),
        compiler_params=_cparams(1),
        name="moe_combine",
    )(dest, ys, x, g2.reshape(b, 1, d), final_g.reshape(1, d))


def _moe(x, h2ext, gsel, w_gu, w_d, g2, final_g, final_norm):
    b, s, _ = x.shape
    dest, src, tile_gid, n_valid = _dispatch_plan(gsel, EXPERT_TILE)
    ys = _experts(h2ext.reshape(b * s, -1), src, tile_gid, n_valid, w_gu, w_d, EXPERT_TILE)
    return _combine(x, ys, dest, g2, final_g, final_norm)


def _bias_tables(rel_bias, s):
    tabs = [rel_bias[:, i * HEADS:(i + 1) * HEADS].T for i in range(N_MIXERS)]
    t = {}
    t["a"] = jnp.stack([_band_bias_tiles(tabs[0], window // r, DILATED_WINDOW, r, 1)
                        for window, r in DILATED_CONFIGS], axis=1)
    t["b"] = _band_bias_tiles(tabs[1], SWA_WINDOW - 1, 128, 1, GQA)
    t["c"] = _flash_bias_tiles(tabs[2], s)
    t["ds"] = _flash_bias_tiles(tabs[3], s)
    t["dw"] = _band_bias_tiles(tabs[3], NSA_WINDOW - 1, NSA_WINDOW, 1, GQA)
    return t


def _token_mixer_branches(p, gates, p_cmp, bias, sink, pe, w1, w2):
    b, _, s, _ = p.shape
    o_a = _dilated_attention(p, bias["a"])
    o_b = _banded_attention(p, p, p, bias["b"], grid01=(b, 1),
                            q_map=lambda i, j: (i, QB // HEADS), kv_maps=(lambda i, j: (i, KB // KV_HEADS),
                                                                         lambda i, j: (i, VB // KV_HEADS)),
                            bias_map=lambda i, j: 0, r_heads=GQA, units=KV_HEADS, w=128, out_heads=HEADS,
                            sink=sink, name="swa")
    o_c = _moba_attention(p, bias["c"])
    kvc = _nsa_compress(p_cmp, pe, w1, w2)
    o_dc, pen = _nsa_compressed_attention(p, kvc, gates)
    o_ds = _nsa_selected_attention(p, pen, gates, bias["ds"])
    o_dw = _banded_attention(p, p, p, bias["dw"], grid01=(b, KV_HEADS),
                             q_map=lambda i, j: (i, QD // GQA + j), kv_maps=(lambda i, j: (i, KDW + j),
                                                                            lambda i, j: (i, VDW + j)),
                             bias_map=lambda i, j: j, r_heads=GQA, units=1, w=NSA_WINDOW, out_heads=HEADS,
                             gates=gates, gate_branch=2, name="nsa_window")
    return o_a, o_b, o_c, o_dc, o_ds, o_dw


def _split_w_in(w_in_l):
    w_heads = w_in_l[:, :PROJ_COLS].astype(BF16)
    w_gate = jnp.pad(w_in_l[:, PROJ_COLS:], ((0, 0), (0, LANES - N_GATE_COLS))).astype(BF16)
    return w_heads, w_gate


def _sink_rows(sink_l):
    return jnp.repeat(sink_l.astype(F32).reshape(KV_HEADS, GQA), BAND_TILE, axis=1)[..., None]


def kernel(x, c, rel_bias, router_w, router_bias, norm1_g, norm2_g, ada_w, ada_b, w_in, nsa_pe_k, nsa_pe_v,
           nsa_cmp_w1_k, nsa_cmp_w2_k, nsa_cmp_w1_v, nsa_cmp_w2_v, sinks, mix_norm_g, w_out, exp_w_gate,
           exp_w_up, exp_w_down, final_g):
    b, s, d = x.shape
    mod = _modulation(c, ada_w, ada_b)
    bias = _bias_tables(rel_bias, s)
    rw_t = router_w.T.astype(BF16)
    for l in range(DEPTH):
        sh1, sc1, g1, sh2, sc2, g2 = jnp.split(mod[l], 6, axis=-1)
        w_heads, w_gate = _split_w_in(w_in[l])
        p, gates, p_cmp = _in_projection(x, norm1_g[l], sc1, sh1, w_heads, w_gate)
        pe = jnp.stack([nsa_pe_k[l], nsa_pe_v[l]])
        w1 = jnp.stack([nsa_cmp_w1_k[l], nsa_cmp_w1_v[l]])
        w2 = jnp.stack([nsa_cmp_w2_k[l], nsa_cmp_w2_v[l]])
        o_a, o_b, o_c, o_dc, o_ds, o_dw = _token_mixer_branches(p, gates, p_cmp, bias, _sink_rows(sinks[l]), pe,
                                                                   w1, w2)
        x, h2ext, gsel = _mixer_out(x, o_a, o_b, o_c, o_dc, o_ds, o_dw, mix_norm_g[l],
                                    w_out[l].astype(BF16), g1, norm2_g[l], sc2, sh2, rw_t, router_bias)
        w_gu = jnp.concatenate([exp_w_gate[l], exp_w_up[l]], axis=-1).astype(BF16)
        x = _moe(x, h2ext, gsel, w_gu, exp_w_down[l].astype(BF16), g2, final_g, l == DEPTH - 1)
    return x
```
